```python
import math
import jax, jax.numpy as jnp
from jax import lax
import numpy as np

D_MODEL = 2048
BATCH = 2
SEQ = 4096
DEPTH = 2
DEC_BATCH = 32
DEC_SEQ = 1
PAST_LEN = 16384
PAGE_SIZE = 128

F32 = jnp.float32
NEG_INF = -1e30
BRANCH_W = D_MODEL // 2
N_BRANCH = 4
CONV_W = 4
BLOCK = 128
GDN_DK = 128
GDN_DV = 128
GDN_HEADS = BRANCH_W // GDN_DV
GDN_QKV = GDN_HEADS * (2 * GDN_DK + GDN_DV)
GDN_CHUNK = 64
DIL_GROUPS = ((128, 1), (512, 4), (2048, 16))
N_DIL = len(DIL_GROUPS)
DIL_HD = 128
DIL_QH = BRANCH_W // DIL_HD
DIL_KVH = 2
DIL_G = DIL_QH // DIL_KVH
LRU_W = BRANCH_W
LRU_BLOCKS = 8
LRU_BS = LRU_W // LRU_BLOCKS
LRU_C = 8.0
SWA_HD = 64
SWA_QH = BRANCH_W // SWA_HD
SWA_KVH = 2
SWA_G = SWA_QH // SWA_KVH
SWA_WINDOW = 128
ROPE_THETA = 150000.0
REL_BUCKETS = 32
REL_MAX_DIST = 2048
LN_EPS = 1e-5
RMS_EPS = 1e-6
DN_ALPHA = (2.0 * DEPTH) ** 0.25
DN_BETA = (8.0 * DEPTH) ** -0.25
IN_SIZES = (GDN_QKV, GDN_HEADS * GDN_DV, GDN_HEADS, GDN_HEADS,
            N_DIL * DIL_QH * DIL_HD, N_DIL * DIL_KVH * DIL_HD, N_DIL * DIL_KVH * DIL_HD, DIL_QH * DIL_HD,
            LRU_W, LRU_W,
            SWA_QH * SWA_HD, SWA_KVH * SWA_HD, SWA_KVH * SWA_HD, SWA_QH * SWA_HD,
            N_BRANCH * D_MODEL)
IN_SPLIT = tuple(int(v) for v in np.cumsum(IN_SIZES)[:-1])
D_IN = int(sum(IN_SIZES))

kernel_name = 'hybrid_gdn_dilated_rglru_swa_decode_step'


def l2norm(x):
    return x * lax.rsqrt(jnp.sum(x * x, axis=-1, keepdims=True) + 1e-6)


def rmsnorm(x, w):
    return x * lax.rsqrt(jnp.mean(x * x, axis=-1, keepdims=True) + RMS_EPS) * w


def layernorm(x, g, b):
    mu = jnp.mean(x, axis=-1, keepdims=True)
    xc = x - mu
    var = jnp.mean(xc * xc, axis=-1, keepdims=True)
    return xc * lax.rsqrt(var + LN_EPS) * g + b


def causal_conv(x, buf, w, b=None):
    T = x.shape[1]
    xe = jnp.concatenate([buf.astype(F32), x.astype(F32)], axis=1)
    w = w.astype(F32)
    y = xe[:, 0:T] * w[0]
    for j in range(1, CONV_W):
        y = y + xe[:, j:j + T] * w[j]
    if b is not None:
        y = y + b.astype(F32)
    return y, xe[:, T:]


def softmax_lse(s, sink):
    m = jnp.max(s, axis=-1)
    if sink is not None:
        m = jnp.maximum(m, sink)
    p = jnp.exp(s - m[..., None])
    den = jnp.sum(p, axis=-1)
    if sink is not None:
        den = den + jnp.exp(sink - m)
    return p / den[..., None], m + jnp.log(den)


def rel_bucket(dist):
    max_exact = REL_BUCKETS // 2
    n = dist.astype(F32)
    large = max_exact + (jnp.log(jnp.maximum(n, 1.0) / max_exact) / math.log(REL_MAX_DIST / max_exact)
                         * (REL_BUCKETS - max_exact)).astype(jnp.int32)
    large = jnp.minimum(large, REL_BUCKETS - 1)
    return jnp.where(dist < max_exact, dist, large)


def dil_offset_bias(rel_bias, gi, win, dil):
    J = win // dil + 1
    b = rel_bias[rel_bucket(dil * jnp.arange(J, dtype=jnp.int32))]
    return b[:, gi * DIL_QH:(gi + 1) * DIL_QH].reshape(J, DIL_KVH, DIL_G).astype(F32)


def to_sub(x, d):
    N, L = x.shape[:2]
    x = jnp.swapaxes(x.reshape((N, L // d, d) + x.shape[2:]), 1, 2)
    return x.reshape((N * d, L // d) + x.shape[3:])


def from_sub(x, n, d):
    Ld = x.shape[1]
    x = jnp.swapaxes(x.reshape((n, d, Ld) + x.shape[2:]), 1, 2)
    return x.reshape((n, Ld * d) + x.shape[3:])


def rope(x, pos):
    half = x.shape[-1] // 2
    inv = ROPE_THETA ** (-jnp.arange(half, dtype=F32) / half)
    ang = pos.astype(F32)[:, None] * inv[None, :]
    ang = ang.reshape((1, ang.shape[0]) + (1,) * (x.ndim - 3) + (half,))
    c, s = jnp.cos(ang), jnp.sin(ang)
    x1, x2 = x[..., :half], x[..., half:]
    return jnp.concatenate([x1 * c - x2 * s, x2 * c + x1 * s], axis=-1)


def banded_attn(q, kv, window, off_bias, sink):
    N, L, Hk, G, hd = q.shape
    nb = -(-L // BLOCK)
    pad = nb * BLOCK - L
    q = jnp.pad(q.astype(F32), ((0, 0), (0, pad), (0, 0), (0, 0), (0, 0)))
    kv = jnp.pad(kv.astype(F32), ((0, 0), (0, pad), (0, 0), (0, 0), (0, 0)))
    qb = q.reshape(N, nb, BLOCK, Hk, G, hd)
    kvb = kv.reshape(N, nb, BLOCK, 2, Hk, hd)
    kvp = jnp.pad(kvb, ((0, 0), (1, 0), (0, 0), (0, 0), (0, 0), (0, 0)))[:, :-1]
    kvc = jnp.concatenate([kvp, kvb], axis=2)
    s = jnp.einsum('nbqhgd,nbkhd->nbhgqk', qb, kvc[:, :, :, 0]) * hd ** -0.5
    kj = jnp.arange(2 * BLOCK)
    off = jnp.arange(BLOCK)[:, None] + BLOCK - kj[None, :]
    first = (jnp.arange(nb)[:, None, None] > 0) | (kj[None, None, :] >= BLOCK)
    valid = (off >= 0) & (off <= window) & first
    if off_bias is not None:
        bias = off_bias[jnp.clip(off, 0, window)]
        s = s + jnp.transpose(bias, (2, 3, 0, 1))
    s = jnp.where(valid[:, None, None], s, NEG_INF)
    p, lse = softmax_lse(s, None if sink is None else sink[:, :, None])
    o = jnp.einsum('nbhgqk,nbkhd->nbqhgd', p, kvc[:, :, :, 1])
    o = o.reshape(N, nb * BLOCK, Hk, G, hd)[:, :L]
    lse = jnp.transpose(lse, (0, 1, 4, 2, 3)).reshape(N, nb * BLOCK, Hk, G)[:, :L]
    return o, lse


def gathered_attn(q, kv_ext, n_buf, window, dil, off_bias, sink):
    N, T, Hk, G, hd = q.shape
    J = window // dil + 1
    idx = n_buf + jnp.arange(T)[:, None] - dil * jnp.arange(J)[None, :]
    valid = idx >= 0
    kvg = kv_ext[:, jnp.maximum(idx, 0)].astype(F32)
    s = jnp.einsum('nthgd,ntjhd->nthgj', q.astype(F32), kvg[:, :, :, 0]) * hd ** -0.5
    if off_bias is not None:
        s = s + jnp.transpose(off_bias, (1, 2, 0))
    s = jnp.where(valid[:, None, None, :], s, NEG_INF)
    p, lse = softmax_lse(s, sink)
    o = jnp.einsum('nthgj,ntjhd->nthgd', p, kvg[:, :, :, 1])
    return o, lse


def gdn_chunked(q, k, v, beta, g, S0):
    N, T, H, DK = q.shape
    DV = v.shape[-1]
    C = GDN_CHUNK
    nc = T // C

    def blk(t):
        return jnp.moveaxis(t.reshape(N, nc, C, H, -1), 3, 1)
    q, k, v = blk(q), blk(k), blk(v)
    beta = blk(beta[..., None])[..., 0]
    g = blk(g[..., None])[..., 0]
    gc = jnp.cumsum(g, axis=-1)
    incl = jnp.tril(jnp.ones((C, C), bool))
    strict = jnp.tril(jnp.ones((C, C), bool), -1)
    decay = jnp.where(incl, jnp.exp(jnp.where(incl, gc[..., :, None] - gc[..., None, :], 0.0)), 0.0)
    kb = k * beta[..., None]
    lmat = jnp.where(strict, jnp.einsum('nhcid,nhcjd->nhcij', kb, k) * decay, 0.0)
    amat = lmat + jnp.eye(C, dtype=F32)
    rhs = jnp.concatenate([v * beta[..., None], kb * jnp.exp(gc)[..., None]], axis=-1)
    sol = lax.linalg.triangular_solve(amat, rhs, left_side=True, lower=True)
    u, w = sol[..., :DV], sol[..., DV:]
    qk = jnp.einsum('nhcid,nhcjd->nhcij', q, k) * decay
    qg = q * jnp.exp(gc)[..., None]
    gl = gc[..., -1]
    kd = k * jnp.exp(gl[..., None] - gc)[..., None]

    def step(S, inp):
        qg_c, qk_c, u_c, w_c, kd_c, gl_c = inp
        v_new = u_c - jnp.einsum('nhck,nhkv->nhcv', w_c, S)
        o = jnp.einsum('nhck,nhkv->nhcv', qg_c, S) + jnp.einsum('nhij,nhjv->nhiv', qk_c, v_new)
        S = S * jnp.exp(gl_c)[..., None, None] + jnp.einsum('nhck,nhcv->nhkv', kd_c, v_new)
        return S, o
    xs = tuple(jnp.moveaxis(t, 2, 0) for t in (qg, qk, u, w, kd, gl))
    S, o = lax.scan(step, S0, xs)
    o = jnp.transpose(o, (1, 0, 3, 2, 4)).reshape(N, T, H, DV)
    return o, S


def gdn_recurrent(q, k, v, beta, g, S0):
    def step(S, inp):
        q_t, k_t, v_t, b_t, g_t = inp
        S = S * jnp.exp(g_t)[..., None, None]
        v_old = jnp.einsum('nhk,nhkv->nhv', k_t, S)
        S = S + jnp.einsum('nhk,nhv->nhkv', k_t * b_t[..., None], v_t - v_old)
        return S, jnp.einsum('nhk,nhkv->nhv', q_t, S)
    xs = tuple(jnp.moveaxis(t, 1, 0) for t in (q, k, v, beta, g))
    S, o = lax.scan(step, S0, xs)
    return jnp.moveaxis(o, 0, 1), S


def rglru(x, h0, wa, ba, wx, bx, lam):
    N, T, W = x.shape
    xb = x.reshape(N, T, LRU_BLOCKS, LRU_BS)
    r = jax.nn.sigmoid(jnp.einsum('ntbi,bij->ntbj', xb, wa).reshape(N, T, W) + ba)
    i = jax.nn.sigmoid(jnp.einsum('ntbi,bij->ntbj', xb, wx).reshape(N, T, W) + bx)
    log_a = -LRU_C * r * jax.nn.softplus(-lam)
    a = jnp.exp(log_a)
    b = jnp.sqrt(-jnp.expm1(2.0 * log_a)) * (i * x)
    b = b.at[:, 0].add(a[:, 0] * h0)

    def comb(e1, e2):
        return e1[0] * e2[0], e2[0] * e1[1] + e2[1]
    _, h = lax.associative_scan(comb, (a, b), axis=1)
    return h, h[:, -1]


def mixer_layer(x, prompt, pos0, gdn_s, gdn_buf, lru_h, lru_buf, dil_bufs, swa_buf, rel_bias,
                w_in, gdn_conv_w, gdn_a_log, gdn_dt_bias, gdn_norm_w, lru_conv_w, lru_conv_b,
                lru_wa, lru_ba, lru_wx, lru_bx, lru_lambda, swa_sink, w_branch, w_out, ln_g, ln_b):
    N, T, _ = x.shape
    dt = x.dtype
    h = jnp.einsum('ntd,de->nte', x, w_in)
    (a_qkv, a_z, a_b, a_a, b_q, b_k, b_v, b_g, c_x, c_g,
     d_q, d_k, d_v, d_g, m_g) = jnp.split(h, IN_SPLIT, axis=-1)

    qkv, new_gdn_buf = causal_conv(a_qkv, gdn_buf, gdn_conv_w)
    qkv = jax.nn.silu(qkv)
    q, k, v = jnp.split(qkv, (GDN_HEADS * GDN_DK, 2 * GDN_HEADS * GDN_DK), axis=-1)
    q = l2norm(q.reshape(N, T, GDN_HEADS, GDN_DK)) * GDN_DK ** -0.5
    k = l2norm(k.reshape(N, T, GDN_HEADS, GDN_DK))
    v = v.reshape(N, T, GDN_HEADS, GDN_DV)
    beta = jax.nn.sigmoid(a_b.astype(F32))
    g = -jnp.exp(gdn_a_log.astype(F32)) * jax.nn.softplus(a_a.astype(F32) + gdn_dt_bias.astype(F32))
    if prompt:
        o_a, new_gdn_s = gdn_chunked(q, k, v, beta, g, gdn_s.astype(F32))
    else:
        o_a, new_gdn_s = gdn_recurrent(q, k, v, beta, g, gdn_s.astype(F32))
    y_a = rmsnorm(o_a, gdn_norm_w.astype(F32)).reshape(N, T, -1) * jax.nn.silu(a_z.astype(F32))

    bq = b_q.astype(F32).reshape(N, T, N_DIL, DIL_KVH, DIL_G, DIL_HD)
    bkv = jnp.stack([b_k.reshape(N, T, N_DIL, DIL_KVH, DIL_HD),
                     b_v.reshape(N, T, N_DIL, DIL_KVH, DIL_HD)], axis=3)
    outs, lses, new_dil = [], [], []
    for gi, (win, dil) in enumerate(DIL_GROUPS):
        ob = dil_offset_bias(rel_bias, gi, win, dil)
        qg, kvg = bq[:, :, gi], bkv[:, :, gi]
        if prompt:
            o_g, l_g = banded_attn(to_sub(qg, dil), to_sub(kvg, dil), win // dil, ob, None)
            o_g, l_g = from_sub(o_g, N, dil), from_sub(l_g, N, dil)
            new_dil.append(kvg[:, T - min(win, T):])
        else:
            buf = dil_bufs[gi]
            n_buf = buf.shape[1]
            ext = jnp.concatenate([buf, kvg.astype(buf.dtype)], axis=1)
            o_g, l_g = gathered_attn(qg, ext, n_buf, win, dil, ob, None)
            new_dil.append(ext[:, ext.shape[1] - n_buf:])
        outs.append(o_g)
        lses.append(l_g)
    wts = jax.nn.softmax(jnp.stack(lses), axis=0)
    o_b = jnp.sum(wts[..., None] * jnp.stack(outs), axis=0)
    y_b = o_b.reshape(N, T, -1) * jax.nn.silu(b_g.astype(F32))

    cx, new_lru_buf = causal_conv(c_x, lru_buf, lru_conv_w, lru_conv_b)
    hs, new_lru_h = rglru(cx, lru_h.astype(F32), lru_wa.astype(F32), lru_ba.astype(F32),
                          lru_wx.astype(F32), lru_bx.astype(F32), lru_lambda.astype(F32))
    y_c = hs * jax.nn.silu(c_g.astype(F32))

    pos = pos0 + jnp.arange(T)
    sq = rope(d_q.astype(F32).reshape(N, T, SWA_KVH, SWA_G, SWA_HD), pos)
    sk = rope(d_k.astype(F32).reshape(N, T, SWA_KVH, SWA_HD), pos).astype(dt)
    skv = jnp.stack([sk, d_v.reshape(N, T, SWA_KVH, SWA_HD)], axis=2)
    sink = swa_sink.astype(F32).reshape(SWA_KVH, SWA_G)
    if prompt:
        o_d, _ = banded_attn(sq, skv, SWA_WINDOW, None, sink)
        new_swa = skv[:, T - min(SWA_WINDOW, T):]
    else:
        n_buf = swa_buf.shape[1]
        ext = jnp.concatenate([swa_buf, skv.astype(swa_buf.dtype)], axis=1)
        o_d, _ = gathered_attn(sq, ext, n_buf, SWA_WINDOW, 1, None, sink)
        new_swa = ext[:, ext.shape[1] - n_buf:]
    y_d = o_d.reshape(N, T, -1) * jax.nn.silu(d_g.astype(F32))

    ys = jnp.stack([y_a, y_b, y_c, y_d], axis=2).astype(dt)
    br = jnp.einsum('ntbw,bwd->ntbd', ys, w_branch)
    gates = jax.nn.sigmoid(m_g.astype(F32).reshape(N, T, N_BRANCH, D_MODEL))
    merged = jnp.sum(gates * br.astype(F32), axis=2).astype(dt)
    f = jnp.einsum('ntd,de->nte', merged, w_out)
    x_new = layernorm(DN_ALPHA * x.astype(F32) + f.astype(F32), ln_g.astype(F32), ln_b.astype(F32)).astype(dt)
    return x_new, (new_gdn_s, new_gdn_buf, new_dil[0], new_dil[1], new_dil[2], new_swa, new_lru_h, new_lru_buf)


def setup_inputs(seed: int = 0) -> dict:
    key = jax.random.key(seed)
    ks = jax.random.split(key, 32)

    def nrm(i, shape, scale):
        return scale * jax.random.normal(ks[i], shape, F32)
    dt0 = jnp.exp(jax.random.uniform(ks[20], (DEPTH, GDN_HEADS), F32, math.log(1e-3), math.log(1e-1)))
    a_lru = jax.random.uniform(ks[21], (DEPTH, LRU_W), F32, 0.9, 0.999)
    s_lru = a_lru ** (1.0 / LRU_C)
    w0, w1, w2 = DIL_GROUPS[0][0], DIL_GROUPS[1][0], DIL_GROUPS[2][0]
    return {
        'x_prompt': nrm(0, (BATCH, SEQ, D_MODEL), 1.0),
        'x_sample': nrm(1, (DEC_BATCH, DEC_SEQ, D_MODEL), 1.0),
        'state_gdn': nrm(2, (DEPTH, DEC_BATCH, GDN_HEADS, GDN_DK, GDN_DV), 0.1),
        'state_gdn_conv': nrm(3, (DEPTH, DEC_BATCH, CONV_W - 1, GDN_QKV), 1.0),
        'cache_dil_w128': nrm(4, (DEPTH, DEC_BATCH, min(w0, PAST_LEN), 2, DIL_KVH, DIL_HD), 1.0),
        'cache_dil_w512': nrm(5, (DEPTH, DEC_BATCH, min(w1, PAST_LEN), 2, DIL_KVH, DIL_HD), 1.0),
        'cache_dil_w2048': nrm(6, (DEPTH, DEC_BATCH, min(w2, PAST_LEN), 2, DIL_KVH, DIL_HD), 1.0),
        'cache_swa': nrm(7, (DEPTH, DEC_BATCH, min(SWA_WINDOW, PAST_LEN), 2, SWA_KVH, SWA_HD), 1.0),
        'state_rglru': nrm(8, (DEPTH, DEC_BATCH, LRU_W), 0.5),
        'state_rglru_conv': nrm(9, (DEPTH, DEC_BATCH, CONV_W - 1, LRU_W), 1.0),
        'w_in': nrm(10, (DEPTH, D_MODEL, D_IN), D_MODEL ** -0.5),
        'gdn_conv_w': nrm(11, (DEPTH, CONV_W, GDN_QKV), 0.5),
        'gdn_a_log': jnp.log(jax.random.uniform(ks[12], (DEPTH, GDN_HEADS), F32, 1.0, 16.0)),
        'gdn_dt_bias': dt0 + jnp.log(-jnp.expm1(-dt0)),
        'gdn_norm_w': 1.0 + nrm(13, (DEPTH, GDN_DV), 0.02),
        'lru_conv_w': nrm(14, (DEPTH, CONV_W, LRU_W), 0.5),
        'lru_conv_b': nrm(15, (DEPTH, LRU_W), 0.02),
        'lru_wa': nrm(16, (DEPTH, LRU_BLOCKS, LRU_BS, LRU_BS), LRU_BS ** -0.5),
        'lru_ba': nrm(17, (DEPTH, LRU_W), 0.02),
        'lru_wx': nrm(18, (DEPTH, LRU_BLOCKS, LRU_BS, LRU_BS), LRU_BS ** -0.5),
        'lru_bx': nrm(19, (DEPTH, LRU_W), 0.02),
        'lru_lambda': jnp.log(s_lru) - jnp.log1p(-s_lru),
        'swa_sink': nrm(22, (DEPTH, SWA_QH), 0.5),
        'rel_bias': nrm(23, (REL_BUCKETS, N_DIL * DIL_QH), 0.5),
        'w_branch': nrm(24, (DEPTH, N_BRANCH, BRANCH_W, D_MODEL), BRANCH_W ** -0.5 * DN_BETA),
        'w_out': nrm(25, (DEPTH, D_MODEL, D_MODEL), D_MODEL ** -0.5 * DN_BETA),
        'ln_g': 1.0 + nrm(26, (DEPTH, D_MODEL), 0.02),
        'ln_b': nrm(27, (DEPTH, D_MODEL), 0.02),
    }


def reference(x_prompt, x_sample, state_gdn, state_gdn_conv, cache_dil_w128, cache_dil_w512,
              cache_dil_w2048, cache_swa, state_rglru, state_rglru_conv, w_in, gdn_conv_w, gdn_a_log,
              gdn_dt_bias, gdn_norm_w, lru_conv_w, lru_conv_b, lru_wa, lru_ba, lru_wx, lru_bx, lru_lambda,
              swa_sink, rel_bias, w_branch, w_out, ln_g, ln_b):
    xp, xs = x_prompt, x_sample
    NP = x_prompt.shape[0]
    new_p = [[] for _ in range(8)]
    new_s = [[] for _ in range(8)]
    for l in range(DEPTH):
        lw = (w_in[l], gdn_conv_w[l], gdn_a_log[l], gdn_dt_bias[l], gdn_norm_w[l], lru_conv_w[l],
              lru_conv_b[l], lru_wa[l], lru_ba[l], lru_wx[l], lru_bx[l], lru_lambda[l], swa_sink[l],
              w_branch[l], w_out[l], ln_g[l], ln_b[l])
        xp, sp = mixer_layer(xp, True, 0,
                             jnp.zeros((NP, GDN_HEADS, GDN_DK, GDN_DV), F32),
                             jnp.zeros((NP, CONV_W - 1, GDN_QKV), F32),
                             jnp.zeros((NP, LRU_W), F32),
                             jnp.zeros((NP, CONV_W - 1, LRU_W), F32),
                             None, None, rel_bias, *lw)
        xs, ss = mixer_layer(xs, False, PAST_LEN, state_gdn[l], state_gdn_conv[l], state_rglru[l],
                             state_rglru_conv[l], (cache_dil_w128[l], cache_dil_w512[l], cache_dil_w2048[l]),
                             cache_swa[l], rel_bias, *lw)
        for i in range(8):
            new_p[i].append(sp[i])
            new_s[i].append(ss[i])
    p = [jnp.stack(v) for v in new_p]
    s = [jnp.stack(v) for v in new_s]
    return (xp, xs, p[0], s[0], p[1], s[1], p[2], s[2], p[3], s[3], p[4], s[4], p[5], s[5], p[6], s[6], p[7], s[7])
```

```python
import functools
import math

import jax
import jax.numpy as jnp
import numpy as np
from jax import lax
from jax.experimental import pallas as pl
from jax.experimental.pallas import tpu as pltpu

F32 = jnp.float32
BF16 = jnp.bfloat16
NEG_INF = -1e30

D_MODEL = 2048
DEPTH = 2
PAST_LEN = 16384
BRANCH_W = D_MODEL // 2
N_BRANCH = 4
CONV_W = 4
BLOCK = 128
GDN_DK = 128
GDN_DV = 128
GDN_HEADS = BRANCH_W // GDN_DV
GDN_QKV = GDN_HEADS * (2 * GDN_DK + GDN_DV)
DIL_GROUPS = ((128, 1), (512, 4), (2048, 16))
N_DIL = len(DIL_GROUPS)
DIL_HD = 128
DIL_QH = BRANCH_W // DIL_HD
DIL_KVH = 2
DIL_G = DIL_QH // DIL_KVH
LRU_W = BRANCH_W
LRU_BLOCKS = 8
LRU_BS = LRU_W // LRU_BLOCKS
LRU_C = 8.0
SWA_HD = 64
SWA_QH = BRANCH_W // SWA_HD
SWA_KVH = 2
SWA_G = SWA_QH // SWA_KVH
SWA_WINDOW = 128
ROPE_THETA = 150000.0
REL_BUCKETS = 32
REL_MAX_DIST = 2048
LN_EPS = 1e-5
RMS_EPS = 1e-6
DN_ALPHA = (2.0 * DEPTH) ** 0.25

IN_SIZES = (GDN_QKV, GDN_HEADS * GDN_DV, GDN_HEADS, GDN_HEADS,
            N_DIL * DIL_QH * DIL_HD, N_DIL * DIL_KVH * DIL_HD, N_DIL * DIL_KVH * DIL_HD, DIL_QH * DIL_HD,
            LRU_W, LRU_W,
            SWA_QH * SWA_HD, SWA_KVH * SWA_HD, SWA_KVH * SWA_HD, SWA_QH * SWA_HD,
            N_BRANCH * D_MODEL)
_OFF = [0]
for _s in IN_SIZES:
    _OFF.append(_OFF[-1] + _s)
(O_AQKV, O_AZ, O_AB, O_AA, O_BQ, O_BK, O_BV, O_BG, O_CX, O_CG, O_DQ, O_DK, O_DV, O_DG, O_MG, O_END) = _OFF

LANE = 128
U_MG, U_AZ, U_AQKV, U_BQ, U_BG, U_CX, U_CG, U_DQ, U_DG, U_BK, U_BV, U_DK, U_DV, U_AB = (
    0, 64, 72, 96, 120, 128, 136, 144, 152, 160, 166, 172, 173, 174)
N_UNITS = 175
D_INP = N_UNITS * LANE
GDN_CHUNK = 128
VMEM_LIMIT = 56 * 1024 * 1024


def _cparams(sem):
    return pltpu.CompilerParams(dimension_semantics=sem, vmem_limit_bytes=VMEM_LIMIT)


def _sigmoid(x):
    return 1.0 / (1.0 + jnp.exp(-x))


def _silu(x):
    return x * _sigmoid(x)


def _softplus(x):
    return jnp.maximum(x, 0.0) + jnp.log1p(jnp.exp(-jnp.abs(x)))


def _dot(a, b):
    return jnp.dot(a.astype(BF16), b.astype(BF16), preferred_element_type=F32)


def _dot_nt(a, b):
    return lax.dot_general(a.astype(BF16), b.astype(BF16), (((1,), (1,)), ((), ())),
                           preferred_element_type=F32)


def _dot_hi(a, b):
    return jnp.dot(a, b, preferred_element_type=F32, precision=lax.Precision.HIGHEST)


def _dot_nt_hi(a, b):
    return lax.dot_general(a, b, (((1,), (1,)), ((), ())), preferred_element_type=F32,
                           precision=lax.Precision.HIGHEST)


def _matmul_kernel(x_ref, w_ref, o_ref):
    o_ref[...] = jnp.dot(x_ref[...], w_ref[...], preferred_element_type=F32)


def _matmul(x, w, tm, tn):
    m, k = x.shape
    n = w.shape[1]
    return pl.pallas_call(
        _matmul_kernel,
        out_shape=jax.ShapeDtypeStruct((m, n), F32),
        grid=(m // tm, n // tn),
        in_specs=[pl.BlockSpec((tm, k), lambda i, j: (i, 0)),
                  pl.BlockSpec((k, tn), lambda i, j: (0, j))],
        out_specs=pl.BlockSpec((tm, tn), lambda i, j: (i, j)),
        compiler_params=_cparams(("parallel", "arbitrary")),
        name="in_proj",
    )(x, w)


def _conv_rows(x, prev8, cw):
    row8 = lax.broadcasted_iota(jnp.int32, (8, x.shape[1]), 0)
    y = x * cw[CONV_W - 1:CONV_W]
    for k in range(1, CONV_W):
        xk = pltpu.roll(x, k, 0)
        fk = pltpu.roll(prev8, k, 0)
        head = jnp.where(row8 < k, fk, xk[0:8])
        xk = jnp.concatenate([head, xk[8:]], axis=0)
        y = y + xk * cw[CONV_W - 1 - k:CONV_W - k]
    return y


def _tri_inv(lmat, ri, ci):
    c = lmat.shape[0]
    eye = (ri == ci).astype(F32)
    d = jnp.where((ri >> 4) == (ci >> 4), lmat, 0.0)
    x = eye - d
    p = _dot_hi(d, d)
    x = x + _dot_hi(x, p)
    p = _dot_hi(p, p)
    x = x + _dot_hi(x, p)
    p = _dot_hi(p, p)
    x = x + _dot_hi(x, p)
    sh = 4
    while (1 << sh) < c:
        lower_left = ((ri >> (sh + 1)) == (ci >> (sh + 1))) & (((ri >> sh) & 1) == 1) & (((ci >> sh) & 1) == 0)
        cm = jnp.where(lower_left, lmat, 0.0)
        x = x - _dot_hi(x, _dot_hi(cm, x))
        sh += 1
    return x


def _gdn_kernel(qkv_ref, z_ref, ab_ref, cw_ref, arow_ref, dtb_ref, nw_ref, y_ref, s_ref, prev_ref):
    cidx = pl.program_id(1)
    C = GDN_CHUNK

    @pl.when(cidx == 0)
    def _():
        s_ref[...] = jnp.zeros_like(s_ref)
        prev_ref[...] = jnp.zeros_like(prev_ref)

    x = qkv_ref[...]
    act = _silu(_conv_rows(x, prev_ref[...], cw_ref[...]))
    prev_ref[...] = x[C - 8:C]

    ab = ab_ref[...]
    beta_t = _sigmoid(ab)
    g_t = -arow_ref[...] * _softplus(ab + dtb_ref[...])
    ri = lax.broadcasted_iota(jnp.int32, (C, C), 0)
    ci = lax.broadcasted_iota(jnp.int32, (C, C), 1)
    incl = ri >= ci
    strict = ri > ci
    gc_t = _dot_hi(incl.astype(F32), g_t)
    gc_tt = gc_t.T
    egc_t = jnp.exp(gc_t)
    nw = nw_ref[...]
    z = z_ref[...]

    for h in range(GDN_HEADS):
        q = act[:, h * GDN_DK:(h + 1) * GDN_DK]
        k = act[:, (GDN_HEADS + h) * GDN_DK:(GDN_HEADS + h + 1) * GDN_DK]
        v = act[:, (2 * GDN_HEADS + h) * GDN_DK:(2 * GDN_HEADS + h + 1) * GDN_DK]
        q = q * lax.rsqrt(jnp.sum(q * q, axis=-1, keepdims=True) + 1e-6) * (GDN_DK ** -0.5)
        k = k * lax.rsqrt(jnp.sum(k * k, axis=-1, keepdims=True) + 1e-6)
        beta = beta_t[:, h:h + 1]
        gcc = gc_t[:, GDN_HEADS + h:GDN_HEADS + h + 1]
        gcr = gc_tt[GDN_HEADS + h:GDN_HEADS + h + 1, :]
        egc = egc_t[:, GDN_HEADS + h:GDN_HEADS + h + 1]
        decay = jnp.where(incl, jnp.exp(jnp.where(incl, gcc - gcr, 0.0)), 0.0)
        kb = k * beta
        lmat = jnp.where(strict, _dot_nt(kb, k) * decay, 0.0)
        tinv = _tri_inv(lmat, ri, ci)
        rhs = jnp.concatenate([v * beta, kb * egc], axis=1)
        sol = _dot_hi(tinv, rhs)
        u = sol[:, :GDN_DV]
        w = sol[:, GDN_DV:]
        qk = _dot_nt(q, k) * decay
        qg = q * egc
        glc = gc_t[C - 1:C, GDN_HEADS + h:GDN_HEADS + h + 1]
        kd = k * jnp.exp(glc - gcc)
        s = s_ref[0, h]
        v_new = u - _dot(w, s)
        o = _dot(qg, s) + _dot(qk, v_new)
        s_ref[0, h] = s * jnp.exp(glc) + _dot(kd.T, v_new)
        y = o * lax.rsqrt(jnp.mean(o * o, axis=-1, keepdims=True) + RMS_EPS) * nw
        y = y * _silu(z[:, h * GDN_DV:(h + 1) * GDN_DV])
        y_ref[:, h * GDN_DV:(h + 1) * GDN_DV] = y.astype(BF16)


def _gdn_prompt(hp, n_seq, t_len, conv_w, arow, dtb, norm_w):
    C = GDN_CHUNK
    nc = t_len // C
    return pl.pallas_call(
        _gdn_kernel,
        out_shape=(jax.ShapeDtypeStruct((n_seq * t_len, BRANCH_W), BF16),
                   jax.ShapeDtypeStruct((n_seq, GDN_HEADS, GDN_DK, GDN_DV), F32)),
        grid=(n_seq, nc),
        in_specs=[pl.BlockSpec((C, GDN_QKV), lambda n, c: (n * nc + c, U_AQKV * LANE // GDN_QKV)),
                  pl.BlockSpec((C, BRANCH_W), lambda n, c: (n * nc + c, U_AZ * LANE // BRANCH_W)),
                  pl.BlockSpec((C, LANE), lambda n, c: (n * nc + c, U_AB)),
                  pl.BlockSpec((CONV_W, GDN_QKV), lambda n, c: (0, 0)),
                  pl.BlockSpec((1, LANE), lambda n, c: (0, 0)),
                  pl.BlockSpec((1, LANE), lambda n, c: (0, 0)),
                  pl.BlockSpec((1, GDN_DV), lambda n, c: (0, 0))],
        out_specs=(pl.BlockSpec((C, BRANCH_W), lambda n, c: (n * nc + c, 0)),
                   pl.BlockSpec((1, GDN_HEADS, GDN_DK, GDN_DV), lambda n, c: (n, 0, 0, 0))),
        scratch_shapes=[pltpu.VMEM((8, GDN_QKV), F32)],
        compiler_params=_cparams(("parallel", "arbitrary")),
        name="gdn_prompt",
    )(hp, hp, hp, conv_w, arow, dtb, norm_w)


def _attn_kernel(*refs, n_kvh, n_g, hd, window, has_bias, has_sink, has_gate):
    refs = list(refs)
    q_ref, kc_ref, kp_ref, vc_ref, vp_ref = refs[:5]
    pos = 5
    bias_ref = sink_ref = gate_ref = None
    if has_bias:
        bias_ref = refs[pos]; pos += 1
    if has_sink:
        sink_ref = refs[pos]; pos += 1
    if has_gate:
        gate_ref = refs[pos]; pos += 1
    o_ref = refs[pos]; pos += 1
    lse_ref = None if has_gate else refs[pos]

    i = pl.program_id(1)
    rows = n_g * BLOCK
    qi = lax.broadcasted_iota(jnp.int32, (rows, 2 * BLOCK), 0) & (BLOCK - 1)
    kj = lax.broadcasted_iota(jnp.int32, (rows, 2 * BLOCK), 1)
    off = qi + BLOCK - kj
    kmin = jnp.where(i > 0, 0, BLOCK)
    valid = (off >= 0) & (off <= window) & (kj >= kmin)
    q = q_ref[0]
    kc = kc_ref[0]
    kp = kp_ref[0]
    vc = vc_ref[0]
    vp = vp_ref[0]
    scale = hd ** -0.5
    for kvh in range(n_kvh):
        qh = jnp.concatenate([q[:, (kvh * n_g + g) * hd:(kvh * n_g + g + 1) * hd] for g in range(n_g)], axis=0)
        kcat = jnp.concatenate([kp[:, kvh * hd:(kvh + 1) * hd], kc[:, kvh * hd:(kvh + 1) * hd]], axis=0)
        vcat = jnp.concatenate([vp[:, kvh * hd:(kvh + 1) * hd], vc[:, kvh * hd:(kvh + 1) * hd]], axis=0)
        s = _dot_nt(qh, kcat) * scale
        if has_bias:
            s = s + bias_ref[kvh * n_g:(kvh + 1) * n_g].reshape(rows, 2 * BLOCK)
        s = jnp.where(valid, s, NEG_INF)
        m = jnp.max(s, axis=-1, keepdims=True)
        if has_sink:
            sink = sink_ref[kvh]
            m = jnp.maximum(m, sink)
        p = jnp.exp(s - m)
        den = jnp.sum(p, axis=-1, keepdims=True)
        if has_sink:
            den = den + jnp.exp(sink - m)
        o = _dot(p, vcat) / den
        for g in range(n_g):
            hq = kvh * n_g + g
            og = o[g * BLOCK:(g + 1) * BLOCK]
            if has_gate:
                og = og * _silu(gate_ref[0, :, hq * hd:(hq + 1) * hd])
            o_ref[0, :, hq * hd:(hq + 1) * hd] = og.astype(o_ref.dtype)
        if lse_ref is not None:
            lse = m + jnp.log(den)
            for g in range(n_g):
                hq = kvh * n_g + g
                lse_ref[0, :, hq:hq + 1] = lse[g * BLOCK:(g + 1) * BLOCK]


def _banded_attn(q, qblk, k, kblk, v, vblk, n_seq, length, n_kvh, n_g, hd, window,
                 bias=None, sink=None, gate=None, gblk=0, name="attn"):
    wq = n_kvh * n_g * hd
    wk = n_kvh * hd
    nb = length // BLOCK
    ins = [q, k, k, v, v]
    specs = [pl.BlockSpec((1, BLOCK, wq), lambda s, i: (s, i, qblk)),
             pl.BlockSpec((1, BLOCK, wk), lambda s, i: (s, i, kblk)),
             pl.BlockSpec((1, BLOCK, wk), lambda s, i: (s, jnp.maximum(i - 1, 0), kblk)),
             pl.BlockSpec((1, BLOCK, wk), lambda s, i: (s, i, vblk)),
             pl.BlockSpec((1, BLOCK, wk), lambda s, i: (s, jnp.maximum(i - 1, 0), vblk))]
    if bias is not None:
        ins.append(bias)
        specs.append(pl.BlockSpec(bias.shape, lambda s, i: (0, 0, 0)))
    if sink is not None:
        ins.append(sink)
        specs.append(pl.BlockSpec(sink.shape, lambda s, i: (0, 0, 0)))
    if gate is not None:
        ins.append(gate)
        specs.append(pl.BlockSpec((1, BLOCK, wq), lambda s, i: (s, i, gblk)))
        out_shape = jax.ShapeDtypeStruct((n_seq, length, wq), BF16)
        out_specs = pl.BlockSpec((1, BLOCK, wq), lambda s, i: (s, i, 0))
    else:
        out_shape = (jax.ShapeDtypeStruct((n_seq, length, wq), F32),
                     jax.ShapeDtypeStruct((n_seq, length, n_kvh * n_g), F32))
        out_specs = (pl.BlockSpec((1, BLOCK, wq), lambda s, i: (s, i, 0)),
                     pl.BlockSpec((1, BLOCK, n_kvh * n_g), lambda s, i: (s, i, 0)))
    kern = functools.partial(_attn_kernel, n_kvh=n_kvh, n_g=n_g, hd=hd, window=window,
                             has_bias=bias is not None, has_sink=sink is not None, has_gate=gate is not None)
    return pl.pallas_call(
        kern, out_shape=out_shape, grid=(n_seq, nb), in_specs=specs, out_specs=out_specs,
        compiler_params=_cparams(("parallel", "arbitrary")), name=name,
    )(*ins)


def _dil_merge_kernel(o0_ref, o1_ref, o2_ref, l0_ref, l1_ref, l2_ref, g_ref, y_ref):
    l0 = l0_ref[...]
    l1 = l1_ref[...]
    l2 = l2_ref[...]
    m = jnp.maximum(jnp.maximum(l0, l1), l2)
    w0 = jnp.exp(l0 - m)
    w1 = jnp.exp(l1 - m)
    w2 = jnp.exp(l2 - m)
    inv = 1.0 / (w0 + w1 + w2)
    w0 = w0 * inv
    w1 = w1 * inv
    w2 = w2 * inv
    for h in range(DIL_QH):
        sl = slice(h * DIL_HD, (h + 1) * DIL_HD)
        o = (w0[:, h:h + 1] * o0_ref[:, sl] + w1[:, h:h + 1] * o1_ref[:, sl] + w2[:, h:h + 1] * o2_ref[:, sl])
        y_ref[:, sl] = (o * _silu(g_ref[:, sl])).astype(BF16)


def _dil_merge(o0, o1, o2, l0, l1, l2, hp, tm):
    m = o0.shape[0]
    ospec = pl.BlockSpec((tm, BRANCH_W), lambda i: (i, 0))
    lspec = pl.BlockSpec((tm, DIL_QH), lambda i: (i, 0))
    return pl.pallas_call(
        _dil_merge_kernel,
        out_shape=jax.ShapeDtypeStruct((m, BRANCH_W), BF16),
        grid=(m // tm,),
        in_specs=[ospec, ospec, ospec, lspec, lspec, lspec,
                  pl.BlockSpec((tm, BRANCH_W), lambda i: (i, U_BG * LANE // BRANCH_W))],
        out_specs=ospec,
        compiler_params=_cparams(("parallel",)),
        name="dil_merge",
    )(o0, o1, o2, l0, l1, l2, hp)


def _lru_gates(cx, wa_ref, wx_ref, ba, bx, lam):
    ra, rx = [], []
    for b in range(LRU_BLOCKS):
        xb = cx[:, b * LRU_BS:(b + 1) * LRU_BS].astype(BF16)
        ra.append(jnp.dot(xb, wa_ref[b], preferred_element_type=F32))
        rx.append(jnp.dot(xb, wx_ref[b], preferred_element_type=F32))
    r = _sigmoid(jnp.concatenate(ra, axis=1) + ba)
    ig = _sigmoid(jnp.concatenate(rx, axis=1) + bx)
    log_a = -LRU_C * r * _softplus(-lam)
    a = jnp.exp(log_a)
    th = jnp.tanh(log_a)
    bterm = jnp.sqrt(-2.0 * th / (1.0 - th)) * (ig * cx)
    return a, bterm


def _lru_kernel(x_ref, g_ref, cw_ref, cb_ref, wa_ref, wx_ref, ba_ref, bx_ref, lam_ref,
                y_ref, hl_ref, prev_ref, h_ref, a_s, b_s):
    tb = x_ref.shape[0]

    @pl.when(pl.program_id(1) == 0)
    def _():
        prev_ref[...] = jnp.zeros_like(prev_ref)
        h_ref[...] = jnp.zeros_like(h_ref)

    x = x_ref[...]
    cx = _conv_rows(x, prev_ref[...], cw_ref[...]) + cb_ref[...]
    prev_ref[...] = x[tb - 8:tb]
    a, bterm = _lru_gates(cx, wa_ref, wx_ref, ba_ref[...], bx_ref[...], lam_ref[...])
    a_s[...] = a
    b_s[...] = bterm

    def step(t, h):
        h = a_s[pl.ds(t, 1), :] * h + b_s[pl.ds(t, 1), :]
        b_s[pl.ds(t, 1), :] = h
        return h

    h = lax.fori_loop(0, tb, step, h_ref[...], unroll=8)
    h_ref[...] = h
    hl_ref[0] = h
    y_ref[...] = (b_s[...] * _silu(g_ref[...])).astype(BF16)


def _lru_prompt(hp, n_seq, t_len, conv_w, conv_b, wa, wx, ba, bx, lam, tb):
    nb = t_len // tb
    vec = pl.BlockSpec((1, LRU_W), lambda n, c: (0, 0))
    wspec = pl.BlockSpec((LRU_BLOCKS, LRU_BS, LRU_BS), lambda n, c: (0, 0, 0))
    return pl.pallas_call(
        _lru_kernel,
        out_shape=(jax.ShapeDtypeStruct((n_seq * t_len, LRU_W), BF16),
                   jax.ShapeDtypeStruct((n_seq, 1, LRU_W), F32)),
        grid=(n_seq, nb),
        in_specs=[pl.BlockSpec((tb, LRU_W), lambda n, c: (n * nb + c, U_CX * LANE // LRU_W)),
                  pl.BlockSpec((tb, LRU_W), lambda n, c: (n * nb + c, U_CG * LANE // LRU_W)),
                  pl.BlockSpec((CONV_W, LRU_W), lambda n, c: (0, 0)),
                  vec, wspec, wspec, vec, vec, vec],
        out_specs=(pl.BlockSpec((tb, LRU_W), lambda n, c: (n * nb + c, 0)),
                   pl.BlockSpec((1, 1, LRU_W), lambda n, c: (n, 0, 0))),
        scratch_shapes=[pltpu.VMEM((8, LRU_W), F32), pltpu.VMEM((1, LRU_W), F32),
                        pltpu.VMEM((tb, LRU_W), F32), pltpu.VMEM((tb, LRU_W), F32)],
        compiler_params=_cparams(("parallel", "arbitrary")),
        name="lru_prompt",
    )(hp, hp, conv_w, conv_b, wa, wx, ba, bx, lam)


def _rope_kernel(q_ref, k_ref, cos_ref, sin_ref, rq_ref, rk_ref):
    cos = cos_ref[...]
    sin = sin_ref[...]
    k = k_ref[...]
    lane = lax.broadcasted_iota(jnp.int32, k.shape, 1)
    first = (lane & (SWA_HD - 1)) < SWA_HD // 2
    half = SWA_HD // 2
    ks = jnp.where(first, pltpu.roll(k, LANE - half, 1), pltpu.roll(k, half, 1))
    rk_ref[...] = k * cos + ks * sin
    q = q_ref[...]
    wq = q.shape[1]
    reps = wq // LANE
    cosq = jnp.concatenate([cos] * reps, axis=1)
    sinq = jnp.concatenate([sin] * reps, axis=1)
    firstq = jnp.concatenate([first] * reps, axis=1)
    qs = jnp.where(firstq, pltpu.roll(q, wq - half, 1), pltpu.roll(q, half, 1))
    rq_ref[...] = (q * cosq + qs * sinq).astype(rq_ref.dtype)


def _rope_prompt(hp, cos_t, sin_t, t_len, tm):
    m = hp.shape[0]
    nt = t_len // tm
    wq = SWA_QH * SWA_HD
    return pl.pallas_call(
        _rope_kernel,
        out_shape=(jax.ShapeDtypeStruct((m, wq), BF16), jax.ShapeDtypeStruct((m, LANE), F32)),
        grid=(m // tm,),
        in_specs=[pl.BlockSpec((tm, wq), lambda i: (i, U_DQ * LANE // wq)),
                  pl.BlockSpec((tm, LANE), lambda i: (i, U_DK)),
                  pl.BlockSpec((tm, LANE), lambda i: (i % nt, 0)),
                  pl.BlockSpec((tm, LANE), lambda i: (i % nt, 0))],
        out_specs=(pl.BlockSpec((tm, wq), lambda i: (i, 0)), pl.BlockSpec((tm, LANE), lambda i: (i, 0))),
        compiler_params=_cparams(("parallel",)),
        name="rope",
    )(hp, hp, cos_t, sin_t)


def _branch_kernel(ya_ref, yb_ref, yc_ref, yd_ref, g0_ref, g1_ref, g2_ref, g3_ref, wb_ref, o_ref):
    acc = None
    for b, (y_ref, g_ref) in enumerate(((ya_ref, g0_ref), (yb_ref, g1_ref), (yc_ref, g2_ref), (yd_ref, g3_ref))):
        br = jnp.dot(y_ref[...], wb_ref[b], preferred_element_type=F32)
        term = _sigmoid(g_ref[...]) * br
        acc = term if acc is None else acc + term
    o_ref[...] = acc.astype(BF16)


def _branch_merge(ys, hp, wb, tm, tn):
    m = ys[0].shape[0]
    nj = D_MODEL // tn
    yspec = pl.BlockSpec((tm, BRANCH_W), lambda i, j: (i, 0))
    gspecs = [pl.BlockSpec((tm, tn), functools.partial(lambda i, j, b: (i, U_MG * LANE // tn + b * nj + j), b=b))
              for b in range(N_BRANCH)]
    return pl.pallas_call(
        _branch_kernel,
        out_shape=jax.ShapeDtypeStruct((m, D_MODEL), BF16),
        grid=(m // tm, nj),
        in_specs=[yspec, yspec, yspec, yspec] + gspecs +
                 [pl.BlockSpec((N_BRANCH, BRANCH_W, tn), lambda i, j: (0, 0, j))],
        out_specs=pl.BlockSpec((tm, tn), lambda i, j: (i, j)),
        compiler_params=_cparams(("parallel", "arbitrary")),
        name="branch_merge",
    )(*ys, hp, hp, hp, hp, wb)


def _out_kernel(m_ref, x_ref, w_ref, g_ref, b_ref, o_ref):
    f = jnp.dot(m_ref[...], w_ref[...], preferred_element_type=F32)
    z = DN_ALPHA * x_ref[...] + f
    mu = jnp.mean(z, axis=-1, keepdims=True)
    zc = z - mu
    var = jnp.mean(zc * zc, axis=-1, keepdims=True)
    o_ref[...] = zc * lax.rsqrt(var + LN_EPS) * g_ref[...] + b_ref[...]


def _out_proj(merged, x, w_out, ln_g, ln_b, tm):
    m = x.shape[0]
    vec = pl.BlockSpec((1, D_MODEL), lambda i: (0, 0))
    return pl.pallas_call(
        _out_kernel,
        out_shape=jax.ShapeDtypeStruct((m, D_MODEL), F32),
        grid=(m // tm,),
        in_specs=[pl.BlockSpec((tm, D_MODEL), lambda i: (i, 0)),
                  pl.BlockSpec((tm, D_MODEL), lambda i: (i, 0)),
                  pl.BlockSpec((D_MODEL, D_MODEL), lambda i: (0, 0)),
                  vec, vec],
        out_specs=pl.BlockSpec((tm, D_MODEL), lambda i: (i, 0)),
        compiler_params=_cparams(("parallel",)),
        name="out_proj",
    )(merged, x, w_out, ln_g, ln_b)


def _gdn_dec_pre_kernel(x_ref, b0_ref, b1_ref, b2_ref, cw_ref, ab_ref, arow_ref, dtb_ref,
                        qkv_ref, beta_ref, eg_ref):
    cw = cw_ref[...]
    y = b0_ref[...] * cw[0:1] + b1_ref[...] * cw[1:2] + b2_ref[...] * cw[2:3] + x_ref[...] * cw[3:4]
    act = _silu(y)
    for h in range(GDN_HEADS):
        sq = slice(h * GDN_DK, (h + 1) * GDN_DK)
        sk = slice((GDN_HEADS + h) * GDN_DK, (GDN_HEADS + h + 1) * GDN_DK)
        q = act[:, sq]
        k = act[:, sk]
        qkv_ref[:, sq] = q * lax.rsqrt(jnp.sum(q * q, axis=-1, keepdims=True) + 1e-6) * (GDN_DK ** -0.5)
        qkv_ref[:, sk] = k * lax.rsqrt(jnp.sum(k * k, axis=-1, keepdims=True) + 1e-6)
    qkv_ref[:, 2 * GDN_HEADS * GDN_DK:] = act[:, 2 * GDN_HEADS * GDN_DK:]
    ab = ab_ref[...]
    beta_ref[...] = _sigmoid(ab)
    eg_ref[...] = jnp.exp(-arow_ref[...] * _softplus(ab + dtb_ref[...]))


def _gdn_dec_pre(x, b0, b1, b2, conv_w, ab, arow, dtb):
    n = x.shape[0]
    full = lambda a: pl.BlockSpec(a.shape, lambda i: (0,) * a.ndim)
    ins = (x, b0, b1, b2, conv_w, ab, arow, dtb)
    return pl.pallas_call(
        _gdn_dec_pre_kernel,
        out_shape=(jax.ShapeDtypeStruct((n, GDN_QKV), F32), jax.ShapeDtypeStruct((n, LANE), F32),
                   jax.ShapeDtypeStruct((n, LANE), F32)),
        grid=(1,),
        in_specs=[full(a) for a in ins],
        out_specs=(pl.BlockSpec((n, GDN_QKV), lambda i: (0, 0)), pl.BlockSpec((n, LANE), lambda i: (0, 0)),
                   pl.BlockSpec((n, LANE), lambda i: (0, 0))),
        compiler_params=_cparams(("arbitrary",)),
        name="gdn_dec_pre",
    )(*ins)


def _gdn_dec_kernel(qt_ref, kt_ref, v_ref, z_ref, beta_ref, eg_ref, nw_ref, s_ref, so_ref, y_ref):
    qt = qt_ref[0]
    kt = kt_ref[0]
    v = v_ref[0]
    z = z_ref[0]
    beta = beta_ref[0]
    eg = eg_ref[0]
    nw = nw_ref[...]
    for h in range(GDN_HEADS):
        s = s_ref[0, h] * eg[:, GDN_HEADS + h:GDN_HEADS + h + 1]
        kcol = kt[:, h:h + 1]
        v_old = jnp.sum(s * kcol, axis=0, keepdims=True)
        delta = (v[h:h + 1, :] - v_old) * beta[:, h:h + 1]
        s = s + kcol * delta
        so_ref[0, h] = s
        o = jnp.sum(s * qt[:, h:h + 1], axis=0, keepdims=True)
        y = o * lax.rsqrt(jnp.mean(o * o, axis=-1, keepdims=True) + RMS_EPS) * nw
        y_ref[0, h:h + 1, :] = y * _silu(z[h:h + 1, :])


def _gdn_dec(qt, kt, v, z, beta, eg, norm_w, state, layer):
    n = qt.shape[0]
    tspec = pl.BlockSpec((1, GDN_DK, GDN_HEADS), lambda i: (i, 0, 0))
    hspec = pl.BlockSpec((1, GDN_HEADS, GDN_DV), lambda i: (i, 0, 0))
    rspec = pl.BlockSpec((1, 1, LANE), lambda i: (i, 0, 0))
    return pl.pallas_call(
        _gdn_dec_kernel,
        out_shape=(jax.ShapeDtypeStruct((n, GDN_HEADS, GDN_DK, GDN_DV), F32),
                   jax.ShapeDtypeStruct((n, GDN_HEADS, GDN_DV), F32)),
        grid=(n,),
        in_specs=[tspec, tspec, hspec, hspec, rspec, rspec,
                  pl.BlockSpec((1, GDN_DV), lambda i: (0, 0)),
                  pl.BlockSpec((None, 1, GDN_HEADS, GDN_DK, GDN_DV), lambda i: (layer, i, 0, 0, 0))],
        out_specs=(pl.BlockSpec((1, GDN_HEADS, GDN_DK, GDN_DV), lambda i: (i, 0, 0, 0)), hspec),
        compiler_params=_cparams(("parallel",)),
        name="gdn_dec",
    )(qt, kt, v, z, beta, eg, norm_w, state)


def _dec_attend(q, kc, vc, knew, vnew, n_g, scale, bias_c=None, bias_n=None, sink=None):
    hq = q.shape[0]
    row = lax.broadcasted_iota(jnp.int32, (hq, 1), 0)
    sc = None
    kn = None
    vn = None
    for kvh in range(len(kc)):
        sel = (row >= kvh * n_g) & (row < (kvh + 1) * n_g)
        s_k = _dot_nt_hi(q, kc[kvh]) * scale
        sc = s_k if sc is None else jnp.where(sel, s_k, sc)
        kn = jnp.broadcast_to(knew[kvh], q.shape) if kn is None else jnp.where(sel, knew[kvh], kn)
        vn = jnp.broadcast_to(vnew[kvh], q.shape) if vn is None else jnp.where(sel, vnew[kvh], vn)
    sn = jnp.sum(q * kn, axis=-1, keepdims=True) * scale
    if bias_c is not None:
        sc = sc + bias_c
        sn = sn + bias_n
    m = jnp.maximum(jnp.max(sc, axis=-1, keepdims=True), sn)
    if sink is not None:
        m = jnp.maximum(m, sink)
    pc = jnp.exp(sc - m)
    pn = jnp.exp(sn - m)
    den = jnp.sum(pc, axis=-1, keepdims=True) + pn
    if sink is not None:
        den = den + jnp.exp(sink - m)
    oc = None
    for kvh in range(len(kc)):
        sel = (row >= kvh * n_g) & (row < (kvh + 1) * n_g)
        o_k = _dot_hi(pc, vc[kvh])
        oc = o_k if oc is None else jnp.where(sel, o_k, oc)
    o = (oc + pn * vn) / den
    return o, m + jnp.log(den)


def _dil_dec_kernel(q_ref, kn_ref, vn_ref, g_ref, c0_ref, c1_ref, c2_ref, bc_ref, bn_ref, y_ref):
    q_all = q_ref[0]
    kn_all = kn_ref[0]
    vn_all = vn_ref[0]
    outs, lses = [], []
    kvw = DIL_KVH * DIL_HD
    for gi, c_ref in enumerate((c0_ref, c1_ref, c2_ref)):
        q = q_all[gi * DIL_QH:(gi + 1) * DIL_QH]
        kc = [c_ref[:, kvh * DIL_HD:(kvh + 1) * DIL_HD] for kvh in range(DIL_KVH)]
        vc = [c_ref[:, kvw + kvh * DIL_HD:kvw + (kvh + 1) * DIL_HD] for kvh in range(DIL_KVH)]
        knew = [kn_all[gi * DIL_KVH + kvh:gi * DIL_KVH + kvh + 1] for kvh in range(DIL_KVH)]
        vnew = [vn_all[gi * DIL_KVH + kvh:gi * DIL_KVH + kvh + 1] for kvh in range(DIL_KVH)]
        o, lse = _dec_attend(q, kc, vc, knew, vnew, DIL_G, DIL_HD ** -0.5, bc_ref[gi], bn_ref[gi])
        outs.append(o)
        lses.append(lse)
    m = jnp.maximum(jnp.maximum(lses[0], lses[1]), lses[2])
    ws = [jnp.exp(l - m) for l in lses]
    inv = 1.0 / (ws[0] + ws[1] + ws[2])
    o = (ws[0] * outs[0] + ws[1] * outs[1] + ws[2] * outs[2]) * inv
    y_ref[0] = o * _silu(g_ref[0])


def _dil_dec(q3, kn3, vn3, g3, caches, layer, bias_c, bias_n):
    n = q3.shape[0]
    kvw2 = 2 * DIL_KVH * DIL_HD
    views, cspecs = [], []
    for (win, dil), c in zip(DIL_GROUPS, caches):
        views.append(c.reshape(DEPTH, n, win // dil, dil * kvw2))
        cspecs.append(pl.BlockSpec((None, None, win // dil, kvw2), lambda i: (layer, i, 0, 0)))
    return pl.pallas_call(
        _dil_dec_kernel,
        out_shape=jax.ShapeDtypeStruct((n, DIL_QH, DIL_HD), F32),
        grid=(n,),
        in_specs=[pl.BlockSpec((1, N_DIL * DIL_QH, DIL_HD), lambda i: (i, 0, 0)),
                  pl.BlockSpec((1, N_DIL * DIL_KVH, DIL_HD), lambda i: (i, 0, 0)),
                  pl.BlockSpec((1, N_DIL * DIL_KVH, DIL_HD), lambda i: (i, 0, 0)),
                  pl.BlockSpec((1, DIL_QH, DIL_HD), lambda i: (i, 0, 0))] + cspecs +
                 [pl.BlockSpec(bias_c.shape, lambda i: (0, 0, 0)),
                  pl.BlockSpec(bias_n.shape, lambda i: (0, 0, 0))],
        out_specs=pl.BlockSpec((1, DIL_QH, DIL_HD), lambda i: (i, 0, 0)),
        compiler_params=_cparams(("parallel",)),
        name="dil_dec",
    )(q3, kn3, vn3, g3, *views, bias_c, bias_n)


def _swa_dec_kernel(q_ref, kn_ref, vn_ref, g_ref, c_ref, sink_ref, cos_ref, sin_ref, perm_ref, y_ref, rk_ref):
    cos = cos_ref[...]
    sin = sin_ref[...]
    perm = perm_ref[...]
    q = q_ref[0]
    kn = kn_ref[0]
    q = q * cos + _dot_hi(q, perm) * sin
    kn = kn * cos + _dot_hi(kn, perm) * sin
    rk_ref[0] = kn
    vn = vn_ref[0]
    kvw = SWA_KVH * SWA_HD
    kc = [c_ref[:, kvh * SWA_HD:(kvh + 1) * SWA_HD] for kvh in range(SWA_KVH)]
    vc = [c_ref[:, kvw + kvh * SWA_HD:kvw + (kvh + 1) * SWA_HD] for kvh in range(SWA_KVH)]
    knew = [kn[kvh:kvh + 1] for kvh in range(SWA_KVH)]
    vnew = [vn[kvh:kvh + 1] for kvh in range(SWA_KVH)]
    o, _ = _dec_attend(q, kc, vc, knew, vnew, SWA_G, SWA_HD ** -0.5, sink=sink_ref[...])
    y_ref[0] = o * _silu(g_ref[0])


def _swa_dec(q3, kn3, vn3, g3, cache, layer, sink_col, cos_d, sin_d, perm):
    n = q3.shape[0]
    kvw2 = 2 * SWA_KVH * SWA_HD
    win = cache.shape[2]
    view = cache.reshape(DEPTH, n, win, kvw2)
    return pl.pallas_call(
        _swa_dec_kernel,
        out_shape=(jax.ShapeDtypeStruct((n, SWA_QH, SWA_HD), F32),
                   jax.ShapeDtypeStruct((n, SWA_KVH, SWA_HD), F32)),
        grid=(n,),
        in_specs=[pl.BlockSpec((1, SWA_QH, SWA_HD), lambda i: (i, 0, 0)),
                  pl.BlockSpec((1, SWA_KVH, SWA_HD), lambda i: (i, 0, 0)),
                  pl.BlockSpec((1, SWA_KVH, SWA_HD), lambda i: (i, 0, 0)),
                  pl.BlockSpec((1, SWA_QH, SWA_HD), lambda i: (i, 0, 0)),
                  pl.BlockSpec((None, None, win, kvw2), lambda i: (layer, i, 0, 0)),
                  pl.BlockSpec(sink_col.shape, lambda i: (0, 0)),
                  pl.BlockSpec(cos_d.shape, lambda i: (0, 0)),
                  pl.BlockSpec(sin_d.shape, lambda i: (0, 0)),
                  pl.BlockSpec(perm.shape, lambda i: (0, 0))],
        out_specs=(pl.BlockSpec((1, SWA_QH, SWA_HD), lambda i: (i, 0, 0)),
                   pl.BlockSpec((1, SWA_KVH, SWA_HD), lambda i: (i, 0, 0))),
        compiler_params=_cparams(("parallel",)),
        name="swa_dec",
    )(q3, kn3, vn3, g3, view, sink_col, cos_d, sin_d, perm)


def _lru_dec_kernel(x_ref, g_ref, b0_ref, b1_ref, b2_ref, h0_ref, cw_ref, cb_ref, wa_ref, wx_ref,
                    ba_ref, bx_ref, lam_ref, y_ref, h_ref):
    cw = cw_ref[...]
    cx = (b0_ref[...] * cw[0:1] + b1_ref[...] * cw[1:2] + b2_ref[...] * cw[2:3] + x_ref[...] * cw[3:4]
          + cb_ref[...])
    a, bterm = _lru_gates(cx, wa_ref, wx_ref, ba_ref[...], bx_ref[...], lam_ref[...])
    h = a * h0_ref[...] + bterm
    h_ref[...] = h
    y_ref[...] = h * _silu(g_ref[...])


def _lru_dec(x, g, b0, b1, b2, h0, conv_w, conv_b, wa, wx, ba, bx, lam):
    n = x.shape[0]
    ins = (x, g, b0, b1, b2, h0, conv_w, conv_b, wa, wx, ba, bx, lam)
    full = lambda a: pl.BlockSpec(a.shape, lambda i: (0,) * a.ndim)
    return pl.pallas_call(
        _lru_dec_kernel,
        out_shape=(jax.ShapeDtypeStruct((n, LRU_W), F32), jax.ShapeDtypeStruct((n, LRU_W), F32)),
        grid=(1,),
        in_specs=[full(a) for a in ins],
        out_specs=(pl.BlockSpec((n, LRU_W), lambda i: (0, 0)), pl.BlockSpec((n, LRU_W), lambda i: (0, 0))),
        compiler_params=_cparams(("arbitrary",)),
        name="lru_dec",
    )(*ins)


def _rel_bucket(dist):
    max_exact = REL_BUCKETS // 2
    n = dist.astype(F32)
    large = max_exact + (jnp.log(jnp.maximum(n, 1.0) / max_exact) / math.log(REL_MAX_DIST / max_exact)
                         * (REL_BUCKETS - max_exact)).astype(jnp.int32)
    large = jnp.minimum(large, REL_BUCKETS - 1)
    return jnp.where(dist < max_exact, dist, large)


def _offset_bias(rel_bias, gi, win, dil):
    j = win // dil + 1
    b = rel_bias[_rel_bucket(dil * jnp.arange(j, dtype=jnp.int32))]
    return b[:, gi * DIL_QH:(gi + 1) * DIL_QH].astype(F32)


def _rope_tables(pos):
    half = SWA_HD // 2
    inv = ROPE_THETA ** (-jnp.arange(half, dtype=F32) / half)
    ang = pos.astype(F32)[:, None] * inv[None, :]
    c, s = jnp.cos(ang), jnp.sin(ang)
    return jnp.concatenate([c, c], axis=1), jnp.concatenate([-s, s], axis=1)


def _to_sub(x, d):
    n, l = x.shape[:2]
    x = jnp.swapaxes(x.reshape((n, l // d, d) + x.shape[2:]), 1, 2)
    return x.reshape((n * d, l // d) + x.shape[3:])


def _from_sub(x, n, d):
    ld = x.shape[1]
    x = jnp.swapaxes(x.reshape((n, d, ld) + x.shape[2:]), 1, 2)
    return x.reshape((n, ld * d) + x.shape[3:])


def _reorder_w_in(w):
    pad = jnp.zeros((w.shape[0], LANE - 2 * GDN_HEADS), w.dtype)
    parts = [w[:, O_MG:O_END], w[:, O_AZ:O_AB], w[:, O_AQKV:O_AZ], w[:, O_BQ:O_BK], w[:, O_BG:O_CX],
             w[:, O_CX:O_CG], w[:, O_CG:O_DQ], w[:, O_DQ:O_DK], w[:, O_DG:O_MG], w[:, O_BK:O_BV],
             w[:, O_BV:O_BG], w[:, O_DK:O_DV], w[:, O_DV:O_DG], w[:, O_AB:O_BQ], pad]
    return jnp.concatenate(parts, axis=1).astype(BF16)


def _cols(h, unit, width):
    return h[..., unit * LANE:unit * LANE + width]


def kernel(x_prompt, x_sample, state_gdn, state_gdn_conv, cache_dil_w128, cache_dil_w512, cache_dil_w2048,
           cache_swa, state_rglru, state_rglru_conv, w_in, gdn_conv_w, gdn_a_log, gdn_dt_bias, gdn_norm_w,
           lru_conv_w, lru_conv_b, lru_wa, lru_ba, lru_wx, lru_bx, lru_lambda, swa_sink, rel_bias, w_branch,
           w_out, ln_g, ln_b):
    nb, t_len, _ = x_prompt.shape
    ns = x_sample.shape[0]
    mp = nb * t_len
    caches = (cache_dil_w128, cache_dil_w512, cache_dil_w2048)
    xp = x_prompt.reshape(mp, D_MODEL)
    xs = x_sample.reshape(ns, D_MODEL)

    cos64, sin64 = _rope_tables(jnp.arange(t_len))
    cos_t = jnp.concatenate([cos64, cos64], axis=1)
    sin_t = jnp.concatenate([sin64, sin64], axis=1)
    cos_d, sin_d = _rope_tables(jnp.full((1,), PAST_LEN))
    perm = jnp.asarray(np.roll(np.eye(SWA_HD, dtype=np.float32), SWA_HD // 2, axis=0))
    qi = jnp.arange(BLOCK)[:, None]
    kj = jnp.arange(2 * BLOCK)[None, :]
    bias_p, bias_c, bias_n = [], [], []
    for gi, (win, dil) in enumerate(DIL_GROUPS):
        ob = _offset_bias(rel_bias, gi, win, dil)
        jw = win // dil
        bias_p.append(jnp.transpose(ob[jnp.clip(qi + BLOCK - kj, 0, jw)], (2, 0, 1)))
        bias_c.append(jnp.transpose(ob[jw - jnp.arange(jw)], (1, 0)))
        bias_n.append(ob[0][:, None])
    bias_c = jnp.stack(bias_c)
    bias_n = jnp.stack(bias_n)

    new_p = [[] for _ in range(8)]
    new_s = [[] for _ in range(8)]
    for l in range(DEPTH):
        w_r = _reorder_w_in(w_in[l])
        wb = w_branch[l].astype(BF16)
        wo = w_out[l].astype(BF16)
        wa = lru_wa[l].astype(BF16)
        wx = lru_wx[l].astype(BF16)
        zpad = jnp.zeros((1, LANE - 2 * GDN_HEADS), F32)
        arow = jnp.concatenate([jnp.zeros((1, GDN_HEADS), F32), jnp.exp(gdn_a_log[l])[None], zpad], axis=1)
        dtb = jnp.concatenate([jnp.zeros((1, GDN_HEADS), F32), gdn_dt_bias[l][None], zpad], axis=1)
        norm_w = gdn_norm_w[l][None]
        conv_b = lru_conv_b[l][None]
        ba, bx, lam = lru_ba[l][None], lru_bx[l][None], lru_lambda[l][None]
        lng, lnb = ln_g[l][None], ln_b[l][None]
        sink = swa_sink[l].astype(F32)

        hp = _matmul(xp.astype(BF16), w_r, 1024, 896)
        hp3 = hp.reshape(nb, t_len, D_INP)
        ya, s_p = _gdn_prompt(hp, nb, t_len, gdn_conv_w[l], arow, dtb, norm_w)
        outs, lses = [], []
        for gi, (win, dil) in enumerate(DIL_GROUPS):
            qg = _to_sub(_cols(hp3, U_BQ + gi * DIL_QH, DIL_QH * DIL_HD), dil)
            kg = _to_sub(_cols(hp3, U_BK + gi * DIL_KVH, DIL_KVH * DIL_HD), dil)
            vg = _to_sub(_cols(hp3, U_BV + gi * DIL_KVH, DIL_KVH * DIL_HD), dil)
            o_g, l_g = _banded_attn(qg, 0, kg, 0, vg, 0, nb * dil, t_len // dil, DIL_KVH, DIL_G, DIL_HD,
                                    win // dil, bias=bias_p[gi], name=f"dil_attn{gi}")
            outs.append(_from_sub(o_g, nb, dil).reshape(mp, BRANCH_W))
            lses.append(_from_sub(l_g, nb, dil).reshape(mp, DIL_QH))
        yb = _dil_merge(outs[0], outs[1], outs[2], lses[0], lses[1], lses[2], hp, 512)
        yc, h_p = _lru_prompt(hp, nb, t_len, lru_conv_w[l], conv_b, wa, wx, ba, bx, lam, 512)
        rq, rk = _rope_prompt(hp, cos_t, sin_t, t_len, 512)
        sink_p = jnp.broadcast_to(sink.reshape(SWA_KVH, SWA_G, 1, 1), (SWA_KVH, SWA_G, BLOCK, 1)).reshape(
            SWA_KVH, SWA_G * BLOCK, 1)
        yd = _banded_attn(rq.reshape(nb, t_len, BRANCH_W), 0, rk.reshape(nb, t_len, LANE), 0, hp3, U_DV,
                          nb, t_len, SWA_KVH, SWA_G, SWA_HD, SWA_WINDOW, sink=sink_p,
                          gate=hp3, gblk=U_DG * LANE // BRANCH_W, name="swa_attn")
        merged = _branch_merge((ya, yb, yc, yd.reshape(mp, BRANCH_W)), hp, wb, 512, 512)
        xp_new = _out_proj(merged, xp, wo, lng, lnb, 256)

        new_p[0].append(s_p)
        new_p[1].append(_cols(hp3, U_AQKV, GDN_QKV)[:, t_len - (CONV_W - 1):])
        for gi, (win, dil) in enumerate(DIL_GROUPS):
            kk = _cols(hp3, U_BK + gi * DIL_KVH, DIL_KVH * DIL_HD)[:, t_len - win:]
            vv = _cols(hp3, U_BV + gi * DIL_KVH, DIL_KVH * DIL_HD)[:, t_len - win:]
            new_p[2 + gi].append(jnp.stack([kk, vv], axis=2).reshape(nb, win, 2, DIL_KVH, DIL_HD))
        kk = rk.reshape(nb, t_len, LANE)[:, t_len - SWA_WINDOW:]
        vv = _cols(hp3, U_DV, LANE)[:, t_len - SWA_WINDOW:]
        new_p[5].append(jnp.stack([kk, vv], axis=2).reshape(nb, SWA_WINDOW, 2, SWA_KVH, SWA_HD))
        new_p[6].append(h_p.reshape(nb, LRU_W))
        new_p[7].append(_cols(hp3, U_CX, LRU_W)[:, t_len - (CONV_W - 1):])

        hs = _matmul(xs.astype(BF16), w_r, ns, 896)
        gbuf = state_gdn_conv[l]
        a_qkv_s = _cols(hs, U_AQKV, GDN_QKV)
        qkv_n, beta_s, eg_s = _gdn_dec_pre(a_qkv_s, gbuf[:, 0], gbuf[:, 1], gbuf[:, 2], gdn_conv_w[l],
                                           _cols(hs, U_AB, LANE), arow, dtb)
        qkv4 = qkv_n.reshape(ns, 3, GDN_HEADS, GDN_DK)
        s_s, ya_s = _gdn_dec(jnp.swapaxes(qkv4[:, 0], 1, 2), jnp.swapaxes(qkv4[:, 1], 1, 2), qkv4[:, 2],
                             _cols(hs, U_AZ, BRANCH_W).reshape(ns, GDN_HEADS, GDN_DV),
                             beta_s[:, None], eg_s[:, None], norm_w, state_gdn, l)
        bk_s = _cols(hs, U_BK, N_DIL * DIL_KVH * DIL_HD)
        bv_s = _cols(hs, U_BV, N_DIL * DIL_KVH * DIL_HD)
        yb_s = _dil_dec(_cols(hs, U_BQ, N_DIL * BRANCH_W).reshape(ns, N_DIL * DIL_QH, DIL_HD),
                        bk_s.reshape(ns, N_DIL * DIL_KVH, DIL_HD), bv_s.reshape(ns, N_DIL * DIL_KVH, DIL_HD),
                        _cols(hs, U_BG, BRANCH_W).reshape(ns, DIL_QH, DIL_HD), caches, l, bias_c, bias_n)
        lbuf = state_rglru_conv[l]
        cx_s = _cols(hs, U_CX, LRU_W)
        yc_s, h_s = _lru_dec(cx_s, _cols(hs, U_CG, LRU_W), lbuf[:, 0], lbuf[:, 1], lbuf[:, 2], state_rglru[l],
                             lru_conv_w[l], conv_b, wa, wx, ba, bx, lam)
        dv_s = _cols(hs, U_DV, LANE)
        yd_s, rk_s = _swa_dec(_cols(hs, U_DQ, BRANCH_W).reshape(ns, SWA_QH, SWA_HD),
                              _cols(hs, U_DK, LANE).reshape(ns, SWA_KVH, SWA_HD),
                              dv_s.reshape(ns, SWA_KVH, SWA_HD),
                              _cols(hs, U_DG, BRANCH_W).reshape(ns, SWA_QH, SWA_HD),
                              cache_swa, l, sink.reshape(SWA_QH, 1), cos_d, sin_d, perm)
        ys_s = (ya_s.reshape(ns, BRANCH_W).astype(BF16), yb_s.reshape(ns, BRANCH_W).astype(BF16),
                yc_s.astype(BF16), yd_s.reshape(ns, BRANCH_W).astype(BF16))
        merged_s = _branch_merge(ys_s, hs, wb, ns, 512)
        xs_new = _out_proj(merged_s, xs, wo, lng, lnb, ns)

        new_s[0].append(s_s)
        new_s[1].append(jnp.concatenate([gbuf[:, 1:], a_qkv_s[:, None]], axis=1))
        for gi in range(N_DIL):
            kk = bk_s[:, gi * DIL_KVH * DIL_HD:(gi + 1) * DIL_KVH * DIL_HD].reshape(ns, 1, DIL_KVH, DIL_HD)
            vv = bv_s[:, gi * DIL_KVH * DIL_HD:(gi + 1) * DIL_KVH * DIL_HD].reshape(ns, 1, DIL_KVH, DIL_HD)
            new_s[2 + gi].append(jnp.stack([kk, vv], axis=2))
        new_s[5].append(jnp.stack([rk_s[:, None], dv_s.reshape(ns, 1, SWA_KVH, SWA_HD)], axis=2))
        new_s[6].append(h_s)
        new_s[7].append(jnp.concatenate([lbuf[:, 1:], cx_s[:, None]], axis=1))

        xp, xs = xp_new, xs_new

    p = [jnp.stack(v) for v in new_p]
    s = [jnp.stack(new_s[i]) for i in (0, 1)]
    for i, c in zip((2, 3, 4, 5), caches + (cache_swa,)):
        s.append(jnp.concatenate([c[:, :, 1:], jnp.stack(new_s[i])], axis=2))
    s += [jnp.stack(new_s[i]) for i in (6, 7)]
    return (xp.reshape(nb, t_len, D_MODEL), xs.reshape(ns, 1, D_MODEL),
            p[0], s[0], p[1], s[1], p[2], s[2], p[3], s[3], p[4], s[4], p[5], s[5], p[6], s[6], p[7], s[7])
```

```python
import functools
import math

import jax
import jax.numpy as jnp
import numpy as np
from jax import lax
from jax.experimental import pallas as pl
from jax.experimental.pallas import tpu as pltpu

F32 = jnp.float32
BF16 = jnp.bfloat16
NEG_INF = -1e30

D_MODEL = 2048
DEPTH = 2
PAST_LEN = 16384
BRANCH_W = D_MODEL // 2
N_BRANCH = 4
CONV_W = 4
BLOCK = 128
GDN_DK = 128
GDN_DV = 128
GDN_HEADS = BRANCH_W // GDN_DV
GDN_QKV = GDN_HEADS * (2 * GDN_DK + GDN_DV)
DIL_GROUPS = ((128, 1), (512, 4), (2048, 16))
N_DIL = len(DIL_GROUPS)
DIL_HD = 128
DIL_QH = BRANCH_W // DIL_HD
DIL_KVH = 2
DIL_G = DIL_QH // DIL_KVH
LRU_W = BRANCH_W
LRU_BLOCKS = 8
LRU_BS = LRU_W // LRU_BLOCKS
LRU_C = 8.0
SWA_HD = 64
SWA_QH = BRANCH_W // SWA_HD
SWA_KVH = 2
SWA_G = SWA_QH // SWA_KVH
SWA_WINDOW = 128
ROPE_THETA = 150000.0
REL_BUCKETS = 32
REL_MAX_DIST = 2048
LN_EPS = 1e-5
RMS_EPS = 1e-6
DN_ALPHA = (2.0 * DEPTH) ** 0.25

IN_SIZES = (GDN_QKV, GDN_HEADS * GDN_DV, GDN_HEADS, GDN_HEADS,
            N_DIL * DIL_QH * DIL_HD, N_DIL * DIL_KVH * DIL_HD, N_DIL * DIL_KVH * DIL_HD, DIL_QH * DIL_HD,
            LRU_W, LRU_W,
            SWA_QH * SWA_HD, SWA_KVH * SWA_HD, SWA_KVH * SWA_HD, SWA_QH * SWA_HD,
            N_BRANCH * D_MODEL)
_OFF = [0]
for _s in IN_SIZES:
    _OFF.append(_OFF[-1] + _s)
(O_AQKV, O_AZ, O_AB, O_AA, O_BQ, O_BK, O_BV, O_BG, O_CX, O_CG, O_DQ, O_DK, O_DV, O_DG, O_MG, O_END) = _OFF

LANE = 128
U_MG, U_AZ, U_AQKV, U_BQ, U_BG, U_CX, U_CG, U_DQ, U_DG, U_BK, U_BV, U_DK, U_DV, U_AB = (
    0, 64, 72, 96, 120, 128, 136, 144, 152, 160, 166, 172, 173, 174)
N_UNITS = 176
D_INP = N_UNITS * LANE
GDN_CHUNK = 128
VMEM_LIMIT = 56 * 1024 * 1024


def _cparams(sem):
    return pltpu.CompilerParams(dimension_semantics=sem, vmem_limit_bytes=VMEM_LIMIT)


def _sigmoid(x):
    return 1.0 / (1.0 + jnp.exp(-x))


def _silu(x):
    return x * _sigmoid(x)


def _softplus(x):
    return jnp.maximum(x, 0.0) + jnp.log1p(jnp.exp(-jnp.abs(x)))


def _dot(a, b):
    return jnp.dot(a.astype(BF16), b.astype(BF16), preferred_element_type=F32)


def _dot_nt(a, b):
    return lax.dot_general(a.astype(BF16), b.astype(BF16), (((1,), (1,)), ((), ())),
                           preferred_element_type=F32)


def _dot_hi(a, b):
    return jnp.dot(a, b, preferred_element_type=F32, precision=lax.Precision.HIGHEST)


def _split2(a):
    hi = a.astype(BF16)
    return hi, (a - hi.astype(F32)).astype(BF16)


def _dot3(a, b):
    ah, al = _split2(a)
    bh, bl = _split2(b)
    return (jnp.dot(ah, bh, preferred_element_type=F32) + jnp.dot(ah, bl, preferred_element_type=F32)
            + jnp.dot(al, bh, preferred_element_type=F32))


def _dot_exact_lhs(a, b):
    a = a.astype(BF16)
    b0 = b.astype(BF16)
    r1 = b - b0.astype(F32)
    b1 = r1.astype(BF16)
    b2 = (r1 - b1.astype(F32)).astype(BF16)
    return (jnp.dot(a, b0, preferred_element_type=F32) + jnp.dot(a, b1, preferred_element_type=F32)
            + jnp.dot(a, b2, preferred_element_type=F32))


def _dot_nt_hi(a, b):
    return lax.dot_general(a, b, (((1,), (1,)), ((), ())), preferred_element_type=F32,
                           precision=lax.Precision.HIGHEST)


def _matmul_kernel(x_ref, w_ref, o_ref):
    o_ref[...] = jnp.dot(x_ref[...], w_ref[...], preferred_element_type=F32)


def _matmul(x, w, tm, tn):
    m, k = x.shape
    n = w.shape[1]
    return pl.pallas_call(
        _matmul_kernel,
        out_shape=jax.ShapeDtypeStruct((m, n), F32),
        grid=(m // tm, n // tn),
        in_specs=[pl.BlockSpec((tm, k), lambda i, j: (i, 0)),
                  pl.BlockSpec((k, tn), lambda i, j: (0, j))],
        out_specs=pl.BlockSpec((tm, tn), lambda i, j: (i, j)),
        compiler_params=_cparams(("parallel", "arbitrary")),
        name="in_proj",
    )(x, w)


def _conv_rows(x, prev8, cw):
    row8 = lax.broadcasted_iota(jnp.int32, (8, x.shape[1]), 0)
    y = x * cw[CONV_W - 1:CONV_W]
    for k in range(1, CONV_W):
        xk = pltpu.roll(x, k, 0)
        fk = pltpu.roll(prev8, k, 0)
        head = jnp.where(row8 < k, fk, xk[0:8])
        xk = jnp.concatenate([head, xk[8:]], axis=0)
        y = y + xk * cw[CONV_W - 1 - k:CONV_W - k]
    return y


def _tri_inv(lmat, ri, ci):
    c = lmat[0].shape[0]
    eye = (ri == ci).astype(F32)
    blk = (ri >> 4) == (ci >> 4)
    d = [jnp.where(blk, l, 0.0) for l in lmat]
    x = [eye - di for di in d]
    p = [_dot(di, di) for di in d]
    x = [xi + _dot(xi, pi) for xi, pi in zip(x, p)]
    p = [_dot(pi, pi) for pi in p]
    x = [xi + _dot(xi, pi) for xi, pi in zip(x, p)]
    p = [_dot(pi, pi) for pi in p]
    x = [xi + _dot(xi, pi) for xi, pi in zip(x, p)]
    sh = 4
    while (1 << sh) < c:
        lower_left = ((ri >> (sh + 1)) == (ci >> (sh + 1))) & (((ri >> sh) & 1) == 1) & (((ci >> sh) & 1) == 0)
        t = [_dot(jnp.where(lower_left, l, 0.0), xi) for l, xi in zip(lmat, x)]
        x = [xi - _dot(xi, ti) for xi, ti in zip(x, t)]
        sh += 1
    return x


def _tri_solve(lmat, rhs, ri, ci):
    tinv = _tri_inv(lmat, ri, ci)
    sol = [_dot(t, r) for t, r in zip(tinv, rhs)]
    resid = [r - s - _dot3(l, s) for r, s, l in zip(rhs, sol, lmat)]
    return [s + _dot(t, r) for s, t, r in zip(sol, tinv, resid)]


def _gdn_kernel(qkv_ref, z_ref, ab_ref, cw_ref, arow_ref, dtb_ref, nw_ref, y_ref, s_ref, prev_ref):
    cidx = pl.program_id(1)
    C = GDN_CHUNK

    @pl.when(cidx == 0)
    def _():
        s_ref[...] = jnp.zeros_like(s_ref)
        prev_ref[...] = jnp.zeros_like(prev_ref)

    x = qkv_ref[...]
    act = _silu(_conv_rows(x, prev_ref[...], cw_ref[...]))
    prev_ref[...] = x[C - 8:C]

    ab = ab_ref[...]
    beta_t = _sigmoid(ab)
    g_t = -arow_ref[...] * _softplus(ab + dtb_ref[...])
    ri = lax.broadcasted_iota(jnp.int32, (C, C), 0)
    ci = lax.broadcasted_iota(jnp.int32, (C, C), 1)
    incl = ri >= ci
    strict = ri > ci
    gc_t = _dot_exact_lhs(incl.astype(F32), g_t)
    gc_tt = gc_t.T
    egc_t = jnp.exp(gc_t)
    nw = nw_ref[...]
    z = z_ref[...]

    heads = range(GDN_HEADS)
    q, k, v, beta, gcc, egc, decay, glc = [], [], [], [], [], [], [], []
    for h in heads:
        qh = act[:, h * GDN_DK:(h + 1) * GDN_DK]
        kh = act[:, (GDN_HEADS + h) * GDN_DK:(GDN_HEADS + h + 1) * GDN_DK]
        q.append(qh * lax.rsqrt(jnp.sum(qh * qh, axis=-1, keepdims=True) + 1e-6) * (GDN_DK ** -0.5))
        k.append(kh * lax.rsqrt(jnp.sum(kh * kh, axis=-1, keepdims=True) + 1e-6))
        v.append(act[:, (2 * GDN_HEADS + h) * GDN_DK:(2 * GDN_HEADS + h + 1) * GDN_DK])
        beta.append(beta_t[:, h:h + 1])
        gcc.append(gc_t[:, GDN_HEADS + h:GDN_HEADS + h + 1])
        gcr = gc_tt[GDN_HEADS + h:GDN_HEADS + h + 1, :]
        egc.append(egc_t[:, GDN_HEADS + h:GDN_HEADS + h + 1])
        glc.append(gc_t[C - 1:C, GDN_HEADS + h:GDN_HEADS + h + 1])
        decay.append(jnp.where(incl, jnp.exp(jnp.where(incl, gcc[h] - gcr, 0.0)), 0.0))
    kb = [k[h] * beta[h] for h in heads]
    lmat = [jnp.where(strict, _dot_nt(kb[h], k[h]) * decay[h], 0.0) for h in heads]
    rhs = [jnp.concatenate([v[h] * beta[h], kb[h] * egc[h]], axis=1) for h in heads]
    sol = _tri_solve(lmat, rhs, ri, ci)
    qk = [_dot_nt(q[h], k[h]) * decay[h] for h in heads]
    s = [s_ref[0, h] for h in heads]
    v_new = [sol[h][:, :GDN_DV] - _dot(sol[h][:, GDN_DV:], s[h]) for h in heads]
    o = [_dot(q[h] * egc[h], s[h]) + _dot(qk[h], v_new[h]) for h in heads]
    for h in heads:
        kd = k[h] * jnp.exp(glc[h] - gcc[h])
        s_ref[0, h] = s[h] * jnp.exp(glc[h]) + _dot(kd.T, v_new[h])
    for h in heads:
        y = o[h] * lax.rsqrt(jnp.mean(o[h] * o[h], axis=-1, keepdims=True) + RMS_EPS) * nw
        y = y * _silu(z[:, h * GDN_DV:(h + 1) * GDN_DV])
        y_ref[:, h * GDN_DV:(h + 1) * GDN_DV] = y.astype(BF16)


def _gdn_prompt(hp, n_seq, t_len, conv_w, arow, dtb, norm_w):
    C = GDN_CHUNK
    nc = t_len // C
    return pl.pallas_call(
        _gdn_kernel,
        out_shape=(jax.ShapeDtypeStruct((n_seq * t_len, BRANCH_W), BF16),
                   jax.ShapeDtypeStruct((n_seq, GDN_HEADS, GDN_DK, GDN_DV), F32)),
        grid=(n_seq, nc),
        in_specs=[pl.BlockSpec((C, GDN_QKV), lambda n, c: (n * nc + c, U_AQKV * LANE // GDN_QKV)),
                  pl.BlockSpec((C, BRANCH_W), lambda n, c: (n * nc + c, U_AZ * LANE // BRANCH_W)),
                  pl.BlockSpec((C, LANE), lambda n, c: (n * nc + c, U_AB)),
                  pl.BlockSpec((CONV_W, GDN_QKV), lambda n, c: (0, 0)),
                  pl.BlockSpec((1, LANE), lambda n, c: (0, 0)),
                  pl.BlockSpec((1, LANE), lambda n, c: (0, 0)),
                  pl.BlockSpec((1, GDN_DV), lambda n, c: (0, 0))],
        out_specs=(pl.BlockSpec((C, BRANCH_W), lambda n, c: (n * nc + c, 0)),
                   pl.BlockSpec((1, GDN_HEADS, GDN_DK, GDN_DV), lambda n, c: (n, 0, 0, 0))),
        scratch_shapes=[pltpu.VMEM((8, GDN_QKV), F32)],
        compiler_params=_cparams(("parallel", "arbitrary")),
        name="gdn_prompt",
    )(hp, hp, hp, conv_w, arow, dtb, norm_w)


def _band_mask(rows, cols, window, first_block):
    qi = lax.broadcasted_iota(jnp.int32, (rows, cols), 0) & (BLOCK - 1)
    kj = lax.broadcasted_iota(jnp.int32, (rows, cols), 1) & (2 * BLOCK - 1)
    off = qi + BLOCK - kj
    kmin = jnp.where(first_block, BLOCK, 0)
    return (off >= 0) & (off <= window) & (kj >= kmin)


def _dil_attn_kernel(q_ref, kc_ref, kp_ref, vc_ref, vp_ref, bias_ref, o_ref, lse_ref, *, window):
    hd, n_g = DIL_HD, DIL_G
    rows = n_g * BLOCK
    kvhs = range(DIL_KVH)
    valid = _band_mask(rows, 2 * BLOCK, window, pl.program_id(2) == 0)
    q = q_ref[0]
    kc, kp, vc, vp = kc_ref[0], kp_ref[0], vc_ref[0], vp_ref[0]
    s = []
    for kvh in kvhs:
        qh = jnp.concatenate([q[:, (kvh * n_g + g) * hd:(kvh * n_g + g + 1) * hd] for g in range(n_g)], axis=0)
        kcat = jnp.concatenate([kp[:, kvh * hd:(kvh + 1) * hd], kc[:, kvh * hd:(kvh + 1) * hd]], axis=0)
        sk = _dot_nt(qh, kcat) * (hd ** -0.5) + bias_ref[kvh * n_g:(kvh + 1) * n_g].reshape(rows, 2 * BLOCK)
        s.append(jnp.where(valid, sk, NEG_INF))
    m = [jnp.max(sk, axis=-1, keepdims=True) for sk in s]
    p = [jnp.exp(sk - mk) for sk, mk in zip(s, m)]
    den = [jnp.sum(pk, axis=-1, keepdims=True) for pk in p]
    o = []
    for kvh in kvhs:
        vcat = jnp.concatenate([vp[:, kvh * hd:(kvh + 1) * hd], vc[:, kvh * hd:(kvh + 1) * hd]], axis=0)
        o.append(_dot(p[kvh], vcat) / den[kvh])
    lse_ref[...] = jnp.zeros_like(lse_ref)
    for kvh in kvhs:
        lse = m[kvh] + jnp.log(den[kvh])
        for g in range(n_g):
            hq = kvh * n_g + g
            o_ref[0, :, hq * hd:(hq + 1) * hd] = o[kvh][g * BLOCK:(g + 1) * BLOCK]
            lse_ref[0, :, hq:hq + 1] = lse[g * BLOCK:(g + 1) * BLOCK]


def _swa_attn_kernel(q_ref, kc_ref, kp_ref, vc_ref, vp_ref, sink_ref, gate_ref, y_ref):
    half = SWA_HD
    n_slab = SWA_QH // 2
    slab_per_kvh = n_slab // SWA_KVH
    k2 = jnp.concatenate([kp_ref[0], kc_ref[0]], axis=0)
    v2 = jnp.concatenate([vp_ref[0], vc_ref[0]], axis=0)
    lane = lax.broadcasted_iota(jnp.int32, k2.shape, 1)
    lo = lane < half
    k2r = pltpu.roll(k2, half, 1)
    v2r = pltpu.roll(v2, half, 1)
    kk = [jnp.concatenate([jnp.where(lo, k2, 0.0), jnp.where(lo, 0.0, k2r)], axis=0),
          jnp.concatenate([jnp.where(lo, k2r, 0.0), jnp.where(lo, 0.0, k2)], axis=0)]
    vv = [jnp.concatenate([jnp.where(lo, v2, 0.0), jnp.where(lo, 0.0, v2r)], axis=0),
          jnp.concatenate([jnp.where(lo, v2r, 0.0), jnp.where(lo, 0.0, v2)], axis=0)]
    kk = [a.astype(BF16) for a in kk]
    vv = [a.astype(BF16) for a in vv]
    valid = _band_mask(BLOCK, 4 * BLOCK, SWA_WINDOW, pl.program_id(2) == 0)
    sink = sink_ref[...]
    lane_o = lax.broadcasted_iota(jnp.int32, (BLOCK, LANE), 1) < half
    slabs = range(n_slab)
    s = []
    for a in slabs:
        sa = lax.dot_general(q_ref[0, :, a * LANE:(a + 1) * LANE], kk[a // slab_per_kvh],
                             (((1,), (1,)), ((), ())), preferred_element_type=F32) * (SWA_HD ** -0.5)
        s.append(jnp.where(valid, sa, NEG_INF))
    heads = range(SWA_QH)
    sh = [s[h // 2][:, (h % 2) * 2 * BLOCK:(h % 2 + 1) * 2 * BLOCK] for h in heads]
    snk = [sink[:, h:h + 1] for h in heads]
    m = [jnp.maximum(jnp.max(sh[h], axis=-1, keepdims=True), snk[h]) for h in heads]
    pe = [jnp.exp(sh[h] - m[h]) for h in heads]
    dh = [jnp.sum(pe[h], axis=-1, keepdims=True) + jnp.exp(snk[h] - m[h]) for h in heads]
    for a in slabs:
        p = jnp.concatenate([pe[2 * a], pe[2 * a + 1]], axis=1).astype(BF16)
        den = jnp.where(lane_o, dh[2 * a], dh[2 * a + 1])
        o = jnp.dot(p, vv[a // slab_per_kvh], preferred_element_type=F32) / den
        y_ref[0, :, a * LANE:(a + 1) * LANE] = (o * _silu(gate_ref[0, :, a * LANE:(a + 1) * LANE])).astype(BF16)


def _strided_spec(arr, col, width, n_seq, t_len, dil, prev=False):
    w_tot = arr.shape[1]
    view = arr.reshape(n_seq, t_len // dil, dil * w_tot)
    assert col % width == 0 and w_tot % width == 0
    cb, per_tok = col // width, w_tot // width
    if prev:
        return view, pl.BlockSpec((1, BLOCK, width), lambda n, r, i: (n, jnp.maximum(i - 1, 0), r * per_tok + cb))
    return view, pl.BlockSpec((1, BLOCK, width), lambda n, r, i: (n, i, r * per_tok + cb))


def _qkv_specs(q, qcol, wq, k, kcol, v, vcol, wk, n_seq, t_len, dil):
    ins, specs = [], []
    for arr, col, width, prev in ((q, qcol, wq, False), (k, kcol, wk, False), (k, kcol, wk, True),
                                  (v, vcol, wk, False), (v, vcol, wk, True)):
        view, spec = _strided_spec(arr, col, width, n_seq, t_len, dil, prev)
        ins.append(view)
        specs.append(spec)
    return ins, specs


def _dil_attn(hp, gi, n_seq, t_len, bias):
    win, dil = DIL_GROUPS[gi]
    wq, wk = DIL_QH * DIL_HD, DIL_KVH * DIL_HD
    m = n_seq * t_len
    ins, specs = _qkv_specs(hp, (U_BQ + gi * DIL_QH) * LANE, wq, hp, (U_BK + gi * DIL_KVH) * LANE,
                            hp, (U_BV + gi * DIL_KVH) * LANE, wk, n_seq, t_len, dil)
    o, lse = pl.pallas_call(
        functools.partial(_dil_attn_kernel, window=win // dil),
        out_shape=(jax.ShapeDtypeStruct((n_seq, t_len // dil, dil * wq), F32),
                   jax.ShapeDtypeStruct((n_seq, t_len // dil, dil * LANE), F32)),
        grid=(n_seq, dil, t_len // dil // BLOCK),
        in_specs=specs + [pl.BlockSpec(bias.shape, lambda n, r, i: (0, 0, 0))],
        out_specs=(pl.BlockSpec((1, BLOCK, wq), lambda n, r, i: (n, i, r)),
                   pl.BlockSpec((1, BLOCK, LANE), lambda n, r, i: (n, i, r))),
        compiler_params=_cparams(("parallel", "parallel", "arbitrary")), name=f"dil_attn{gi}",
    )(*ins, bias)
    return o.reshape(m, wq), lse.reshape(m, LANE)


def _swa_attn(rq, rk, hp, sink_row, n_seq, t_len):
    wq, wk = SWA_QH * SWA_HD, SWA_KVH * SWA_HD
    ins, specs = _qkv_specs(rq, 0, wq, rk, 0, hp, U_DV * LANE, wk, n_seq, t_len, 1)
    gview, gspec = _strided_spec(hp, U_DG * LANE, wq, n_seq, t_len, 1)
    y = pl.pallas_call(
        _swa_attn_kernel,
        out_shape=jax.ShapeDtypeStruct((n_seq, t_len, wq), BF16),
        grid=(n_seq, 1, t_len // BLOCK),
        in_specs=specs + [pl.BlockSpec(sink_row.shape, lambda n, r, i: (0, 0)), gspec],
        out_specs=pl.BlockSpec((1, BLOCK, wq), lambda n, r, i: (n, i, 0)),
        compiler_params=_cparams(("parallel", "parallel", "arbitrary")), name="swa_attn",
    )(*ins, sink_row, gview)
    return y.reshape(n_seq * t_len, wq)


def _dil_merge_kernel(o0_ref, o1_ref, o2_ref, l0_ref, l1_ref, l2_ref, g_ref, y_ref):
    l0 = l0_ref[...]
    l1 = l1_ref[...]
    l2 = l2_ref[...]
    m = jnp.maximum(jnp.maximum(l0, l1), l2)
    w0 = jnp.exp(l0 - m)
    w1 = jnp.exp(l1 - m)
    w2 = jnp.exp(l2 - m)
    inv = 1.0 / (w0 + w1 + w2)
    w0 = w0 * inv
    w1 = w1 * inv
    w2 = w2 * inv
    for h in range(DIL_QH):
        sl = slice(h * DIL_HD, (h + 1) * DIL_HD)
        o = (w0[:, h:h + 1] * o0_ref[:, sl] + w1[:, h:h + 1] * o1_ref[:, sl] + w2[:, h:h + 1] * o2_ref[:, sl])
        y_ref[:, sl] = (o * _silu(g_ref[:, sl])).astype(BF16)


def _dil_merge(o0, o1, o2, l0, l1, l2, hp, tm):
    m = o0.shape[0]
    ospec = pl.BlockSpec((tm, BRANCH_W), lambda i: (i, 0))
    lspec = pl.BlockSpec((tm, LANE), lambda i: (i, 0))
    return pl.pallas_call(
        _dil_merge_kernel,
        out_shape=jax.ShapeDtypeStruct((m, BRANCH_W), BF16),
        grid=(m // tm,),
        in_specs=[ospec, ospec, ospec, lspec, lspec, lspec,
                  pl.BlockSpec((tm, BRANCH_W), lambda i: (i, U_BG * LANE // BRANCH_W))],
        out_specs=ospec,
        compiler_params=_cparams(("parallel",)),
        name="dil_merge",
    )(o0, o1, o2, l0, l1, l2, hp)


def _lru_gates(cx, wa_ref, wx_ref, ba, bx, lam):
    ra, rx = [], []
    for b in range(LRU_BLOCKS):
        xb = cx[:, b * LRU_BS:(b + 1) * LRU_BS].astype(BF16)
        ra.append(jnp.dot(xb, wa_ref[b], preferred_element_type=F32))
        rx.append(jnp.dot(xb, wx_ref[b], preferred_element_type=F32))
    r = _sigmoid(jnp.concatenate(ra, axis=1) + ba)
    ig = _sigmoid(jnp.concatenate(rx, axis=1) + bx)
    log_a = -LRU_C * r * _softplus(-lam)
    a = jnp.exp(log_a)
    th = jnp.tanh(log_a)
    bterm = jnp.sqrt(-2.0 * th / (1.0 - th)) * (ig * cx)
    return a, bterm


def _lru_kernel(x_ref, g_ref, cw_ref, cb_ref, wa_ref, wx_ref, ba_ref, bx_ref, lam_ref,
                y_ref, hl_ref, prev_ref, h_ref, a_s, b_s):
    tb = x_ref.shape[0]

    @pl.when(pl.program_id(1) == 0)
    def _():
        prev_ref[...] = jnp.zeros_like(prev_ref)
        h_ref[...] = jnp.zeros_like(h_ref)

    x = x_ref[...]
    cx = _conv_rows(x, prev_ref[...], cw_ref[...]) + cb_ref[...]
    prev_ref[...] = x[tb - 8:tb]
    a, bterm = _lru_gates(cx, wa_ref, wx_ref, ba_ref[...], bx_ref[...], lam_ref[...])
    a_s[...] = a
    b_s[...] = bterm

    def step(t, h):
        h = a_s[pl.ds(t, 1), :] * h + b_s[pl.ds(t, 1), :]
        b_s[pl.ds(t, 1), :] = h
        return h

    h = lax.fori_loop(0, tb, step, h_ref[...], unroll=8)
    h_ref[...] = h
    hl_ref[0] = h
    y_ref[...] = (b_s[...] * _silu(g_ref[...])).astype(BF16)


def _lru_prompt(hp, n_seq, t_len, conv_w, conv_b, wa, wx, ba, bx, lam, tb):
    nb = t_len // tb
    vec = pl.BlockSpec((1, LRU_W), lambda n, c: (0, 0))
    wspec = pl.BlockSpec((LRU_BLOCKS, LRU_BS, LRU_BS), lambda n, c: (0, 0, 0))
    return pl.pallas_call(
        _lru_kernel,
        out_shape=(jax.ShapeDtypeStruct((n_seq * t_len, LRU_W), BF16),
                   jax.ShapeDtypeStruct((n_seq, 1, LRU_W), F32)),
        grid=(n_seq, nb),
        in_specs=[pl.BlockSpec((tb, LRU_W), lambda n, c: (n * nb + c, U_CX * LANE // LRU_W)),
                  pl.BlockSpec((tb, LRU_W), lambda n, c: (n * nb + c, U_CG * LANE // LRU_W)),
                  pl.BlockSpec((CONV_W, LRU_W), lambda n, c: (0, 0)),
                  vec, wspec, wspec, vec, vec, vec],
        out_specs=(pl.BlockSpec((tb, LRU_W), lambda n, c: (n * nb + c, 0)),
                   pl.BlockSpec((1, 1, LRU_W), lambda n, c: (n, 0, 0))),
        scratch_shapes=[pltpu.VMEM((8, LRU_W), F32), pltpu.VMEM((1, LRU_W), F32),
                        pltpu.VMEM((tb, LRU_W), F32), pltpu.VMEM((tb, LRU_W), F32)],
        compiler_params=_cparams(("parallel", "arbitrary")),
        name="lru_prompt",
    )(hp, hp, conv_w, conv_b, wa, wx, ba, bx, lam)


def _rope_kernel(q_ref, k_ref, cos_ref, sin_ref, rq_ref, rk_ref):
    cos = cos_ref[...]
    sin = sin_ref[...]
    k = k_ref[...]
    lane = lax.broadcasted_iota(jnp.int32, k.shape, 1)
    first = (lane & (SWA_HD - 1)) < SWA_HD // 2
    half = SWA_HD // 2
    ks = jnp.where(first, pltpu.roll(k, LANE - half, 1), pltpu.roll(k, half, 1))
    rk_ref[...] = k * cos + ks * sin
    q = q_ref[...]
    wq = q.shape[1]
    reps = wq // LANE
    cosq = jnp.concatenate([cos] * reps, axis=1)
    sinq = jnp.concatenate([sin] * reps, axis=1)
    firstq = jnp.concatenate([first] * reps, axis=1)
    qs = jnp.where(firstq, pltpu.roll(q, wq - half, 1), pltpu.roll(q, half, 1))
    rq_ref[...] = (q * cosq + qs * sinq).astype(rq_ref.dtype)


def _rope_prompt(hp, cos_t, sin_t, t_len, tm):
    m = hp.shape[0]
    nt = t_len // tm
    wq = SWA_QH * SWA_HD
    return pl.pallas_call(
        _rope_kernel,
        out_shape=(jax.ShapeDtypeStruct((m, wq), BF16), jax.ShapeDtypeStruct((m, LANE), F32)),
        grid=(m // tm,),
        in_specs=[pl.BlockSpec((tm, wq), lambda i: (i, U_DQ * LANE // wq)),
                  pl.BlockSpec((tm, LANE), lambda i: (i, U_DK)),
                  pl.BlockSpec((tm, LANE), lambda i: (i % nt, 0)),
                  pl.BlockSpec((tm, LANE), lambda i: (i % nt, 0))],
        out_specs=(pl.BlockSpec((tm, wq), lambda i: (i, 0)), pl.BlockSpec((tm, LANE), lambda i: (i, 0))),
        compiler_params=_cparams(("parallel",)),
        name="rope",
    )(hp, hp, cos_t, sin_t)


def _branch_kernel(ya_ref, yb_ref, yc_ref, yd_ref, g0_ref, g1_ref, g2_ref, g3_ref, wb_ref, o_ref, wbf_ref):
    acc = None
    @pl.when(pl.program_id(1) == 0)
    def _():
        wbf_ref[...] = wb_ref[...].astype(BF16)

    for b, (y_ref, g_ref) in enumerate(((ya_ref, g0_ref), (yb_ref, g1_ref), (yc_ref, g2_ref), (yd_ref, g3_ref))):
        br = jnp.dot(y_ref[...], wbf_ref[b], preferred_element_type=F32)
        term = _sigmoid(g_ref[...]) * br
        acc = term if acc is None else acc + term
    o_ref[...] = acc.astype(BF16)


def _branch_merge(ys, hp, wb, tm, tn):
    m = ys[0].shape[0]
    nj = D_MODEL // tn
    yspec = pl.BlockSpec((tm, BRANCH_W), lambda j, i: (i, 0))
    gspecs = [pl.BlockSpec((tm, tn), functools.partial(lambda j, i, b: (i, U_MG * LANE // tn + b * nj + j), b=b))
              for b in range(N_BRANCH)]
    return pl.pallas_call(
        _branch_kernel,
        out_shape=jax.ShapeDtypeStruct((m, D_MODEL), BF16),
        grid=(nj, m // tm),
        in_specs=[yspec, yspec, yspec, yspec] + gspecs +
                 [pl.BlockSpec((N_BRANCH, BRANCH_W, tn), lambda j, i: (0, 0, j))],
        out_specs=pl.BlockSpec((tm, tn), lambda j, i: (i, j)),
        scratch_shapes=[pltpu.VMEM((N_BRANCH, BRANCH_W, tn), BF16)],
        compiler_params=_cparams(("parallel", "arbitrary")),
        name="branch_merge",
    )(*ys, hp, hp, hp, hp, wb)


def _out_kernel(m_ref, x_ref, w_ref, g_ref, b_ref, o_ref, wbf_ref):
    @pl.when(pl.program_id(0) == 0)
    def _():
        wbf_ref[...] = w_ref[...].astype(BF16)

    f = jnp.dot(m_ref[...], wbf_ref[...], preferred_element_type=F32)
    z = DN_ALPHA * x_ref[...] + f
    mu = jnp.mean(z, axis=-1, keepdims=True)
    zc = z - mu
    var = jnp.mean(zc * zc, axis=-1, keepdims=True)
    o_ref[...] = zc * lax.rsqrt(var + LN_EPS) * g_ref[...] + b_ref[...]


def _out_proj(merged, x, w_out, ln_g, ln_b, tm):
    m = x.shape[0]
    vec = pl.BlockSpec((1, D_MODEL), lambda i: (0, 0))
    return pl.pallas_call(
        _out_kernel,
        out_shape=jax.ShapeDtypeStruct((m, D_MODEL), F32),
        grid=(m // tm,),
        in_specs=[pl.BlockSpec((tm, D_MODEL), lambda i: (i, 0)),
                  pl.BlockSpec((tm, D_MODEL), lambda i: (i, 0)),
                  pl.BlockSpec((D_MODEL, D_MODEL), lambda i: (0, 0), pipeline_mode=pl.Buffered(1)),
                  vec, vec],
        out_specs=pl.BlockSpec((tm, D_MODEL), lambda i: (i, 0)),
        scratch_shapes=[pltpu.VMEM((D_MODEL, D_MODEL), BF16)],
        compiler_params=_cparams(("arbitrary",)),
        name="out_proj",
    )(merged, x, w_out, ln_g, ln_b)


def _gdn_dec_pre_kernel(x_ref, b0_ref, b1_ref, b2_ref, cw_ref, ab_ref, arow_ref, dtb_ref,
                        qkv_ref, beta_ref, eg_ref):
    cw = cw_ref[...]
    y = b0_ref[...] * cw[0:1] + b1_ref[...] * cw[1:2] + b2_ref[...] * cw[2:3] + x_ref[...] * cw[3:4]
    act = _silu(y)
    for h in range(GDN_HEADS):
        sq = slice(h * GDN_DK, (h + 1) * GDN_DK)
        sk = slice((GDN_HEADS + h) * GDN_DK, (GDN_HEADS + h + 1) * GDN_DK)
        q = act[:, sq]
        k = act[:, sk]
        qkv_ref[:, sq] = q * lax.rsqrt(jnp.sum(q * q, axis=-1, keepdims=True) + 1e-6) * (GDN_DK ** -0.5)
        qkv_ref[:, sk] = k * lax.rsqrt(jnp.sum(k * k, axis=-1, keepdims=True) + 1e-6)
    qkv_ref[:, 2 * GDN_HEADS * GDN_DK:] = act[:, 2 * GDN_HEADS * GDN_DK:]
    ab = ab_ref[...]
    beta_ref[...] = _sigmoid(ab)
    eg_ref[...] = jnp.exp(-arow_ref[...] * _softplus(ab + dtb_ref[...]))


def _gdn_dec_pre(x, b0, b1, b2, conv_w, ab, arow, dtb):
    n = x.shape[0]
    full = lambda a: pl.BlockSpec(a.shape, lambda i: (0,) * a.ndim)
    ins = (x, b0, b1, b2, conv_w, ab, arow, dtb)
    return pl.pallas_call(
        _gdn_dec_pre_kernel,
        out_shape=(jax.ShapeDtypeStruct((n, GDN_QKV), F32), jax.ShapeDtypeStruct((n, LANE), F32),
                   jax.ShapeDtypeStruct((n, LANE), F32)),
        grid=(1,),
        in_specs=[full(a) for a in ins],
        out_specs=(pl.BlockSpec((n, GDN_QKV), lambda i: (0, 0)), pl.BlockSpec((n, LANE), lambda i: (0, 0)),
                   pl.BlockSpec((n, LANE), lambda i: (0, 0))),
        compiler_params=_cparams(("arbitrary",)),
        name="gdn_dec_pre",
    )(*ins)


def _gdn_dec_kernel(qt_ref, kt_ref, v_ref, z_ref, beta_ref, eg_ref, nw_ref, s_ref, so_ref, y_ref):
    qt = qt_ref[0]
    kt = kt_ref[0]
    v = v_ref[0]
    z = z_ref[0]
    beta = beta_ref[0]
    eg = eg_ref[0]
    nw = nw_ref[...]
    for h in range(GDN_HEADS):
        s = s_ref[0, h] * eg[:, GDN_HEADS + h:GDN_HEADS + h + 1]
        kcol = kt[:, h:h + 1]
        v_old = jnp.sum(s * kcol, axis=0, keepdims=True)
        delta = (v[h:h + 1, :] - v_old) * beta[:, h:h + 1]
        s = s + kcol * delta
        so_ref[0, h] = s
        o = jnp.sum(s * qt[:, h:h + 1], axis=0, keepdims=True)
        y = o * lax.rsqrt(jnp.mean(o * o, axis=-1, keepdims=True) + RMS_EPS) * nw
        y_ref[0, h:h + 1, :] = y * _silu(z[h:h + 1, :])


def _gdn_dec(qt, kt, v, z, beta, eg, norm_w, state, layer):
    n = qt.shape[0]
    tspec = pl.BlockSpec((1, GDN_DK, GDN_HEADS), lambda i: (i, 0, 0))
    hspec = pl.BlockSpec((1, GDN_HEADS, GDN_DV), lambda i: (i, 0, 0))
    rspec = pl.BlockSpec((1, 1, LANE), lambda i: (i, 0, 0))
    return pl.pallas_call(
        _gdn_dec_kernel,
        out_shape=(jax.ShapeDtypeStruct((n, GDN_HEADS, GDN_DK, GDN_DV), F32),
                   jax.ShapeDtypeStruct((n, GDN_HEADS, GDN_DV), F32)),
        grid=(n,),
        in_specs=[tspec, tspec, hspec, hspec, rspec, rspec,
                  pl.BlockSpec((1, GDN_DV), lambda i: (0, 0)),
                  pl.BlockSpec((None, 1, GDN_HEADS, GDN_DK, GDN_DV), lambda i: (layer, i, 0, 0, 0))],
        out_specs=(pl.BlockSpec((1, GDN_HEADS, GDN_DK, GDN_DV), lambda i: (i, 0, 0, 0)), hspec),
        compiler_params=_cparams(("parallel",)),
        name="gdn_dec",
    )(qt, kt, v, z, beta, eg, norm_w, state)


def _dec_attend(q, kc, vc, knew, vnew, n_g, scale, bias_c=None, bias_n=None, sink=None):
    hq = q.shape[0]
    row = lax.broadcasted_iota(jnp.int32, (hq, 1), 0)
    sc = None
    kn = None
    vn = None
    for kvh in range(len(kc)):
        sel = (row >= kvh * n_g) & (row < (kvh + 1) * n_g)
        s_k = _dot_nt_hi(q, kc[kvh]) * scale
        sc = s_k if sc is None else jnp.where(sel, s_k, sc)
        kn = jnp.broadcast_to(knew[kvh], q.shape) if kn is None else jnp.where(sel, knew[kvh], kn)
        vn = jnp.broadcast_to(vnew[kvh], q.shape) if vn is None else jnp.where(sel, vnew[kvh], vn)
    sn = jnp.sum(q * kn, axis=-1, keepdims=True) * scale
    if bias_c is not None:
        sc = sc + bias_c
        sn = sn + bias_n
    m = jnp.maximum(jnp.max(sc, axis=-1, keepdims=True), sn)
    if sink is not None:
        m = jnp.maximum(m, sink)
    pc = jnp.exp(sc - m)
    pn = jnp.exp(sn - m)
    den = jnp.sum(pc, axis=-1, keepdims=True) + pn
    if sink is not None:
        den = den + jnp.exp(sink - m)
    oc = None
    for kvh in range(len(kc)):
        sel = (row >= kvh * n_g) & (row < (kvh + 1) * n_g)
        o_k = _dot_hi(pc, vc[kvh])
        oc = o_k if oc is None else jnp.where(sel, o_k, oc)
    o = (oc + pn * vn) / den
    return o, m + jnp.log(den)


def _dil_dec_kernel(q_ref, kn_ref, vn_ref, g_ref, c0_ref, c1_ref, c2_ref, bc_ref, bn_ref, y_ref):
    q_all = q_ref[0]
    kn_all = kn_ref[0]
    vn_all = vn_ref[0]
    outs, lses = [], []
    kvw = DIL_KVH * DIL_HD
    for gi, c_ref in enumerate((c0_ref, c1_ref, c2_ref)):
        q = q_all[gi * DIL_QH:(gi + 1) * DIL_QH]
        kc = [c_ref[:, kvh * DIL_HD:(kvh + 1) * DIL_HD] for kvh in range(DIL_KVH)]
        vc = [c_ref[:, kvw + kvh * DIL_HD:kvw + (kvh + 1) * DIL_HD] for kvh in range(DIL_KVH)]
        knew = [kn_all[gi * DIL_KVH + kvh:gi * DIL_KVH + kvh + 1] for kvh in range(DIL_KVH)]
        vnew = [vn_all[gi * DIL_KVH + kvh:gi * DIL_KVH + kvh + 1] for kvh in range(DIL_KVH)]
        o, lse = _dec_attend(q, kc, vc, knew, vnew, DIL_G, DIL_HD ** -0.5, bc_ref[gi], bn_ref[gi])
        outs.append(o)
        lses.append(lse)
    m = jnp.maximum(jnp.maximum(lses[0], lses[1]), lses[2])
    ws = [jnp.exp(l - m) for l in lses]
    inv = 1.0 / (ws[0] + ws[1] + ws[2])
    o = (ws[0] * outs[0] + ws[1] * outs[1] + ws[2] * outs[2]) * inv
    y_ref[0] = o * _silu(g_ref[0])


def _dil_dec(q3, kn3, vn3, g3, caches, layer, bias_c, bias_n):
    n = q3.shape[0]
    kvw2 = 2 * DIL_KVH * DIL_HD
    views, cspecs = [], []
    for (win, dil), c in zip(DIL_GROUPS, caches):
        views.append(c.reshape(DEPTH, n, win // dil, dil * kvw2))
        cspecs.append(pl.BlockSpec((None, None, win // dil, kvw2), lambda i: (layer, i, 0, 0)))
    return pl.pallas_call(
        _dil_dec_kernel,
        out_shape=jax.ShapeDtypeStruct((n, DIL_QH, DIL_HD), F32),
        grid=(n,),
        in_specs=[pl.BlockSpec((1, N_DIL * DIL_QH, DIL_HD), lambda i: (i, 0, 0)),
                  pl.BlockSpec((1, N_DIL * DIL_KVH, DIL_HD), lambda i: (i, 0, 0)),
                  pl.BlockSpec((1, N_DIL * DIL_KVH, DIL_HD), lambda i: (i, 0, 0)),
                  pl.BlockSpec((1, DIL_QH, DIL_HD), lambda i: (i, 0, 0))] + cspecs +
                 [pl.BlockSpec(bias_c.shape, lambda i: (0, 0, 0)),
                  pl.BlockSpec(bias_n.shape, lambda i: (0, 0, 0))],
        out_specs=pl.BlockSpec((1, DIL_QH, DIL_HD), lambda i: (i, 0, 0)),
        compiler_params=_cparams(("parallel",)),
        name="dil_dec",
    )(q3, kn3, vn3, g3, *views, bias_c, bias_n)


def _swa_dec_kernel(q_ref, kn_ref, vn_ref, g_ref, c_ref, sink_ref, cos_ref, sin_ref, perm_ref, y_ref, rk_ref):
    cos = cos_ref[...]
    sin = sin_ref[...]
    perm = perm_ref[...]
    q = q_ref[0]
    kn = kn_ref[0]
    q = q * cos + _dot_hi(q, perm) * sin
    kn = kn * cos + _dot_hi(kn, perm) * sin
    rk_ref[0] = kn
    vn = vn_ref[0]
    kvw = SWA_KVH * SWA_HD
    kc = [c_ref[:, kvh * SWA_HD:(kvh + 1) * SWA_HD] for kvh in range(SWA_KVH)]
    vc = [c_ref[:, kvw + kvh * SWA_HD:kvw + (kvh + 1) * SWA_HD] for kvh in range(SWA_KVH)]
    knew = [kn[kvh:kvh + 1] for kvh in range(SWA_KVH)]
    vnew = [vn[kvh:kvh + 1] for kvh in range(SWA_KVH)]
    o, _ = _dec_attend(q, kc, vc, knew, vnew, SWA_G, SWA_HD ** -0.5, sink=sink_ref[...])
    y_ref[0] = o * _silu(g_ref[0])


def _swa_dec(q3, kn3, vn3, g3, cache, layer, sink_col, cos_d, sin_d, perm):
    n = q3.shape[0]
    kvw2 = 2 * SWA_KVH * SWA_HD
    win = cache.shape[2]
    view = cache.reshape(DEPTH, n, win, kvw2)
    return pl.pallas_call(
        _swa_dec_kernel,
        out_shape=(jax.ShapeDtypeStruct((n, SWA_QH, SWA_HD), F32),
                   jax.ShapeDtypeStruct((n, SWA_KVH, SWA_HD), F32)),
        grid=(n,),
        in_specs=[pl.BlockSpec((1, SWA_QH, SWA_HD), lambda i: (i, 0, 0)),
                  pl.BlockSpec((1, SWA_KVH, SWA_HD), lambda i: (i, 0, 0)),
                  pl.BlockSpec((1, SWA_KVH, SWA_HD), lambda i: (i, 0, 0)),
                  pl.BlockSpec((1, SWA_QH, SWA_HD), lambda i: (i, 0, 0)),
                  pl.BlockSpec((None, None, win, kvw2), lambda i: (layer, i, 0, 0)),
                  pl.BlockSpec(sink_col.shape, lambda i: (0, 0)),
                  pl.BlockSpec(cos_d.shape, lambda i: (0, 0)),
                  pl.BlockSpec(sin_d.shape, lambda i: (0, 0)),
                  pl.BlockSpec(perm.shape, lambda i: (0, 0))],
        out_specs=(pl.BlockSpec((1, SWA_QH, SWA_HD), lambda i: (i, 0, 0)),
                   pl.BlockSpec((1, SWA_KVH, SWA_HD), lambda i: (i, 0, 0))),
        compiler_params=_cparams(("parallel",)),
        name="swa_dec",
    )(q3, kn3, vn3, g3, view, sink_col, cos_d, sin_d, perm)


def _lru_dec_kernel(x_ref, g_ref, b0_ref, b1_ref, b2_ref, h0_ref, cw_ref, cb_ref, wa_ref, wx_ref,
                    ba_ref, bx_ref, lam_ref, y_ref, h_ref):
    cw = cw_ref[...]
    cx = (b0_ref[...] * cw[0:1] + b1_ref[...] * cw[1:2] + b2_ref[...] * cw[2:3] + x_ref[...] * cw[3:4]
          + cb_ref[...])
    a, bterm = _lru_gates(cx, wa_ref, wx_ref, ba_ref[...], bx_ref[...], lam_ref[...])
    h = a * h0_ref[...] + bterm
    h_ref[...] = h
    y_ref[...] = h * _silu(g_ref[...])


def _lru_dec(x, g, b0, b1, b2, h0, conv_w, conv_b, wa, wx, ba, bx, lam):
    n = x.shape[0]
    ins = (x, g, b0, b1, b2, h0, conv_w, conv_b, wa, wx, ba, bx, lam)
    full = lambda a: pl.BlockSpec(a.shape, lambda i: (0,) * a.ndim)
    return pl.pallas_call(
        _lru_dec_kernel,
        out_shape=(jax.ShapeDtypeStruct((n, LRU_W), F32), jax.ShapeDtypeStruct((n, LRU_W), F32)),
        grid=(1,),
        in_specs=[full(a) for a in ins],
        out_specs=(pl.BlockSpec((n, LRU_W), lambda i: (0, 0)), pl.BlockSpec((n, LRU_W), lambda i: (0, 0))),
        compiler_params=_cparams(("arbitrary",)),
        name="lru_dec",
    )(*ins)


def _rel_bucket(dist):
    max_exact = REL_BUCKETS // 2
    n = dist.astype(F32)
    large = max_exact + (jnp.log(jnp.maximum(n, 1.0) / max_exact) / math.log(REL_MAX_DIST / max_exact)
                         * (REL_BUCKETS - max_exact)).astype(jnp.int32)
    large = jnp.minimum(large, REL_BUCKETS - 1)
    return jnp.where(dist < max_exact, dist, large)


def _offset_bias(rel_bias, gi, win, dil):
    j = win // dil + 1
    b = rel_bias[_rel_bucket(dil * jnp.arange(j, dtype=jnp.int32))]
    return b[:, gi * DIL_QH:(gi + 1) * DIL_QH].astype(F32)


def _rope_tables(pos):
    half = SWA_HD // 2
    inv = ROPE_THETA ** (-jnp.arange(half, dtype=F32) / half)
    ang = pos.astype(F32)[:, None] * inv[None, :]
    c, s = jnp.cos(ang), jnp.sin(ang)
    return jnp.concatenate([c, c], axis=1), jnp.concatenate([-s, s], axis=1)


def _shift_kernel(c_ref, n_ref, o_ref):
    w = c_ref.shape[0]
    x = c_ref[...]
    row = lax.broadcasted_iota(jnp.int32, x.shape, 0)
    o_ref[...] = jnp.where(row == w - 1, n_ref[...], pltpu.roll(x, w - 1, 0))


def _shift_append(cache, new_rows):
    d, n, w = cache.shape[:3]
    rw = int(np.prod(cache.shape[3:]))
    out = pl.pallas_call(
        _shift_kernel,
        out_shape=jax.ShapeDtypeStruct((d, n, w, rw), cache.dtype),
        grid=(d, n),
        in_specs=[pl.BlockSpec((None, None, w, rw), lambda a, b: (a, b, 0, 0)),
                  pl.BlockSpec((None, None, 1, rw), lambda a, b: (a, b, 0, 0))],
        out_specs=pl.BlockSpec((None, None, w, rw), lambda a, b: (a, b, 0, 0)),
        compiler_params=_cparams(("parallel", "parallel")),
        name="shift_append",
    )(cache.reshape(d, n, w, rw), new_rows.reshape(d, n, 1, rw))
    return out.reshape(cache.shape)


def _reorder_w_in(w):
    pad = jnp.zeros((w.shape[0], 2 * LANE - 2 * GDN_HEADS), w.dtype)
    parts = [w[:, O_MG:O_END], w[:, O_AZ:O_AB], w[:, O_AQKV:O_AZ], w[:, O_BQ:O_BK], w[:, O_BG:O_CX],
             w[:, O_CX:O_CG], w[:, O_CG:O_DQ], w[:, O_DQ:O_DK], w[:, O_DG:O_MG], w[:, O_BK:O_BV],
             w[:, O_BV:O_BG], w[:, O_DK:O_DV], w[:, O_DV:O_DG], w[:, O_AB:O_BQ], pad]
    return jnp.concatenate(parts, axis=1).astype(BF16)


def _cols(h, unit, width):
    return h[..., unit * LANE:unit * LANE + width]


def kernel(x_prompt, x_sample, state_gdn, state_gdn_conv, cache_dil_w128, cache_dil_w512, cache_dil_w2048,
           cache_swa, state_rglru, state_rglru_conv, w_in, gdn_conv_w, gdn_a_log, gdn_dt_bias, gdn_norm_w,
           lru_conv_w, lru_conv_b, lru_wa, lru_ba, lru_wx, lru_bx, lru_lambda, swa_sink, rel_bias, w_branch,
           w_out, ln_g, ln_b):
    nb, t_len, _ = x_prompt.shape
    ns = x_sample.shape[0]
    mp = nb * t_len
    caches = (cache_dil_w128, cache_dil_w512, cache_dil_w2048)
    xp = x_prompt.reshape(mp, D_MODEL)
    xs = x_sample.reshape(ns, D_MODEL)

    cos64, sin64 = _rope_tables(jnp.arange(t_len))
    cos_t = jnp.concatenate([cos64, cos64], axis=1)
    sin_t = jnp.concatenate([sin64, sin64], axis=1)
    cos_d, sin_d = _rope_tables(jnp.full((1,), PAST_LEN))
    perm = jnp.asarray(np.roll(np.eye(SWA_HD, dtype=np.float32), SWA_HD // 2, axis=0))
    qi = jnp.arange(BLOCK)[:, None]
    kj = jnp.arange(2 * BLOCK)[None, :]
    bias_p, bias_c, bias_n = [], [], []
    for gi, (win, dil) in enumerate(DIL_GROUPS):
        ob = _offset_bias(rel_bias, gi, win, dil)
        jw = win // dil
        onehot = (jnp.clip(qi + BLOCK - kj, 0, jw)[:, :, None] == jnp.arange(jw + 1)).astype(F32)
        bias_p.append(jnp.einsum('qkj,jh->hqk', onehot, ob, precision=lax.Precision.HIGHEST))
        bias_c.append(jnp.transpose(ob[::-1][:jw], (1, 0)))
        bias_n.append(ob[0][:, None])
    bias_c = jnp.stack(bias_c)
    bias_n = jnp.stack(bias_n)

    new_p = [[] for _ in range(8)]
    new_s = [[] for _ in range(8)]
    for l in range(DEPTH):
        w_r = _reorder_w_in(w_in[l])
        wb = w_branch[l]
        wo = w_out[l]
        wa = lru_wa[l].astype(BF16)
        wx = lru_wx[l].astype(BF16)
        zpad = jnp.zeros((1, LANE - 2 * GDN_HEADS), F32)
        arow = jnp.concatenate([jnp.zeros((1, GDN_HEADS), F32), jnp.exp(gdn_a_log[l])[None], zpad], axis=1)
        dtb = jnp.concatenate([jnp.zeros((1, GDN_HEADS), F32), gdn_dt_bias[l][None], zpad], axis=1)
        norm_w = gdn_norm_w[l][None]
        conv_b = lru_conv_b[l][None]
        ba, bx, lam = lru_ba[l][None], lru_bx[l][None], lru_lambda[l][None]
        lng, lnb = ln_g[l][None], ln_b[l][None]
        sink = swa_sink[l].astype(F32)

        hp = _matmul(xp.astype(BF16), w_r, 1024, 1024)
        hp3 = hp.reshape(nb, t_len, D_INP)
        ya, s_p = _gdn_prompt(hp, nb, t_len, gdn_conv_w[l], arow, dtb, norm_w)
        outs, lses = [], []
        for gi in range(N_DIL):
            o_g, l_g = _dil_attn(hp, gi, nb, t_len, bias_p[gi])
            outs.append(o_g)
            lses.append(l_g)
        yb = _dil_merge(outs[0], outs[1], outs[2], lses[0], lses[1], lses[2], hp, 512)
        yc, h_p = _lru_prompt(hp, nb, t_len, lru_conv_w[l], conv_b, wa, wx, ba, bx, lam, 512)
        rq, rk = _rope_prompt(hp, cos_t, sin_t, t_len, 512)
        sink_row = jnp.concatenate([sink[None], jnp.zeros((1, LANE - SWA_QH), F32)], axis=1)
        yd = _swa_attn(rq, rk, hp, sink_row, nb, t_len)
        merged = _branch_merge((ya, yb, yc, yd), hp, wb, 512, 512)
        xp_new = _out_proj(merged, xp, wo, lng, lnb, 256)

        new_p[0].append(s_p)
        new_p[1].append(_cols(hp3, U_AQKV, GDN_QKV)[:, t_len - (CONV_W - 1):])
        for gi, (win, dil) in enumerate(DIL_GROUPS):
            kk = _cols(hp3, U_BK + gi * DIL_KVH, DIL_KVH * DIL_HD)[:, t_len - win:]
            vv = _cols(hp3, U_BV + gi * DIL_KVH, DIL_KVH * DIL_HD)[:, t_len - win:]
            new_p[2 + gi].append(jnp.stack([kk, vv], axis=2).reshape(nb, win, 2, DIL_KVH, DIL_HD))
        kk = rk.reshape(nb, t_len, LANE)[:, t_len - SWA_WINDOW:]
        vv = _cols(hp3, U_DV, LANE)[:, t_len - SWA_WINDOW:]
        new_p[5].append(jnp.stack([kk, vv], axis=2).reshape(nb, SWA_WINDOW, 2, SWA_KVH, SWA_HD))
        new_p[6].append(h_p.reshape(nb, LRU_W))
        new_p[7].append(_cols(hp3, U_CX, LRU_W)[:, t_len - (CONV_W - 1):])

        hs = _matmul(xs.astype(BF16), w_r, ns, 1024)
        gbuf = state_gdn_conv[l]
        a_qkv_s = _cols(hs, U_AQKV, GDN_QKV)
        qkv_n, beta_s, eg_s = _gdn_dec_pre(a_qkv_s, gbuf[:, 0], gbuf[:, 1], gbuf[:, 2], gdn_conv_w[l],
                                           _cols(hs, U_AB, LANE), arow, dtb)
        qkv4 = qkv_n.reshape(ns, 3, GDN_HEADS, GDN_DK)
        s_s, ya_s = _gdn_dec(jnp.swapaxes(qkv4[:, 0], 1, 2), jnp.swapaxes(qkv4[:, 1], 1, 2), qkv4[:, 2],
                             _cols(hs, U_AZ, BRANCH_W).reshape(ns, GDN_HEADS, GDN_DV),
                             beta_s[:, None], eg_s[:, None], norm_w, state_gdn, l)
        bk_s = _cols(hs, U_BK, N_DIL * DIL_KVH * DIL_HD)
        bv_s = _cols(hs, U_BV, N_DIL * DIL_KVH * DIL_HD)
        yb_s = _dil_dec(_cols(hs, U_BQ, N_DIL * BRANCH_W).reshape(ns, N_DIL * DIL_QH, DIL_HD),
                        bk_s.reshape(ns, N_DIL * DIL_KVH, DIL_HD), bv_s.reshape(ns, N_DIL * DIL_KVH, DIL_HD),
                        _cols(hs, U_BG, BRANCH_W).reshape(ns, DIL_QH, DIL_HD), caches, l, bias_c, bias_n)
        lbuf = state_rglru_conv[l]
        cx_s = _cols(hs, U_CX, LRU_W)
        yc_s, h_s = _lru_dec(cx_s, _cols(hs, U_CG, LRU_W), lbuf[:, 0], lbuf[:, 1], lbuf[:, 2], state_rglru[l],
                             lru_conv_w[l], conv_b, wa, wx, ba, bx, lam)
        dv_s = _cols(hs, U_DV, LANE)
        yd_s, rk_s = _swa_dec(_cols(hs, U_DQ, BRANCH_W).reshape(ns, SWA_QH, SWA_HD),
                              _cols(hs, U_DK, LANE).reshape(ns, SWA_KVH, SWA_HD),
                              dv_s.reshape(ns, SWA_KVH, SWA_HD),
                              _cols(hs, U_DG, BRANCH_W).reshape(ns, SWA_QH, SWA_HD),
                              cache_swa, l, sink.reshape(SWA_QH, 1), cos_d, sin_d, perm)
        ys_s = (ya_s.reshape(ns, BRANCH_W).astype(BF16), yb_s.reshape(ns, BRANCH_W).astype(BF16),
                yc_s.astype(BF16), yd_s.reshape(ns, BRANCH_W).astype(BF16))
        merged_s = _branch_merge(ys_s, hs, wb, ns, 512)
        xs_new = _out_proj(merged_s, xs, wo, lng, lnb, ns)

        new_s[0].append(s_s)
        new_s[1].append(jnp.concatenate([gbuf[:, 1:], a_qkv_s[:, None]], axis=1))
        for gi in range(N_DIL):
            kk = bk_s[:, gi * DIL_KVH * DIL_HD:(gi + 1) * DIL_KVH * DIL_HD].reshape(ns, 1, DIL_KVH, DIL_HD)
            vv = bv_s[:, gi * DIL_KVH * DIL_HD:(gi + 1) * DIL_KVH * DIL_HD].reshape(ns, 1, DIL_KVH, DIL_HD)
            new_s[2 + gi].append(jnp.stack([kk, vv], axis=2))
        new_s[5].append(jnp.stack([rk_s[:, None], dv_s.reshape(ns, 1, SWA_KVH, SWA_HD)], axis=2))
        new_s[6].append(h_s)
        new_s[7].append(jnp.concatenate([lbuf[:, 1:], cx_s[:, None]], axis=1))

        xp, xs = xp_new, xs_new

    p = [jnp.stack(v) for v in new_p]
    s = [jnp.stack(new_s[i]) for i in (0, 1)]
    for i, c in zip((2, 3, 4, 5), caches + (cache_swa,)):
        s.append(_shift_append(c, jnp.stack(new_s[i])))
    s += [jnp.stack(new_s[i]) for i in (6, 7)]
    return (xp.reshape(nb, t_len, D_MODEL), xs.reshape(ns, 1, D_MODEL),
            p[0], s[0], p[1], s[1], p[2], s[2], p[3], s[3], p[4], s[4], p[5], s[5], p[6], s[6], p[7], s[7])
```

```python
import functools
import math

import jax
import jax.numpy as jnp
import numpy as np
from jax import lax
from jax.experimental import pallas as pl
from jax.experimental.pallas import tpu as pltpu

F32 = jnp.float32
BF16 = jnp.bfloat16
NEG_INF = -1e30

D_MODEL = 2048
DEPTH = 2
PAST_LEN = 16384
BRANCH_W = D_MODEL // 2
N_BRANCH = 4
CONV_W = 4
BLOCK = 128
GDN_DK = 128
GDN_DV = 128
GDN_HEADS = BRANCH_W // GDN_DV
GDN_QKV = GDN_HEADS * (2 * GDN_DK + GDN_DV)
DIL_GROUPS = ((128, 1), (512, 4), (2048, 16))
N_DIL = len(DIL_GROUPS)
DIL_HD = 128
DIL_QH = BRANCH_W // DIL_HD
DIL_KVH = 2
DIL_G = DIL_QH // DIL_KVH
LRU_W = BRANCH_W
LRU_BLOCKS = 8
LRU_BS = LRU_W // LRU_BLOCKS
LRU_C = 8.0
SWA_HD = 64
SWA_QH = BRANCH_W // SWA_HD
SWA_KVH = 2
SWA_G = SWA_QH // SWA_KVH
SWA_WINDOW = 128
ROPE_THETA = 150000.0
REL_BUCKETS = 32
REL_MAX_DIST = 2048
LN_EPS = 1e-5
RMS_EPS = 1e-6
DN_ALPHA = (2.0 * DEPTH) ** 0.25

IN_SIZES = (GDN_QKV, GDN_HEADS * GDN_DV, GDN_HEADS, GDN_HEADS,
            N_DIL * DIL_QH * DIL_HD, N_DIL * DIL_KVH * DIL_HD, N_DIL * DIL_KVH * DIL_HD, DIL_QH * DIL_HD,
            LRU_W, LRU_W,
            SWA_QH * SWA_HD, SWA_KVH * SWA_HD, SWA_KVH * SWA_HD, SWA_QH * SWA_HD,
            N_BRANCH * D_MODEL)
_OFF = [0]
for _s in IN_SIZES:
    _OFF.append(_OFF[-1] + _s)
(O_AQKV, O_AZ, O_AB, O_AA, O_BQ, O_BK, O_BV, O_BG, O_CX, O_CG, O_DQ, O_DK, O_DV, O_DG, O_MG, O_END) = _OFF

LANE = 128
U_MG, U_AZ, U_AQKV, U_BQ, U_BG, U_CX, U_CG, U_DQ, U_DG, U_BK, U_BV, U_DK, U_DV, U_AB = (
    0, 64, 72, 96, 120, 128, 136, 144, 152, 160, 166, 172, 173, 174)
N_UNITS = 176
D_INP = N_UNITS * LANE
GDN_CHUNK = 128
VMEM_LIMIT = 56 * 1024 * 1024


def _cparams(sem):
    return pltpu.CompilerParams(dimension_semantics=sem, vmem_limit_bytes=VMEM_LIMIT)


def _sigmoid(x):
    return 1.0 / (1.0 + jnp.exp(-x))


def _silu(x):
    return x * _sigmoid(x)


def _softplus(x):
    return jnp.maximum(x, 0.0) + jnp.log1p(jnp.exp(-jnp.abs(x)))


def _dot(a, b):
    return jnp.dot(a.astype(BF16), b.astype(BF16), preferred_element_type=F32)


def _dot_nt(a, b):
    return lax.dot_general(a.astype(BF16), b.astype(BF16), (((1,), (1,)), ((), ())),
                           preferred_element_type=F32)


def _dot_hi(a, b):
    return jnp.dot(a, b, preferred_element_type=F32, precision=lax.Precision.HIGHEST)


def _split2(a):
    hi = a.astype(BF16)
    return hi, (a - hi.astype(F32)).astype(BF16)


def _dot3(a, b):
    ah, al = _split2(a)
    bh, bl = _split2(b)
    return (jnp.dot(ah, bh, preferred_element_type=F32) + jnp.dot(ah, bl, preferred_element_type=F32)
            + jnp.dot(al, bh, preferred_element_type=F32))


def _dot_exact_lhs(a, b):
    a = a.astype(BF16)
    b0 = b.astype(BF16)
    r1 = b - b0.astype(F32)
    b1 = r1.astype(BF16)
    b2 = (r1 - b1.astype(F32)).astype(BF16)
    return (jnp.dot(a, b0, preferred_element_type=F32) + jnp.dot(a, b1, preferred_element_type=F32)
            + jnp.dot(a, b2, preferred_element_type=F32))


def _dot_nt_hi(a, b):
    return lax.dot_general(a, b, (((1,), (1,)), ((), ())), preferred_element_type=F32,
                           precision=lax.Precision.HIGHEST)


def _matmul_kernel(x_ref, w_ref, o_ref):
    o_ref[...] = jnp.dot(x_ref[...], w_ref[...], preferred_element_type=F32)


def _matmul(x, w, tm, tn):
    m, k = x.shape
    n = w.shape[1]
    return pl.pallas_call(
        _matmul_kernel,
        out_shape=jax.ShapeDtypeStruct((m, n), F32),
        grid=(m // tm, n // tn),
        in_specs=[pl.BlockSpec((tm, k), lambda i, j: (i, 0)),
                  pl.BlockSpec((k, tn), lambda i, j: (0, j))],
        out_specs=pl.BlockSpec((tm, tn), lambda i, j: (i, j)),
        compiler_params=_cparams(("parallel", "arbitrary")),
        name="in_proj",
    )(x, w)


def _conv_rows(x, prev8, cw):
    row8 = lax.broadcasted_iota(jnp.int32, (8, x.shape[1]), 0)
    y = x * cw[CONV_W - 1:CONV_W]
    for k in range(1, CONV_W):
        xk = pltpu.roll(x, k, 0)
        fk = pltpu.roll(prev8, k, 0)
        head = jnp.where(row8 < k, fk, xk[0:8])
        xk = jnp.concatenate([head, xk[8:]], axis=0)
        y = y + xk * cw[CONV_W - 1 - k:CONV_W - k]
    return y


def _tri_inv(lmat, ri, ci):
    c = lmat[0].shape[0]
    eye = (ri == ci).astype(F32)
    blk = (ri >> 4) == (ci >> 4)
    d = [jnp.where(blk, l, 0.0) for l in lmat]
    x = [eye - di for di in d]
    p = [_dot(di, di) for di in d]
    x = [xi + _dot(xi, pi) for xi, pi in zip(x, p)]
    p = [_dot(pi, pi) for pi in p]
    x = [xi + _dot(xi, pi) for xi, pi in zip(x, p)]
    p = [_dot(pi, pi) for pi in p]
    x = [xi + _dot(xi, pi) for xi, pi in zip(x, p)]
    sh = 4
    while (1 << sh) < c:
        lower_left = ((ri >> (sh + 1)) == (ci >> (sh + 1))) & (((ri >> sh) & 1) == 1) & (((ci >> sh) & 1) == 0)
        t = [_dot(jnp.where(lower_left, l, 0.0), xi) for l, xi in zip(lmat, x)]
        x = [xi - _dot(xi, ti) for xi, ti in zip(x, t)]
        sh += 1
    return x


def _tri_solve(lmat, rhs, ri, ci):
    tinv = _tri_inv(lmat, ri, ci)
    sol = [_dot(t, r) for t, r in zip(tinv, rhs)]
    resid = [r - s - _dot3(l, s) for r, s, l in zip(rhs, sol, lmat)]
    return [s + _dot(t, r) for s, t, r in zip(sol, tinv, resid)]


def _gdn_kernel(qkv_ref, z_ref, ab_ref, cw_ref, arow_ref, dtb_ref, nw_ref, y_ref, s_ref, prev_ref):
    cidx = pl.program_id(1)
    C = GDN_CHUNK

    @pl.when(cidx == 0)
    def _():
        s_ref[...] = jnp.zeros_like(s_ref)
        prev_ref[...] = jnp.zeros_like(prev_ref)

    x = qkv_ref[...]
    act = _silu(_conv_rows(x, prev_ref[...], cw_ref[...]))
    prev_ref[...] = x[C - 8:C]

    ab = ab_ref[...]
    beta_t = _sigmoid(ab)
    g_t = -arow_ref[...] * _softplus(ab + dtb_ref[...])
    ri = lax.broadcasted_iota(jnp.int32, (C, C), 0)
    ci = lax.broadcasted_iota(jnp.int32, (C, C), 1)
    incl = ri >= ci
    strict = ri > ci
    gc_t = _dot_exact_lhs(incl.astype(F32), g_t)
    gc_tt = gc_t.T
    egc_t = jnp.exp(gc_t)
    nw = nw_ref[...]
    z = z_ref[...]

    heads = range(GDN_HEADS)
    q, k, v, beta, gcc, egc, decay, glc = [], [], [], [], [], [], [], []
    for h in heads:
        qh = act[:, h * GDN_DK:(h + 1) * GDN_DK]
        kh = act[:, (GDN_HEADS + h) * GDN_DK:(GDN_HEADS + h + 1) * GDN_DK]
        q.append(qh * lax.rsqrt(jnp.sum(qh * qh, axis=-1, keepdims=True) + 1e-6) * (GDN_DK ** -0.5))
        k.append(kh * lax.rsqrt(jnp.sum(kh * kh, axis=-1, keepdims=True) + 1e-6))
        v.append(act[:, (2 * GDN_HEADS + h) * GDN_DK:(2 * GDN_HEADS + h + 1) * GDN_DK])
        beta.append(beta_t[:, h:h + 1])
        gcc.append(gc_t[:, GDN_HEADS + h:GDN_HEADS + h + 1])
        gcr = gc_tt[GDN_HEADS + h:GDN_HEADS + h + 1, :]
        egc.append(egc_t[:, GDN_HEADS + h:GDN_HEADS + h + 1])
        glc.append(gc_t[C - 1:C, GDN_HEADS + h:GDN_HEADS + h + 1])
        decay.append(jnp.where(incl, jnp.exp(jnp.where(incl, gcc[h] - gcr, 0.0)), 0.0))
    kb = [k[h] * beta[h] for h in heads]
    lmat = [jnp.where(strict, _dot_nt(kb[h], k[h]) * decay[h], 0.0) for h in heads]
    rhs = [jnp.concatenate([v[h] * beta[h], kb[h] * egc[h]], axis=1) for h in heads]
    sol = _tri_solve(lmat, rhs, ri, ci)
    qk = [_dot_nt(q[h], k[h]) * decay[h] for h in heads]
    s = [s_ref[0, h] for h in heads]
    v_new = [sol[h][:, :GDN_DV] - _dot(sol[h][:, GDN_DV:], s[h]) for h in heads]
    o = [_dot(q[h] * egc[h], s[h]) + _dot(qk[h], v_new[h]) for h in heads]
    for h in heads:
        kd = k[h] * jnp.exp(glc[h] - gcc[h])
        s_ref[0, h] = s[h] * jnp.exp(glc[h]) + _dot(kd.T, v_new[h])
    for h in heads:
        y = o[h] * lax.rsqrt(jnp.mean(o[h] * o[h], axis=-1, keepdims=True) + RMS_EPS) * nw
        y = y * _silu(z[:, h * GDN_DV:(h + 1) * GDN_DV])
        y_ref[:, h * GDN_DV:(h + 1) * GDN_DV] = y.astype(BF16)


def _gdn_prompt(hp, n_seq, t_len, conv_w, arow, dtb, norm_w):
    C = GDN_CHUNK
    nc = t_len // C
    return pl.pallas_call(
        _gdn_kernel,
        out_shape=(jax.ShapeDtypeStruct((n_seq * t_len, BRANCH_W), BF16),
                   jax.ShapeDtypeStruct((n_seq, GDN_HEADS, GDN_DK, GDN_DV), F32)),
        grid=(n_seq, nc),
        in_specs=[pl.BlockSpec((C, GDN_QKV), lambda n, c: (n * nc + c, U_AQKV * LANE // GDN_QKV)),
                  pl.BlockSpec((C, BRANCH_W), lambda n, c: (n * nc + c, U_AZ * LANE // BRANCH_W)),
                  pl.BlockSpec((C, LANE), lambda n, c: (n * nc + c, U_AB)),
                  pl.BlockSpec((CONV_W, GDN_QKV), lambda n, c: (0, 0)),
                  pl.BlockSpec((1, LANE), lambda n, c: (0, 0)),
                  pl.BlockSpec((1, LANE), lambda n, c: (0, 0)),
                  pl.BlockSpec((1, GDN_DV), lambda n, c: (0, 0))],
        out_specs=(pl.BlockSpec((C, BRANCH_W), lambda n, c: (n * nc + c, 0)),
                   pl.BlockSpec((1, GDN_HEADS, GDN_DK, GDN_DV), lambda n, c: (n, 0, 0, 0))),
        scratch_shapes=[pltpu.VMEM((8, GDN_QKV), F32)],
        compiler_params=_cparams(("parallel", "arbitrary")),
        name="gdn_prompt",
    )(hp, hp, hp, conv_w, arow, dtb, norm_w)


def _band_mask(rows, cols, window, first_block):
    qi = lax.broadcasted_iota(jnp.int32, (rows, cols), 0) & (BLOCK - 1)
    kj = lax.broadcasted_iota(jnp.int32, (rows, cols), 1) & (2 * BLOCK - 1)
    off = qi + BLOCK - kj
    kmin = jnp.where(first_block, BLOCK, 0)
    return (off >= 0) & (off <= window) & (kj >= kmin)


def _dil_attn_kernel(*refs, window, dil):
    hd, n_g = DIL_HD, DIL_G
    q_refs = refs[:n_g]
    kc_ref, kp_ref, vc_ref, vp_ref, bias_ref, o_ref, lse_ref = refs[n_g:n_g + 7]
    o_scr = refs[n_g + 7:]
    rows = n_g * BLOCK
    valid = _band_mask(rows, 2 * BLOCK, window, pl.program_id(1) == 0)
    bias = bias_ref[...].reshape(rows, 2 * BLOCK)
    lane = lax.broadcasted_iota(jnp.int32, (BLOCK, LANE), 1)

    def sub(r):
        return pl.ds(r, BLOCK, stride=dil) if dil > 1 else slice(None)

    for r in range(dil):
        qh = jnp.concatenate([q_refs[g][sub(r), :] for g in range(n_g)], axis=0)
        kcat = jnp.concatenate([kp_ref[sub(r), :], kc_ref[sub(r), :]], axis=0)
        vcat = jnp.concatenate([vp_ref[sub(r), :], vc_ref[sub(r), :]], axis=0)
        s = jnp.where(valid, _dot_nt(qh, kcat) * (hd ** -0.5) + bias, NEG_INF)
        m = jnp.max(s, axis=-1, keepdims=True)
        p = jnp.exp(s - m)
        den = jnp.sum(p, axis=-1, keepdims=True)
        o = _dot(p, vcat) / den
        lse = m + jnp.log(den)
        lse_t = jnp.zeros((BLOCK, LANE), F32)
        for g in range(n_g):
            o_scr[g][sub(r), :] = o[g * BLOCK:(g + 1) * BLOCK]
            lse_t = jnp.where(lane == g, lse[g * BLOCK:(g + 1) * BLOCK], lse_t)
        lse_ref[sub(r), :] = lse_t
    for g in range(n_g):
        o_ref[:, g * hd:(g + 1) * hd] = o_scr[g][...]


def _swa_attn_kernel(q_ref, kc_ref, kp_ref, vc_ref, vp_ref, sink_ref, gate_ref, y_ref):
    half = SWA_HD
    n_slab = SWA_QH // 2
    slab_per_kvh = n_slab // SWA_KVH
    k2 = jnp.concatenate([kp_ref[...], kc_ref[...]], axis=0)
    v2 = jnp.concatenate([vp_ref[...], vc_ref[...]], axis=0)
    lane = lax.broadcasted_iota(jnp.int32, k2.shape, 1)
    lo = lane < half
    k2r = pltpu.roll(k2, half, 1)
    v2r = pltpu.roll(v2, half, 1)
    kk = [jnp.concatenate([jnp.where(lo, k2, 0.0), jnp.where(lo, 0.0, k2r)], axis=0),
          jnp.concatenate([jnp.where(lo, k2r, 0.0), jnp.where(lo, 0.0, k2)], axis=0)]
    vv = [jnp.concatenate([jnp.where(lo, v2, 0.0), jnp.where(lo, 0.0, v2r)], axis=0),
          jnp.concatenate([jnp.where(lo, v2r, 0.0), jnp.where(lo, 0.0, v2)], axis=0)]
    kk = [a.astype(BF16) for a in kk]
    vv = [a.astype(BF16) for a in vv]
    valid = _band_mask(BLOCK, 4 * BLOCK, SWA_WINDOW, pl.program_id(1) == 0)
    sink = sink_ref[...]
    lane_o = lax.broadcasted_iota(jnp.int32, (BLOCK, LANE), 1) < half
    slabs = range(n_slab)
    s = []
    for a in slabs:
        sa = lax.dot_general(q_ref[:, a * LANE:(a + 1) * LANE], kk[a // slab_per_kvh],
                             (((1,), (1,)), ((), ())), preferred_element_type=F32) * (SWA_HD ** -0.5)
        s.append(jnp.where(valid, sa, NEG_INF))
    heads = range(SWA_QH)
    sh = [s[h // 2][:, (h % 2) * 2 * BLOCK:(h % 2 + 1) * 2 * BLOCK] for h in heads]
    snk = [sink[:, h:h + 1] for h in heads]
    m = [jnp.maximum(jnp.max(sh[h], axis=-1, keepdims=True), snk[h]) for h in heads]
    pe = [jnp.exp(sh[h] - m[h]) for h in heads]
    dh = [jnp.sum(pe[h], axis=-1, keepdims=True) + jnp.exp(snk[h] - m[h]) for h in heads]
    for a in slabs:
        p = jnp.concatenate([pe[2 * a], pe[2 * a + 1]], axis=1).astype(BF16)
        den = jnp.where(lane_o, dh[2 * a], dh[2 * a + 1])
        o = jnp.dot(p, vv[a // slab_per_kvh], preferred_element_type=F32) / den
        y_ref[:, a * LANE:(a + 1) * LANE] = (o * _silu(gate_ref[:, a * LANE:(a + 1) * LANE])).astype(BF16)


def _row_specs(rows, nblk, width, cb):
    cur = pl.BlockSpec((rows, width), lambda n, i, *_: (n * nblk + i, cb))
    prev = pl.BlockSpec((rows, width), lambda n, i, *_: (n * nblk + jnp.maximum(i - 1, 0), cb))
    return cur, prev


def _dil_attn(hp, gi, n_seq, t_len, bias):
    win, dil = DIL_GROUPS[gi]
    wq = DIL_G * DIL_HD
    rows = BLOCK * dil
    nblk = t_len // rows
    m = n_seq * t_len
    qspecs = [pl.BlockSpec((rows, DIL_HD), functools.partial(
        lambda n, i, h, g: (n * nblk + i, U_BQ + gi * DIL_QH + h * DIL_G + g), g=g)) for g in range(DIL_G)]
    kcur, kprev = (pl.BlockSpec((rows, DIL_HD), functools.partial(
        lambda n, i, h, back: (n * nblk + jnp.maximum(i - back, 0), U_BK + gi * DIL_KVH + h), back=b))
        for b in (0, 1))
    vcur, vprev = (pl.BlockSpec((rows, DIL_HD), functools.partial(
        lambda n, i, h, back: (n * nblk + jnp.maximum(i - back, 0), U_BV + gi * DIL_KVH + h), back=b))
        for b in (0, 1))
    return pl.pallas_call(
        functools.partial(_dil_attn_kernel, window=win // dil, dil=dil),
        out_shape=(jax.ShapeDtypeStruct((m, DIL_QH * DIL_HD), F32),
                   jax.ShapeDtypeStruct((m, DIL_KVH * LANE), F32)),
        grid=(n_seq, nblk, DIL_KVH),
        in_specs=qspecs + [kcur, kprev, vcur, vprev,
                           pl.BlockSpec((DIL_G, BLOCK, 2 * BLOCK), lambda n, i, h: (h, 0, 0))],
        out_specs=(pl.BlockSpec((rows, wq), lambda n, i, h: (n * nblk + i, h)),
                   pl.BlockSpec((rows, LANE), lambda n, i, h: (n * nblk + i, h))),
        scratch_shapes=[pltpu.VMEM((rows, DIL_HD), F32)] * DIL_G,
        compiler_params=_cparams(("parallel", "arbitrary", "arbitrary")), name=f"dil_attn{gi}",
    )(*([hp] * (DIL_G + 4)), bias)


def _swa_attn(rq, rk, hp, sink_row, n_seq, t_len):
    wq, wk = SWA_QH * SWA_HD, SWA_KVH * SWA_HD
    nblk = t_len // BLOCK
    qspec, _ = _row_specs(BLOCK, nblk, wq, 0)
    kcur, kprev = _row_specs(BLOCK, nblk, wk, 0)
    vcur, vprev = _row_specs(BLOCK, nblk, wk, U_DV)
    gspec, _ = _row_specs(BLOCK, nblk, wq, U_DG * LANE // wq)
    return pl.pallas_call(
        _swa_attn_kernel,
        out_shape=jax.ShapeDtypeStruct((n_seq * t_len, wq), BF16),
        grid=(n_seq, nblk),
        in_specs=[qspec, kcur, kprev, vcur, vprev, pl.BlockSpec(sink_row.shape, lambda n, i: (0, 0)), gspec],
        out_specs=qspec,
        compiler_params=_cparams(("parallel", "arbitrary")), name="swa_attn",
    )(rq, rk, rk, hp, hp, sink_row, hp)


def _dil_merge_kernel(o0_ref, o1_ref, o2_ref, l0_ref, l1_ref, l2_ref, g_ref, y_ref):
    l0 = l0_ref[...]
    l1 = l1_ref[...]
    l2 = l2_ref[...]
    m = jnp.maximum(jnp.maximum(l0, l1), l2)
    w0 = jnp.exp(l0 - m)
    w1 = jnp.exp(l1 - m)
    w2 = jnp.exp(l2 - m)
    inv = 1.0 / (w0 + w1 + w2)
    w0 = w0 * inv
    w1 = w1 * inv
    w2 = w2 * inv
    for h in range(DIL_QH):
        sl = slice(h * DIL_HD, (h + 1) * DIL_HD)
        c = (h // DIL_G) * LANE + h % DIL_G
        o = (w0[:, c:c + 1] * o0_ref[:, sl] + w1[:, c:c + 1] * o1_ref[:, sl] + w2[:, c:c + 1] * o2_ref[:, sl])
        y_ref[:, sl] = (o * _silu(g_ref[:, sl])).astype(BF16)


def _dil_merge(o0, o1, o2, l0, l1, l2, hp, tm):
    m = o0.shape[0]
    ospec = pl.BlockSpec((tm, BRANCH_W), lambda i: (i, 0))
    lspec = pl.BlockSpec((tm, DIL_KVH * LANE), lambda i: (i, 0))
    return pl.pallas_call(
        _dil_merge_kernel,
        out_shape=jax.ShapeDtypeStruct((m, BRANCH_W), BF16),
        grid=(m // tm,),
        in_specs=[ospec, ospec, ospec, lspec, lspec, lspec,
                  pl.BlockSpec((tm, BRANCH_W), lambda i: (i, U_BG * LANE // BRANCH_W))],
        out_specs=ospec,
        compiler_params=_cparams(("parallel",)),
        name="dil_merge",
    )(o0, o1, o2, l0, l1, l2, hp)


def _lru_gates(cx, wa_ref, wx_ref, ba, bx, lam):
    ra, rx = [], []
    for b in range(LRU_BLOCKS):
        xb = cx[:, b * LRU_BS:(b + 1) * LRU_BS].astype(BF16)
        ra.append(jnp.dot(xb, wa_ref[b], preferred_element_type=F32))
        rx.append(jnp.dot(xb, wx_ref[b], preferred_element_type=F32))
    r = _sigmoid(jnp.concatenate(ra, axis=1) + ba)
    ig = _sigmoid(jnp.concatenate(rx, axis=1) + bx)
    log_a = -LRU_C * r * _softplus(-lam)
    a = jnp.exp(log_a)
    th = jnp.tanh(log_a)
    bterm = jnp.sqrt(-2.0 * th / (1.0 - th)) * (ig * cx)
    return a, bterm


def _lru_kernel(x_ref, g_ref, cw_ref, cb_ref, wa_ref, wx_ref, ba_ref, bx_ref, lam_ref,
                y_ref, hl_ref, prev_ref, h_ref, a_s, b_s):
    tb = x_ref.shape[0]

    @pl.when(pl.program_id(1) == 0)
    def _():
        prev_ref[...] = jnp.zeros_like(prev_ref)
        h_ref[...] = jnp.zeros_like(h_ref)

    x = x_ref[...]
    cx = _conv_rows(x, prev_ref[...], cw_ref[...]) + cb_ref[...]
    prev_ref[...] = x[tb - 8:tb]
    a, bterm = _lru_gates(cx, wa_ref, wx_ref, ba_ref[...], bx_ref[...], lam_ref[...])
    a_s[...] = a
    b_s[...] = bterm

    def step(t, h):
        h = a_s[pl.ds(t, 1), :] * h + b_s[pl.ds(t, 1), :]
        b_s[pl.ds(t, 1), :] = h
        return h

    h = lax.fori_loop(0, tb, step, h_ref[...], unroll=8)
    h_ref[...] = h
    hl_ref[0] = h
    y_ref[...] = (b_s[...] * _silu(g_ref[...])).astype(BF16)


def _lru_prompt(hp, n_seq, t_len, conv_w, conv_b, wa, wx, ba, bx, lam, tb):
    nb = t_len // tb
    vec = pl.BlockSpec((1, LRU_W), lambda n, c: (0, 0))
    wspec = pl.BlockSpec((LRU_BLOCKS, LRU_BS, LRU_BS), lambda n, c: (0, 0, 0))
    return pl.pallas_call(
        _lru_kernel,
        out_shape=(jax.ShapeDtypeStruct((n_seq * t_len, LRU_W), BF16),
                   jax.ShapeDtypeStruct((n_seq, 1, LRU_W), F32)),
        grid=(n_seq, nb),
        in_specs=[pl.BlockSpec((tb, LRU_W), lambda n, c: (n * nb + c, U_CX * LANE // LRU_W)),
                  pl.BlockSpec((tb, LRU_W), lambda n, c: (n * nb + c, U_CG * LANE // LRU_W)),
                  pl.BlockSpec((CONV_W, LRU_W), lambda n, c: (0, 0)),
                  vec, wspec, wspec, vec, vec, vec],
        out_specs=(pl.BlockSpec((tb, LRU_W), lambda n, c: (n * nb + c, 0)),
                   pl.BlockSpec((1, 1, LRU_W), lambda n, c: (n, 0, 0))),
        scratch_shapes=[pltpu.VMEM((8, LRU_W), F32), pltpu.VMEM((1, LRU_W), F32),
                        pltpu.VMEM((tb, LRU_W), F32), pltpu.VMEM((tb, LRU_W), F32)],
        compiler_params=_cparams(("parallel", "arbitrary")),
        name="lru_prompt",
    )(hp, hp, conv_w, conv_b, wa, wx, ba, bx, lam)


def _rope_kernel(q_ref, k_ref, cos_ref, sin_ref, rq_ref, rk_ref):
    cos = cos_ref[...]
    sin = sin_ref[...]
    k = k_ref[...]
    lane = lax.broadcasted_iota(jnp.int32, k.shape, 1)
    first = (lane & (SWA_HD - 1)) < SWA_HD // 2
    half = SWA_HD // 2
    ks = jnp.where(first, pltpu.roll(k, LANE - half, 1), pltpu.roll(k, half, 1))
    rk_ref[...] = k * cos + ks * sin
    q = q_ref[...]
    wq = q.shape[1]
    reps = wq // LANE
    cosq = jnp.concatenate([cos] * reps, axis=1)
    sinq = jnp.concatenate([sin] * reps, axis=1)
    firstq = jnp.concatenate([first] * reps, axis=1)
    qs = jnp.where(firstq, pltpu.roll(q, wq - half, 1), pltpu.roll(q, half, 1))
    rq_ref[...] = (q * cosq + qs * sinq).astype(rq_ref.dtype)


def _rope_prompt(hp, cos_t, sin_t, t_len, tm):
    m = hp.shape[0]
    nt = t_len // tm
    wq = SWA_QH * SWA_HD
    return pl.pallas_call(
        _rope_kernel,
        out_shape=(jax.ShapeDtypeStruct((m, wq), BF16), jax.ShapeDtypeStruct((m, LANE), F32)),
        grid=(m // tm,),
        in_specs=[pl.BlockSpec((tm, wq), lambda i: (i, U_DQ * LANE // wq)),
                  pl.BlockSpec((tm, LANE), lambda i: (i, U_DK)),
                  pl.BlockSpec((tm, LANE), lambda i: (i % nt, 0)),
                  pl.BlockSpec((tm, LANE), lambda i: (i % nt, 0))],
        out_specs=(pl.BlockSpec((tm, wq), lambda i: (i, 0)), pl.BlockSpec((tm, LANE), lambda i: (i, 0))),
        compiler_params=_cparams(("parallel",)),
        name="rope",
    )(hp, hp, cos_t, sin_t)


def _branch_kernel(ya_ref, yb_ref, yc_ref, yd_ref, g0_ref, g1_ref, g2_ref, g3_ref, wb_ref, o_ref, wbf_ref):
    acc = None
    @pl.when(pl.program_id(1) == 0)
    def _():
        wbf_ref[...] = wb_ref[...].astype(BF16)

    for b, (y_ref, g_ref) in enumerate(((ya_ref, g0_ref), (yb_ref, g1_ref), (yc_ref, g2_ref), (yd_ref, g3_ref))):
        br = jnp.dot(y_ref[...], wbf_ref[b], preferred_element_type=F32)
        term = _sigmoid(g_ref[...]) * br
        acc = term if acc is None else acc + term
    o_ref[...] = acc.astype(BF16)


def _branch_merge(ys, hp, wb, tm, tn):
    m = ys[0].shape[0]
    nj = D_MODEL // tn
    yspec = pl.BlockSpec((tm, BRANCH_W), lambda j, i: (i, 0))
    gspecs = [pl.BlockSpec((tm, tn), functools.partial(lambda j, i, b: (i, U_MG * LANE // tn + b * nj + j), b=b))
              for b in range(N_BRANCH)]
    return pl.pallas_call(
        _branch_kernel,
        out_shape=jax.ShapeDtypeStruct((m, D_MODEL), BF16),
        grid=(nj, m // tm),
        in_specs=[yspec, yspec, yspec, yspec] + gspecs +
                 [pl.BlockSpec((N_BRANCH, BRANCH_W, tn), lambda j, i: (0, 0, j))],
        out_specs=pl.BlockSpec((tm, tn), lambda j, i: (i, j)),
        scratch_shapes=[pltpu.VMEM((N_BRANCH, BRANCH_W, tn), BF16)],
        compiler_params=_cparams(("parallel", "arbitrary")),
        name="branch_merge",
    )(*ys, hp, hp, hp, hp, wb)


def _out_kernel(m_ref, x_ref, w_ref, g_ref, b_ref, o_ref, wbf_ref):
    @pl.when(pl.program_id(0) == 0)
    def _():
        wbf_ref[...] = w_ref[...].astype(BF16)

    f = jnp.dot(m_ref[...], wbf_ref[...], preferred_element_type=F32)
    z = DN_ALPHA * x_ref[...] + f
    mu = jnp.mean(z, axis=-1, keepdims=True)
    zc = z - mu
    var = jnp.mean(zc * zc, axis=-1, keepdims=True)
    o_ref[...] = zc * lax.rsqrt(var + LN_EPS) * g_ref[...] + b_ref[...]


def _out_proj(merged, x, w_out, ln_g, ln_b, tm):
    m = x.shape[0]
    vec = pl.BlockSpec((1, D_MODEL), lambda i: (0, 0))
    return pl.pallas_call(
        _out_kernel,
        out_shape=jax.ShapeDtypeStruct((m, D_MODEL), F32),
        grid=(m // tm,),
        in_specs=[pl.BlockSpec((tm, D_MODEL), lambda i: (i, 0)),
                  pl.BlockSpec((tm, D_MODEL), lambda i: (i, 0)),
                  pl.BlockSpec((D_MODEL, D_MODEL), lambda i: (0, 0), pipeline_mode=pl.Buffered(1)),
                  vec, vec],
        out_specs=pl.BlockSpec((tm, D_MODEL), lambda i: (i, 0)),
        scratch_shapes=[pltpu.VMEM((D_MODEL, D_MODEL), BF16)],
        compiler_params=_cparams(("arbitrary",)),
        name="out_proj",
    )(merged, x, w_out, ln_g, ln_b)


def _gdn_dec_pre_kernel(x_ref, b0_ref, b1_ref, b2_ref, cw_ref, ab_ref, arow_ref, dtb_ref,
                        qkv_ref, beta_ref, eg_ref):
    cw = cw_ref[...]
    y = b0_ref[...] * cw[0:1] + b1_ref[...] * cw[1:2] + b2_ref[...] * cw[2:3] + x_ref[...] * cw[3:4]
    act = _silu(y)
    for h in range(GDN_HEADS):
        sq = slice(h * GDN_DK, (h + 1) * GDN_DK)
        sk = slice((GDN_HEADS + h) * GDN_DK, (GDN_HEADS + h + 1) * GDN_DK)
        q = act[:, sq]
        k = act[:, sk]
        qkv_ref[:, sq] = q * lax.rsqrt(jnp.sum(q * q, axis=-1, keepdims=True) + 1e-6) * (GDN_DK ** -0.5)
        qkv_ref[:, sk] = k * lax.rsqrt(jnp.sum(k * k, axis=-1, keepdims=True) + 1e-6)
    qkv_ref[:, 2 * GDN_HEADS * GDN_DK:] = act[:, 2 * GDN_HEADS * GDN_DK:]
    ab = ab_ref[...]
    beta_ref[...] = _sigmoid(ab)
    eg_ref[...] = jnp.exp(-arow_ref[...] * _softplus(ab + dtb_ref[...]))


def _gdn_dec_pre(x, b0, b1, b2, conv_w, ab, arow, dtb):
    n = x.shape[0]
    full = lambda a: pl.BlockSpec(a.shape, lambda i: (0,) * a.ndim)
    ins = (x, b0, b1, b2, conv_w, ab, arow, dtb)
    return pl.pallas_call(
        _gdn_dec_pre_kernel,
        out_shape=(jax.ShapeDtypeStruct((n, GDN_QKV), F32), jax.ShapeDtypeStruct((n, LANE), F32),
                   jax.ShapeDtypeStruct((n, LANE), F32)),
        grid=(1,),
        in_specs=[full(a) for a in ins],
        out_specs=(pl.BlockSpec((n, GDN_QKV), lambda i: (0, 0)), pl.BlockSpec((n, LANE), lambda i: (0, 0)),
                   pl.BlockSpec((n, LANE), lambda i: (0, 0))),
        compiler_params=_cparams(("arbitrary",)),
        name="gdn_dec_pre",
    )(*ins)


def _gdn_dec_kernel(qt_ref, kt_ref, v_ref, z_ref, beta_ref, eg_ref, nw_ref, s_ref, so_ref, y_ref):
    qt = qt_ref[0]
    kt = kt_ref[0]
    v = v_ref[0]
    z = z_ref[0]
    beta = beta_ref[0]
    eg = eg_ref[0]
    nw = nw_ref[...]
    for h in range(GDN_HEADS):
        s = s_ref[0, h] * eg[:, GDN_HEADS + h:GDN_HEADS + h + 1]
        kcol = kt[:, h:h + 1]
        v_old = jnp.sum(s * kcol, axis=0, keepdims=True)
        delta = (v[h:h + 1, :] - v_old) * beta[:, h:h + 1]
        s = s + kcol * delta
        so_ref[0, h] = s
        o = jnp.sum(s * qt[:, h:h + 1], axis=0, keepdims=True)
        y = o * lax.rsqrt(jnp.mean(o * o, axis=-1, keepdims=True) + RMS_EPS) * nw
        y_ref[0, h:h + 1, :] = y * _silu(z[h:h + 1, :])


def _gdn_dec(qt, kt, v, z, beta, eg, norm_w, state, layer):
    n = qt.shape[0]
    tspec = pl.BlockSpec((1, GDN_DK, GDN_HEADS), lambda i: (i, 0, 0))
    hspec = pl.BlockSpec((1, GDN_HEADS, GDN_DV), lambda i: (i, 0, 0))
    rspec = pl.BlockSpec((1, 1, LANE), lambda i: (i, 0, 0))
    return pl.pallas_call(
        _gdn_dec_kernel,
        out_shape=(jax.ShapeDtypeStruct((n, GDN_HEADS, GDN_DK, GDN_DV), F32),
                   jax.ShapeDtypeStruct((n, GDN_HEADS, GDN_DV), F32)),
        grid=(n,),
        in_specs=[tspec, tspec, hspec, hspec, rspec, rspec,
                  pl.BlockSpec((1, GDN_DV), lambda i: (0, 0)),
                  pl.BlockSpec((None, 1, GDN_HEADS, GDN_DK, GDN_DV), lambda i: (layer, i, 0, 0, 0))],
        out_specs=(pl.BlockSpec((1, GDN_HEADS, GDN_DK, GDN_DV), lambda i: (i, 0, 0, 0)), hspec),
        compiler_params=_cparams(("parallel",)),
        name="gdn_dec",
    )(qt, kt, v, z, beta, eg, norm_w, state)


def _dec_attend(q, kc, vc, knew, vnew, n_g, scale, bias_c=None, bias_n=None, sink=None):
    hq = q.shape[0]
    row = lax.broadcasted_iota(jnp.int32, (hq, 1), 0)
    sc = None
    kn = None
    vn = None
    for kvh in range(len(kc)):
        sel = (row >= kvh * n_g) & (row < (kvh + 1) * n_g)
        s_k = _dot_nt_hi(q, kc[kvh]) * scale
        sc = s_k if sc is None else jnp.where(sel, s_k, sc)
        kn = jnp.broadcast_to(knew[kvh], q.shape) if kn is None else jnp.where(sel, knew[kvh], kn)
        vn = jnp.broadcast_to(vnew[kvh], q.shape) if vn is None else jnp.where(sel, vnew[kvh], vn)
    sn = jnp.sum(q * kn, axis=-1, keepdims=True) * scale
    if bias_c is not None:
        sc = sc + bias_c
        sn = sn + bias_n
    m = jnp.maximum(jnp.max(sc, axis=-1, keepdims=True), sn)
    if sink is not None:
        m = jnp.maximum(m, sink)
    pc = jnp.exp(sc - m)
    pn = jnp.exp(sn - m)
    den = jnp.sum(pc, axis=-1, keepdims=True) + pn
    if sink is not None:
        den = den + jnp.exp(sink - m)
    oc = None
    for kvh in range(len(kc)):
        sel = (row >= kvh * n_g) & (row < (kvh + 1) * n_g)
        o_k = _dot_hi(pc, vc[kvh])
        oc = o_k if oc is None else jnp.where(sel, o_k, oc)
    o = (oc + pn * vn) / den
    return o, m + jnp.log(den)


def _dil_dec_kernel(q_ref, kn_ref, vn_ref, g_ref, c0_ref, c1_ref, c2_ref, bc_ref, bn_ref, y_ref):
    q_all = q_ref[0]
    kn_all = kn_ref[0]
    vn_all = vn_ref[0]
    outs, lses = [], []
    kvw = DIL_KVH * DIL_HD
    for gi, c_ref in enumerate((c0_ref, c1_ref, c2_ref)):
        q = q_all[gi * DIL_QH:(gi + 1) * DIL_QH]
        kc = [c_ref[:, kvh * DIL_HD:(kvh + 1) * DIL_HD] for kvh in range(DIL_KVH)]
        vc = [c_ref[:, kvw + kvh * DIL_HD:kvw + (kvh + 1) * DIL_HD] for kvh in range(DIL_KVH)]
        knew = [kn_all[gi * DIL_KVH + kvh:gi * DIL_KVH + kvh + 1] for kvh in range(DIL_KVH)]
        vnew = [vn_all[gi * DIL_KVH + kvh:gi * DIL_KVH + kvh + 1] for kvh in range(DIL_KVH)]
        o, lse = _dec_attend(q, kc, vc, knew, vnew, DIL_G, DIL_HD ** -0.5, bc_ref[gi], bn_ref[gi])
        outs.append(o)
        lses.append(lse)
    m = jnp.maximum(jnp.maximum(lses[0], lses[1]), lses[2])
    ws = [jnp.exp(l - m) for l in lses]
    inv = 1.0 / (ws[0] + ws[1] + ws[2])
    o = (ws[0] * outs[0] + ws[1] * outs[1] + ws[2] * outs[2]) * inv
    y_ref[0] = o * _silu(g_ref[0])


def _dil_dec(q3, kn3, vn3, g3, views, layer, bias_c, bias_n):
    n = q3.shape[0]
    cspecs = [pl.BlockSpec((None, None) + v.shape[2:], lambda i: (layer, i, 0, 0)) for v in views]
    return pl.pallas_call(
        _dil_dec_kernel,
        out_shape=jax.ShapeDtypeStruct((n, DIL_QH, DIL_HD), F32),
        grid=(n,),
        in_specs=[pl.BlockSpec((1, N_DIL * DIL_QH, DIL_HD), lambda i: (i, 0, 0)),
                  pl.BlockSpec((1, N_DIL * DIL_KVH, DIL_HD), lambda i: (i, 0, 0)),
                  pl.BlockSpec((1, N_DIL * DIL_KVH, DIL_HD), lambda i: (i, 0, 0)),
                  pl.BlockSpec((1, DIL_QH, DIL_HD), lambda i: (i, 0, 0))] + cspecs +
                 [pl.BlockSpec(bias_c.shape, lambda i: (0, 0, 0)),
                  pl.BlockSpec(bias_n.shape, lambda i: (0, 0, 0))],
        out_specs=pl.BlockSpec((1, DIL_QH, DIL_HD), lambda i: (i, 0, 0)),
        compiler_params=_cparams(("parallel",)),
        name="dil_dec",
    )(q3, kn3, vn3, g3, *views, bias_c, bias_n)


def _swa_dec_kernel(q_ref, kn_ref, vn_ref, g_ref, c_ref, sink_ref, cos_ref, sin_ref, perm_ref, y_ref, rk_ref):
    cos = cos_ref[...]
    sin = sin_ref[...]
    perm = perm_ref[...]
    q = q_ref[0]
    kn = kn_ref[0]
    q = q * cos + _dot_hi(q, perm) * sin
    kn = kn * cos + _dot_hi(kn, perm) * sin
    rk_ref[0] = kn
    vn = vn_ref[0]
    kvw = SWA_KVH * SWA_HD
    kc = [c_ref[:, kvh * SWA_HD:(kvh + 1) * SWA_HD] for kvh in range(SWA_KVH)]
    vc = [c_ref[:, kvw + kvh * SWA_HD:kvw + (kvh + 1) * SWA_HD] for kvh in range(SWA_KVH)]
    knew = [kn[kvh:kvh + 1] for kvh in range(SWA_KVH)]
    vnew = [vn[kvh:kvh + 1] for kvh in range(SWA_KVH)]
    o, _ = _dec_attend(q, kc, vc, knew, vnew, SWA_G, SWA_HD ** -0.5, sink=sink_ref[...])
    y_ref[0] = o * _silu(g_ref[0])


def _swa_dec(q3, kn3, vn3, g3, view, layer, sink_col, cos_d, sin_d, perm):
    n = q3.shape[0]
    win, kvw2 = view.shape[2:]
    return pl.pallas_call(
        _swa_dec_kernel,
        out_shape=(jax.ShapeDtypeStruct((n, SWA_QH, SWA_HD), F32),
                   jax.ShapeDtypeStruct((n, SWA_KVH, SWA_HD), F32)),
        grid=(n,),
        in_specs=[pl.BlockSpec((1, SWA_QH, SWA_HD), lambda i: (i, 0, 0)),
                  pl.BlockSpec((1, SWA_KVH, SWA_HD), lambda i: (i, 0, 0)),
                  pl.BlockSpec((1, SWA_KVH, SWA_HD), lambda i: (i, 0, 0)),
                  pl.BlockSpec((1, SWA_QH, SWA_HD), lambda i: (i, 0, 0)),
                  pl.BlockSpec((None, None, win, kvw2), lambda i: (layer, i, 0, 0)),
                  pl.BlockSpec(sink_col.shape, lambda i: (0, 0)),
                  pl.BlockSpec(cos_d.shape, lambda i: (0, 0)),
                  pl.BlockSpec(sin_d.shape, lambda i: (0, 0)),
                  pl.BlockSpec(perm.shape, lambda i: (0, 0))],
        out_specs=(pl.BlockSpec((1, SWA_QH, SWA_HD), lambda i: (i, 0, 0)),
                   pl.BlockSpec((1, SWA_KVH, SWA_HD), lambda i: (i, 0, 0))),
        compiler_params=_cparams(("parallel",)),
        name="swa_dec",
    )(q3, kn3, vn3, g3, view, sink_col, cos_d, sin_d, perm)


def _lru_dec_kernel(x_ref, g_ref, b0_ref, b1_ref, b2_ref, h0_ref, cw_ref, cb_ref, wa_ref, wx_ref,
                    ba_ref, bx_ref, lam_ref, y_ref, h_ref):
    cw = cw_ref[...]
    cx = (b0_ref[...] * cw[0:1] + b1_ref[...] * cw[1:2] + b2_ref[...] * cw[2:3] + x_ref[...] * cw[3:4]
          + cb_ref[...])
    a, bterm = _lru_gates(cx, wa_ref, wx_ref, ba_ref[...], bx_ref[...], lam_ref[...])
    h = a * h0_ref[...] + bterm
    h_ref[...] = h
    y_ref[...] = h * _silu(g_ref[...])


def _lru_dec(x, g, b0, b1, b2, h0, conv_w, conv_b, wa, wx, ba, bx, lam):
    n = x.shape[0]
    ins = (x, g, b0, b1, b2, h0, conv_w, conv_b, wa, wx, ba, bx, lam)
    full = lambda a: pl.BlockSpec(a.shape, lambda i: (0,) * a.ndim)
    return pl.pallas_call(
        _lru_dec_kernel,
        out_shape=(jax.ShapeDtypeStruct((n, LRU_W), F32), jax.ShapeDtypeStruct((n, LRU_W), F32)),
        grid=(1,),
        in_specs=[full(a) for a in ins],
        out_specs=(pl.BlockSpec((n, LRU_W), lambda i: (0, 0)), pl.BlockSpec((n, LRU_W), lambda i: (0, 0))),
        compiler_params=_cparams(("arbitrary",)),
        name="lru_dec",
    )(*ins)


def _rel_bucket(dist):
    max_exact = REL_BUCKETS // 2
    n = dist.astype(F32)
    large = max_exact + (jnp.log(jnp.maximum(n, 1.0) / max_exact) / math.log(REL_MAX_DIST / max_exact)
                         * (REL_BUCKETS - max_exact)).astype(jnp.int32)
    large = jnp.minimum(large, REL_BUCKETS - 1)
    return jnp.where(dist < max_exact, dist, large)


def _offset_bias(rel_bias, gi, win, dil):
    j = win // dil + 1
    b = rel_bias[_rel_bucket(dil * jnp.arange(j, dtype=jnp.int32))]
    return b[:, gi * DIL_QH:(gi + 1) * DIL_QH].astype(F32)


def _rope_tables(pos):
    half = SWA_HD // 2
    inv = ROPE_THETA ** (-jnp.arange(half, dtype=F32) / half)
    ang = pos.astype(F32)[:, None] * inv[None, :]
    c, s = jnp.cos(ang), jnp.sin(ang)
    return jnp.concatenate([c, c], axis=1), jnp.concatenate([-s, s], axis=1)


def _shift_copies(c_ref, n_ref, o_ref, sem):
    d, _, w = c_ref.shape[:3]
    copies = []
    for a in range(d):
        copies.append(pltpu.make_async_copy(c_ref.at[a, :, pl.ds(1, w - 1)], o_ref.at[a, :, pl.ds(0, w - 1)],
                                            sem.at[2 * a]))
        copies.append(pltpu.make_async_copy(n_ref.at[a], o_ref.at[a, :, pl.ds(w - 1, 1)], sem.at[2 * a + 1]))
    return copies


def _shift_kernel(c_ref, n_ref, o_ref, sem):
    copies = _shift_copies(c_ref, n_ref, o_ref, sem)
    for cp in copies:
        cp.start()
    for cp in copies:
        cp.wait()


def _shift_append(cache, new_rows):
    anyspec = pl.BlockSpec(memory_space=pl.ANY)
    return pl.pallas_call(
        _shift_kernel,
        out_shape=jax.ShapeDtypeStruct(cache.shape, cache.dtype),
        in_specs=[anyspec, anyspec],
        out_specs=anyspec,
        scratch_shapes=[pltpu.SemaphoreType.DMA((2 * cache.shape[0],))],
        name="shift_append",
    )(cache, new_rows)


def _reorder_w_in(w):
    pad = jnp.zeros((w.shape[0], 2 * LANE - 2 * GDN_HEADS), w.dtype)
    parts = [w[:, O_MG:O_END], w[:, O_AZ:O_AB], w[:, O_AQKV:O_AZ], w[:, O_BQ:O_BK], w[:, O_BG:O_CX],
             w[:, O_CX:O_CG], w[:, O_CG:O_DQ], w[:, O_DQ:O_DK], w[:, O_DG:O_MG], w[:, O_BK:O_BV],
             w[:, O_BV:O_BG], w[:, O_DK:O_DV], w[:, O_DV:O_DG], w[:, O_AB:O_BQ], pad]
    return jnp.concatenate(parts, axis=1).astype(BF16)


def _cols(h, unit, width):
    return h[..., unit * LANE:unit * LANE + width]


def kernel(x_prompt, x_sample, state_gdn, state_gdn_conv, cache_dil_w128, cache_dil_w512, cache_dil_w2048,
           cache_swa, state_rglru, state_rglru_conv, w_in, gdn_conv_w, gdn_a_log, gdn_dt_bias, gdn_norm_w,
           lru_conv_w, lru_conv_b, lru_wa, lru_ba, lru_wx, lru_bx, lru_lambda, swa_sink, rel_bias, w_branch,
           w_out, ln_g, ln_b):
    nb, t_len, _ = x_prompt.shape
    ns = x_sample.shape[0]
    mp = nb * t_len
    caches = (cache_dil_w128, cache_dil_w512, cache_dil_w2048)
    xp = x_prompt.reshape(mp, D_MODEL)
    xs = x_sample.reshape(ns, D_MODEL)

    cos64, sin64 = _rope_tables(jnp.arange(t_len))
    cos_t = jnp.concatenate([cos64, cos64], axis=1)
    sin_t = jnp.concatenate([sin64, sin64], axis=1)
    cos_d, sin_d = _rope_tables(jnp.full((1,), PAST_LEN))
    perm = jnp.asarray(np.roll(np.eye(SWA_HD, dtype=np.float32), SWA_HD // 2, axis=0))
    qi = jnp.arange(BLOCK)[:, None]
    kj = jnp.arange(2 * BLOCK)[None, :]
    bias_p, bias_c, bias_n = [], [], []
    for gi, (win, dil) in enumerate(DIL_GROUPS):
        ob = _offset_bias(rel_bias, gi, win, dil)
        jw = win // dil
        onehot = (jnp.clip(qi + BLOCK - kj, 0, jw)[:, :, None] == jnp.arange(jw + 1)).astype(F32)
        bias_p.append(jnp.einsum('qkj,jh->hqk', onehot, ob, precision=lax.Precision.HIGHEST))
        bias_c.append(jnp.transpose(ob[::-1][:jw], (1, 0)))
        bias_n.append(ob[0][:, None])
    bias_c = jnp.stack(bias_c)
    bias_n = jnp.stack(bias_n)
    dil_views = [c[:, :, ::dil].reshape(DEPTH, ns, win // dil, 2 * DIL_KVH * DIL_HD)
                 for (win, dil), c in zip(DIL_GROUPS, caches)]
    swa_view = cache_swa.reshape(DEPTH, ns, cache_swa.shape[2], 2 * SWA_KVH * SWA_HD)

    new_p = [[] for _ in range(8)]
    new_s = [[] for _ in range(8)]
    for l in range(DEPTH):
        w_r = _reorder_w_in(w_in[l])
        wb = w_branch[l]
        wo = w_out[l]
        wa = lru_wa[l].astype(BF16)
        wx = lru_wx[l].astype(BF16)
        zpad = jnp.zeros((1, LANE - 2 * GDN_HEADS), F32)
        arow = jnp.concatenate([jnp.zeros((1, GDN_HEADS), F32), jnp.exp(gdn_a_log[l])[None], zpad], axis=1)
        dtb = jnp.concatenate([jnp.zeros((1, GDN_HEADS), F32), gdn_dt_bias[l][None], zpad], axis=1)
        norm_w = gdn_norm_w[l][None]
        conv_b = lru_conv_b[l][None]
        ba, bx, lam = lru_ba[l][None], lru_bx[l][None], lru_lambda[l][None]
        lng, lnb = ln_g[l][None], ln_b[l][None]
        sink = swa_sink[l].astype(F32)

        hp = _matmul(xp.astype(BF16), w_r, 2048, 1024)
        hp3 = hp.reshape(nb, t_len, D_INP)
        ya, s_p = _gdn_prompt(hp, nb, t_len, gdn_conv_w[l], arow, dtb, norm_w)
        outs, lses = [], []
        for gi in range(N_DIL):
            o_g, l_g = _dil_attn(hp, gi, nb, t_len, bias_p[gi])
            outs.append(o_g)
            lses.append(l_g)
        yb = _dil_merge(outs[0], outs[1], outs[2], lses[0], lses[1], lses[2], hp, 512)
        yc, h_p = _lru_prompt(hp, nb, t_len, lru_conv_w[l], conv_b, wa, wx, ba, bx, lam, 512)
        rq, rk = _rope_prompt(hp, cos_t, sin_t, t_len, 512)
        sink_row = jnp.concatenate([sink[None], jnp.zeros((1, LANE - SWA_QH), F32)], axis=1)
        yd = _swa_attn(rq, rk, hp, sink_row, nb, t_len)
        merged = _branch_merge((ya, yb, yc, yd), hp, wb, 512, 512)
        xp_new = _out_proj(merged, xp, wo, lng, lnb, 256)

        new_p[0].append(s_p)
        new_p[1].append(_cols(hp3, U_AQKV, GDN_QKV)[:, t_len - (CONV_W - 1):])
        for gi, (win, dil) in enumerate(DIL_GROUPS):
            kk = _cols(hp3, U_BK + gi * DIL_KVH, DIL_KVH * DIL_HD)[:, t_len - win:]
            vv = _cols(hp3, U_BV + gi * DIL_KVH, DIL_KVH * DIL_HD)[:, t_len - win:]
            new_p[2 + gi].append(jnp.stack([kk, vv], axis=2).reshape(nb, win, 2, DIL_KVH, DIL_HD))
        kk = rk.reshape(nb, t_len, LANE)[:, t_len - SWA_WINDOW:]
        vv = _cols(hp3, U_DV, LANE)[:, t_len - SWA_WINDOW:]
        new_p[5].append(jnp.stack([kk, vv], axis=2).reshape(nb, SWA_WINDOW, 2, SWA_KVH, SWA_HD))
        new_p[6].append(h_p.reshape(nb, LRU_W))
        new_p[7].append(_cols(hp3, U_CX, LRU_W)[:, t_len - (CONV_W - 1):])

        hs = _matmul(xs.astype(BF16), w_r, ns, 1024)
        gbuf = state_gdn_conv[l]
        a_qkv_s = _cols(hs, U_AQKV, GDN_QKV)
        qkv_n, beta_s, eg_s = _gdn_dec_pre(a_qkv_s, gbuf[:, 0], gbuf[:, 1], gbuf[:, 2], gdn_conv_w[l],
                                           _cols(hs, U_AB, LANE), arow, dtb)
        qkv4 = qkv_n.reshape(ns, 3, GDN_HEADS, GDN_DK)
        s_s, ya_s = _gdn_dec(jnp.swapaxes(qkv4[:, 0], 1, 2), jnp.swapaxes(qkv4[:, 1], 1, 2), qkv4[:, 2],
                             _cols(hs, U_AZ, BRANCH_W).reshape(ns, GDN_HEADS, GDN_DV),
                             beta_s[:, None], eg_s[:, None], norm_w, state_gdn, l)
        bk_s = _cols(hs, U_BK, N_DIL * DIL_KVH * DIL_HD)
        bv_s = _cols(hs, U_BV, N_DIL * DIL_KVH * DIL_HD)
        yb_s = _dil_dec(_cols(hs, U_BQ, N_DIL * BRANCH_W).reshape(ns, N_DIL * DIL_QH, DIL_HD),
                        bk_s.reshape(ns, N_DIL * DIL_KVH, DIL_HD), bv_s.reshape(ns, N_DIL * DIL_KVH, DIL_HD),
                        _cols(hs, U_BG, BRANCH_W).reshape(ns, DIL_QH, DIL_HD), dil_views, l, bias_c, bias_n)
        lbuf = state_rglru_conv[l]
        cx_s = _cols(hs, U_CX, LRU_W)
        yc_s, h_s = _lru_dec(cx_s, _cols(hs, U_CG, LRU_W), lbuf[:, 0], lbuf[:, 1], lbuf[:, 2], state_rglru[l],
                             lru_conv_w[l], conv_b, wa, wx, ba, bx, lam)
        dv_s = _cols(hs, U_DV, LANE)
        yd_s, rk_s = _swa_dec(_cols(hs, U_DQ, BRANCH_W).reshape(ns, SWA_QH, SWA_HD),
                              _cols(hs, U_DK, LANE).reshape(ns, SWA_KVH, SWA_HD),
                              dv_s.reshape(ns, SWA_KVH, SWA_HD),
                              _cols(hs, U_DG, BRANCH_W).reshape(ns, SWA_QH, SWA_HD),
                              swa_view, l, sink.reshape(SWA_QH, 1), cos_d, sin_d, perm)
        ys_s = (ya_s.reshape(ns, BRANCH_W).astype(BF16), yb_s.reshape(ns, BRANCH_W).astype(BF16),
                yc_s.astype(BF16), yd_s.reshape(ns, BRANCH_W).astype(BF16))
        merged_s = _branch_merge(ys_s, hs, wb, ns, 512)
        xs_new = _out_proj(merged_s, xs, wo, lng, lnb, ns)

        new_s[0].append(s_s)
        new_s[1].append(jnp.concatenate([gbuf[:, 1:], a_qkv_s[:, None]], axis=1))
        for gi in range(N_DIL):
            kk = bk_s[:, gi * DIL_KVH * DIL_HD:(gi + 1) * DIL_KVH * DIL_HD].reshape(ns, 1, DIL_KVH, DIL_HD)
            vv = bv_s[:, gi * DIL_KVH * DIL_HD:(gi + 1) * DIL_KVH * DIL_HD].reshape(ns, 1, DIL_KVH, DIL_HD)
            new_s[2 + gi].append(jnp.stack([kk, vv], axis=2))
        new_s[5].append(jnp.stack([rk_s[:, None], dv_s.reshape(ns, 1, SWA_KVH, SWA_HD)], axis=2))
        new_s[6].append(h_s)
        new_s[7].append(jnp.concatenate([lbuf[:, 1:], cx_s[:, None]], axis=1))

        xp, xs = xp_new, xs_new

    p = [jnp.stack(v) for v in new_p]
    s = [jnp.stack(new_s[i]) for i in (0, 1)]
    for i, c in zip((2, 3, 4, 5), caches + (cache_swa,)):
        s.append(_shift_append(c, jnp.stack(new_s[i])))
    s += [jnp.stack(new_s[i]) for i in (6, 7)]
    return (xp.reshape(nb, t_len, D_MODEL), xs.reshape(ns, 1, D_MODEL),
            p[0], s[0], p[1], s[1], p[2], s[2], p[3], s[3], p[4], s[4], p[5], s[5], p[6], s[6], p[7], s[7])
```

```python
import functools
import math

import jax
import jax.numpy as jnp
import numpy as np
from jax import lax
from jax.experimental import pallas as pl
from jax.experimental.pallas import tpu as pltpu

F32 = jnp.float32
BF16 = jnp.bfloat16
NEG_INF = -1e30

D_MODEL = 2048
DEPTH = 2
PAST_LEN = 16384
BRANCH_W = D_MODEL // 2
N_BRANCH = 4
CONV_W = 4
BLOCK = 128
GDN_DK = 128
GDN_DV = 128
GDN_HEADS = BRANCH_W // GDN_DV
GDN_QKV = GDN_HEADS * (2 * GDN_DK + GDN_DV)
DIL_GROUPS = ((128, 1), (512, 4), (2048, 16))
N_DIL = len(DIL_GROUPS)
DIL_HD = 128
DIL_QH = BRANCH_W // DIL_HD
DIL_KVH = 2
DIL_G = DIL_QH // DIL_KVH
LRU_W = BRANCH_W
LRU_BLOCKS = 8
LRU_BS = LRU_W // LRU_BLOCKS
LRU_C = 8.0
SWA_HD = 64
SWA_QH = BRANCH_W // SWA_HD
SWA_KVH = 2
SWA_G = SWA_QH // SWA_KVH
SWA_WINDOW = 128
ROPE_THETA = 150000.0
REL_BUCKETS = 32
REL_MAX_DIST = 2048
LN_EPS = 1e-5
RMS_EPS = 1e-6
DN_ALPHA = (2.0 * DEPTH) ** 0.25

IN_SIZES = (GDN_QKV, GDN_HEADS * GDN_DV, GDN_HEADS, GDN_HEADS,
            N_DIL * DIL_QH * DIL_HD, N_DIL * DIL_KVH * DIL_HD, N_DIL * DIL_KVH * DIL_HD, DIL_QH * DIL_HD,
            LRU_W, LRU_W,
            SWA_QH * SWA_HD, SWA_KVH * SWA_HD, SWA_KVH * SWA_HD, SWA_QH * SWA_HD,
            N_BRANCH * D_MODEL)
_OFF = [0]
for _s in IN_SIZES:
    _OFF.append(_OFF[-1] + _s)
(O_AQKV, O_AZ, O_AB, O_AA, O_BQ, O_BK, O_BV, O_BG, O_CX, O_CG, O_DQ, O_DK, O_DV, O_DG, O_MG, O_END) = _OFF

LANE = 128
U_MG, U_AZ, U_AQKV, U_BQ, U_BG, U_CX, U_CG, U_DQ, U_DG, U_BK, U_BV, U_DK, U_DV, U_AB = (
    0, 64, 72, 96, 120, 128, 136, 144, 152, 160, 166, 172, 173, 174)
N_UNITS = 176
D_INP = N_UNITS * LANE
GDN_CHUNK = 128
VMEM_LIMIT = 56 * 1024 * 1024


def _cparams(sem):
    return pltpu.CompilerParams(dimension_semantics=sem, vmem_limit_bytes=VMEM_LIMIT)


def _sigmoid(x):
    return 1.0 / (1.0 + jnp.exp(-x))


def _silu(x):
    return x * _sigmoid(x)


def _softplus(x):
    return jnp.maximum(x, 0.0) + jnp.log1p(jnp.exp(-jnp.abs(x)))


def _dot(a, b):
    return jnp.dot(a.astype(BF16), b.astype(BF16), preferred_element_type=F32)


def _dot_nt(a, b):
    return lax.dot_general(a.astype(BF16), b.astype(BF16), (((1,), (1,)), ((), ())),
                           preferred_element_type=F32)


def _dot_hi(a, b):
    return jnp.dot(a, b, preferred_element_type=F32, precision=lax.Precision.HIGHEST)


def _split2(a):
    hi = a.astype(BF16)
    return hi, (a - hi.astype(F32)).astype(BF16)


def _dot3(a, b):
    ah, al = _split2(a)
    bh, bl = _split2(b)
    return (jnp.dot(ah, bh, preferred_element_type=F32) + jnp.dot(ah, bl, preferred_element_type=F32)
            + jnp.dot(al, bh, preferred_element_type=F32))


def _dot_exact_lhs(a, b):
    a = a.astype(BF16)
    b0 = b.astype(BF16)
    r1 = b - b0.astype(F32)
    b1 = r1.astype(BF16)
    b2 = (r1 - b1.astype(F32)).astype(BF16)
    return (jnp.dot(a, b0, preferred_element_type=F32) + jnp.dot(a, b1, preferred_element_type=F32)
            + jnp.dot(a, b2, preferred_element_type=F32))


def _dot_nt_hi(a, b):
    return lax.dot_general(a, b, (((1,), (1,)), ((), ())), preferred_element_type=F32,
                           precision=lax.Precision.HIGHEST)


def _matmul_kernel(x_ref, w_ref, o_ref):
    o_ref[...] = jnp.dot(x_ref[...], w_ref[...], preferred_element_type=F32)


def _matmul(x, w, tm, tn):
    m, k = x.shape
    n = w.shape[1]
    return pl.pallas_call(
        _matmul_kernel,
        out_shape=jax.ShapeDtypeStruct((m, n), F32),
        grid=(m // tm, n // tn),
        in_specs=[pl.BlockSpec((tm, k), lambda i, j: (i, 0)),
                  pl.BlockSpec((k, tn), lambda i, j: (0, j))],
        out_specs=pl.BlockSpec((tm, tn), lambda i, j: (i, j)),
        compiler_params=_cparams(("parallel", "arbitrary")),
        name="in_proj",
    )(x, w)


def _conv_rows(x, prev8, cw):
    row8 = lax.broadcasted_iota(jnp.int32, (8, x.shape[1]), 0)
    y = x * cw[CONV_W - 1:CONV_W]
    for k in range(1, CONV_W):
        xk = pltpu.roll(x, k, 0)
        fk = pltpu.roll(prev8, k, 0)
        head = jnp.where(row8 < k, fk, xk[0:8])
        xk = jnp.concatenate([head, xk[8:]], axis=0)
        y = y + xk * cw[CONV_W - 1 - k:CONV_W - k]
    return y


def _tri_inv(lmat, ri, ci):
    c = lmat[0].shape[0]
    eye = (ri == ci).astype(F32)
    blk = (ri >> 4) == (ci >> 4)
    d = [jnp.where(blk, l, 0.0) for l in lmat]
    x = [eye - di for di in d]
    p = [_dot(di, di) for di in d]
    x = [xi + _dot(xi, pi) for xi, pi in zip(x, p)]
    p = [_dot(pi, pi) for pi in p]
    x = [xi + _dot(xi, pi) for xi, pi in zip(x, p)]
    p = [_dot(pi, pi) for pi in p]
    x = [xi + _dot(xi, pi) for xi, pi in zip(x, p)]
    sh = 4
    while (1 << sh) < c:
        lower_left = ((ri >> (sh + 1)) == (ci >> (sh + 1))) & (((ri >> sh) & 1) == 1) & (((ci >> sh) & 1) == 0)
        t = [_dot(jnp.where(lower_left, l, 0.0), xi) for l, xi in zip(lmat, x)]
        x = [xi - _dot(xi, ti) for xi, ti in zip(x, t)]
        sh += 1
    return x


def _tri_solve(lmat, rhs, ri, ci):
    tinv = _tri_inv(lmat, ri, ci)
    sol = [_dot(t, r) for t, r in zip(tinv, rhs)]
    resid = [r - s - _dot3(l, s) for r, s, l in zip(rhs, sol, lmat)]
    return [s + _dot(t, r) for s, t, r in zip(sol, tinv, resid)]


def _gdn_kernel(qkv_ref, z_ref, ab_ref, cw_ref, arow_ref, dtb_ref, nw_ref, y_ref, s_ref, prev_ref):
    cidx = pl.program_id(1)
    C = GDN_CHUNK

    @pl.when(cidx == 0)
    def _():
        s_ref[...] = jnp.zeros_like(s_ref)
        prev_ref[...] = jnp.zeros_like(prev_ref)

    x = qkv_ref[...]
    act = _silu(_conv_rows(x, prev_ref[...], cw_ref[...]))
    prev_ref[...] = x[C - 8:C]

    ab = ab_ref[...]
    beta_t = _sigmoid(ab)
    g_t = -arow_ref[...] * _softplus(ab + dtb_ref[...])
    ri = lax.broadcasted_iota(jnp.int32, (C, C), 0)
    ci = lax.broadcasted_iota(jnp.int32, (C, C), 1)
    incl = ri >= ci
    strict = ri > ci
    gc_t = _dot_exact_lhs(incl.astype(F32), g_t)
    gc_tt = gc_t.T
    egc_t = jnp.exp(gc_t)
    nw = nw_ref[...]
    z = z_ref[...]

    heads = range(GDN_HEADS)
    q, k, v, beta, gcc, egc, decay, glc = [], [], [], [], [], [], [], []
    for h in heads:
        qh = act[:, h * GDN_DK:(h + 1) * GDN_DK]
        kh = act[:, (GDN_HEADS + h) * GDN_DK:(GDN_HEADS + h + 1) * GDN_DK]
        q.append(qh * lax.rsqrt(jnp.sum(qh * qh, axis=-1, keepdims=True) + 1e-6) * (GDN_DK ** -0.5))
        k.append(kh * lax.rsqrt(jnp.sum(kh * kh, axis=-1, keepdims=True) + 1e-6))
        v.append(act[:, (2 * GDN_HEADS + h) * GDN_DK:(2 * GDN_HEADS + h + 1) * GDN_DK])
        beta.append(beta_t[:, h:h + 1])
        gcc.append(gc_t[:, GDN_HEADS + h:GDN_HEADS + h + 1])
        gcr = gc_tt[GDN_HEADS + h:GDN_HEADS + h + 1, :]
        egc.append(egc_t[:, GDN_HEADS + h:GDN_HEADS + h + 1])
        glc.append(gc_t[C - 1:C, GDN_HEADS + h:GDN_HEADS + h + 1])
        decay.append(jnp.where(incl, jnp.exp(jnp.where(incl, gcc[h] - gcr, 0.0)), 0.0))
    kb = [k[h] * beta[h] for h in heads]
    lmat = [jnp.where(strict, _dot_nt(kb[h], k[h]) * decay[h], 0.0) for h in heads]
    rhs = [jnp.concatenate([v[h] * beta[h], kb[h] * egc[h]], axis=1) for h in heads]
    sol = _tri_solve(lmat, rhs, ri, ci)
    qk = [_dot_nt(q[h], k[h]) * decay[h] for h in heads]
    s = [s_ref[0, h] for h in heads]
    v_new = [sol[h][:, :GDN_DV] - _dot(sol[h][:, GDN_DV:], s[h]) for h in heads]
    o = [_dot(q[h] * egc[h], s[h]) + _dot(qk[h], v_new[h]) for h in heads]
    for h in heads:
        kd = k[h] * jnp.exp(glc[h] - gcc[h])
        s_ref[0, h] = s[h] * jnp.exp(glc[h]) + _dot(kd.T, v_new[h])
    for h in heads:
        y = o[h] * lax.rsqrt(jnp.mean(o[h] * o[h], axis=-1, keepdims=True) + RMS_EPS) * nw
        y = y * _silu(z[:, h * GDN_DV:(h + 1) * GDN_DV])
        y_ref[:, h * GDN_DV:(h + 1) * GDN_DV] = y.astype(BF16)


def _gdn_prompt(hp, n_seq, t_len, conv_w, arow, dtb, norm_w):
    C = GDN_CHUNK
    nc = t_len // C
    return pl.pallas_call(
        _gdn_kernel,
        out_shape=(jax.ShapeDtypeStruct((n_seq * t_len, BRANCH_W), BF16),
                   jax.ShapeDtypeStruct((n_seq, GDN_HEADS, GDN_DK, GDN_DV), F32)),
        grid=(n_seq, nc),
        in_specs=[pl.BlockSpec((C, GDN_QKV), lambda n, c: (n * nc + c, U_AQKV * LANE // GDN_QKV)),
                  pl.BlockSpec((C, BRANCH_W), lambda n, c: (n * nc + c, U_AZ * LANE // BRANCH_W)),
                  pl.BlockSpec((C, LANE), lambda n, c: (n * nc + c, U_AB)),
                  pl.BlockSpec((CONV_W, GDN_QKV), lambda n, c: (0, 0)),
                  pl.BlockSpec((1, LANE), lambda n, c: (0, 0)),
                  pl.BlockSpec((1, LANE), lambda n, c: (0, 0)),
                  pl.BlockSpec((1, GDN_DV), lambda n, c: (0, 0))],
        out_specs=(pl.BlockSpec((C, BRANCH_W), lambda n, c: (n * nc + c, 0)),
                   pl.BlockSpec((1, GDN_HEADS, GDN_DK, GDN_DV), lambda n, c: (n, 0, 0, 0))),
        scratch_shapes=[pltpu.VMEM((8, GDN_QKV), F32)],
        compiler_params=_cparams(("parallel", "arbitrary")),
        name="gdn_prompt",
    )(hp, hp, hp, conv_w, arow, dtb, norm_w)


def _band_mask(rows, cols, window, first_block):
    qi = lax.broadcasted_iota(jnp.int32, (rows, cols), 0) & (BLOCK - 1)
    kj = lax.broadcasted_iota(jnp.int32, (rows, cols), 1) & (2 * BLOCK - 1)
    off = qi + BLOCK - kj
    kmin = jnp.where(first_block, BLOCK, 0)
    return (off >= 0) & (off <= window) & (kj >= kmin)


def _dil_attn_kernel(*refs, window, dil, n_kv):
    hd, n_g = DIL_HD, DIL_G
    nq = n_kv * n_g
    q_refs = refs[:nq]
    kv_refs = refs[nq:nq + 4 * n_kv]
    bias_ref, o_ref, lse_ref = refs[nq + 4 * n_kv:nq + 4 * n_kv + 3]
    o_scr = refs[nq + 4 * n_kv + 3:]
    rows = n_g * BLOCK
    valid = _band_mask(rows, 2 * BLOCK, window, pl.program_id(1) == 0)
    lane = lax.broadcasted_iota(jnp.int32, (BLOCK, LANE), 1)

    def sub(r):
        return pl.ds(r, BLOCK, stride=dil) if dil > 1 else slice(None)

    chains = [(j, r) for r in range(dil) for j in range(n_kv)]
    for c0 in range(0, len(chains), 2):
        grp = chains[c0:c0 + 2]
        s = []
        for j, r in grp:
            kc_ref, kp_ref = kv_refs[4 * j], kv_refs[4 * j + 1]
            qh = jnp.concatenate([q_refs[j * n_g + g][sub(r), :] for g in range(n_g)], axis=0)
            kcat = jnp.concatenate([kp_ref[sub(r), :], kc_ref[sub(r), :]], axis=0)
            bias = bias_ref[j * n_g:(j + 1) * n_g].reshape(rows, 2 * BLOCK)
            s.append(jnp.where(valid, _dot_nt(qh, kcat) * (hd ** -0.5) + bias, NEG_INF))
        m = [jnp.max(sc, axis=-1, keepdims=True) for sc in s]
        p = [jnp.exp(sc - mc) for sc, mc in zip(s, m)]
        den = [jnp.sum(pc, axis=-1, keepdims=True) for pc in p]
        for (j, r), pc, mc, dc in zip(grp, p, m, den):
            vc_ref, vp_ref = kv_refs[4 * j + 2], kv_refs[4 * j + 3]
            vcat = jnp.concatenate([vp_ref[sub(r), :], vc_ref[sub(r), :]], axis=0)
            o = _dot(pc, vcat) / dc
            lse = mc + jnp.log(dc)
            lse_t = jnp.zeros((BLOCK, LANE), F32)
            for g in range(n_g):
                o_scr[j * n_g + g][sub(r), :] = o[g * BLOCK:(g + 1) * BLOCK]
                lse_t = jnp.where(lane == g, lse[g * BLOCK:(g + 1) * BLOCK], lse_t)
            lse_ref[sub(r), j * LANE:(j + 1) * LANE] = lse_t
    for q in range(nq):
        o_ref[:, q * hd:(q + 1) * hd] = o_scr[q][...]


def _swa_attn_kernel(q_ref, kc_ref, kp_ref, vc_ref, vp_ref, sink_ref, gate_ref, y_ref):
    half = SWA_HD
    n_slab = SWA_QH // 2
    slab_per_kvh = n_slab // SWA_KVH
    k2 = jnp.concatenate([kp_ref[...], kc_ref[...]], axis=0)
    v2 = jnp.concatenate([vp_ref[...], vc_ref[...]], axis=0)
    lane = lax.broadcasted_iota(jnp.int32, k2.shape, 1)
    lo = lane < half
    k2r = pltpu.roll(k2, half, 1)
    v2r = pltpu.roll(v2, half, 1)
    kk = [jnp.concatenate([jnp.where(lo, k2, 0.0), jnp.where(lo, 0.0, k2r)], axis=0),
          jnp.concatenate([jnp.where(lo, k2r, 0.0), jnp.where(lo, 0.0, k2)], axis=0)]
    vv = [jnp.concatenate([jnp.where(lo, v2, 0.0), jnp.where(lo, 0.0, v2r)], axis=0),
          jnp.concatenate([jnp.where(lo, v2r, 0.0), jnp.where(lo, 0.0, v2)], axis=0)]
    kk = [a.astype(BF16) for a in kk]
    vv = [a.astype(BF16) for a in vv]
    valid = _band_mask(BLOCK, 4 * BLOCK, SWA_WINDOW, pl.program_id(1) == 0)
    sink = sink_ref[...]
    lane_o = lax.broadcasted_iota(jnp.int32, (BLOCK, LANE), 1) < half
    slabs = range(n_slab)
    s = []
    for a in slabs:
        sa = lax.dot_general(q_ref[:, a * LANE:(a + 1) * LANE], kk[a // slab_per_kvh],
                             (((1,), (1,)), ((), ())), preferred_element_type=F32) * (SWA_HD ** -0.5)
        s.append(jnp.where(valid, sa, NEG_INF))
    heads = range(SWA_QH)
    sh = [s[h // 2][:, (h % 2) * 2 * BLOCK:(h % 2 + 1) * 2 * BLOCK] for h in heads]
    snk = [sink[:, h:h + 1] for h in heads]
    m = [jnp.maximum(jnp.max(sh[h], axis=-1, keepdims=True), snk[h]) for h in heads]
    pe = [jnp.exp(sh[h] - m[h]) for h in heads]
    dh = [jnp.sum(pe[h], axis=-1, keepdims=True) + jnp.exp(snk[h] - m[h]) for h in heads]
    for a in slabs:
        p = jnp.concatenate([pe[2 * a], pe[2 * a + 1]], axis=1).astype(BF16)
        den = jnp.where(lane_o, dh[2 * a], dh[2 * a + 1])
        o = jnp.dot(p, vv[a // slab_per_kvh], preferred_element_type=F32) / den
        y_ref[:, a * LANE:(a + 1) * LANE] = (o * _silu(gate_ref[:, a * LANE:(a + 1) * LANE])).astype(BF16)


def _row_specs(rows, nblk, width, cb):
    cur = pl.BlockSpec((rows, width), lambda n, i, *_: (n * nblk + i, cb))
    prev = pl.BlockSpec((rows, width), lambda n, i, *_: (n * nblk + jnp.maximum(i - 1, 0), cb))
    return cur, prev


def _dil_attn(hp, gi, n_seq, t_len, bias):
    win, dil = DIL_GROUPS[gi]
    n_kv = DIL_KVH if dil == 1 else 1
    nq = n_kv * DIL_G
    rows = BLOCK * dil
    nblk = t_len // rows
    m = n_seq * t_len

    def head_spec(unit0, j, back):
        return pl.BlockSpec((rows, DIL_HD),
                            lambda n, i, h: (n * nblk + jnp.maximum(i - back, 0), unit0 + h * n_kv + j))

    qspecs = [pl.BlockSpec((rows, DIL_HD), functools.partial(
        lambda n, i, h, q: (n * nblk + i, U_BQ + gi * DIL_QH + h * nq + q), q=q)) for q in range(nq)]
    kvspecs = []
    for j in range(n_kv):
        kvspecs += [head_spec(U_BK + gi * DIL_KVH, j, 0), head_spec(U_BK + gi * DIL_KVH, j, 1),
                    head_spec(U_BV + gi * DIL_KVH, j, 0), head_spec(U_BV + gi * DIL_KVH, j, 1)]
    return pl.pallas_call(
        functools.partial(_dil_attn_kernel, window=win // dil, dil=dil, n_kv=n_kv),
        out_shape=(jax.ShapeDtypeStruct((m, DIL_QH * DIL_HD), F32),
                   jax.ShapeDtypeStruct((m, DIL_KVH * LANE), F32)),
        grid=(n_seq, nblk, DIL_KVH // n_kv),
        in_specs=qspecs + kvspecs + [pl.BlockSpec((nq, BLOCK, 2 * BLOCK), lambda n, i, h: (h, 0, 0))],
        out_specs=(pl.BlockSpec((rows, nq * DIL_HD), lambda n, i, h: (n * nblk + i, h)),
                   pl.BlockSpec((rows, n_kv * LANE), lambda n, i, h: (n * nblk + i, h))),
        scratch_shapes=[pltpu.VMEM((rows, DIL_HD), F32)] * nq,
        compiler_params=_cparams(("parallel", "arbitrary", "arbitrary")), name=f"dil_attn{gi}",
    )(*([hp] * (nq + 4 * n_kv)), bias)


def _swa_attn(rq, rk, hp, sink_row, n_seq, t_len):
    wq, wk = SWA_QH * SWA_HD, SWA_KVH * SWA_HD
    nblk = t_len // BLOCK
    qspec, _ = _row_specs(BLOCK, nblk, wq, 0)
    kcur, kprev = _row_specs(BLOCK, nblk, wk, 0)
    vcur, vprev = _row_specs(BLOCK, nblk, wk, U_DV)
    gspec, _ = _row_specs(BLOCK, nblk, wq, U_DG * LANE // wq)
    return pl.pallas_call(
        _swa_attn_kernel,
        out_shape=jax.ShapeDtypeStruct((n_seq * t_len, wq), BF16),
        grid=(n_seq, nblk),
        in_specs=[qspec, kcur, kprev, vcur, vprev, pl.BlockSpec(sink_row.shape, lambda n, i: (0, 0)), gspec],
        out_specs=qspec,
        compiler_params=_cparams(("parallel", "arbitrary")), name="swa_attn",
    )(rq, rk, rk, hp, hp, sink_row, hp)


def _dil_merge_kernel(o0_ref, o1_ref, o2_ref, l0_ref, l1_ref, l2_ref, g_ref, y_ref):
    l0 = l0_ref[...]
    l1 = l1_ref[...]
    l2 = l2_ref[...]
    m = jnp.maximum(jnp.maximum(l0, l1), l2)
    w0 = jnp.exp(l0 - m)
    w1 = jnp.exp(l1 - m)
    w2 = jnp.exp(l2 - m)
    inv = 1.0 / (w0 + w1 + w2)
    w0 = w0 * inv
    w1 = w1 * inv
    w2 = w2 * inv
    for h in range(DIL_QH):
        sl = slice(h * DIL_HD, (h + 1) * DIL_HD)
        c = (h // DIL_G) * LANE + h % DIL_G
        o = (w0[:, c:c + 1] * o0_ref[:, sl] + w1[:, c:c + 1] * o1_ref[:, sl] + w2[:, c:c + 1] * o2_ref[:, sl])
        y_ref[:, sl] = (o * _silu(g_ref[:, sl])).astype(BF16)


def _dil_merge(o0, o1, o2, l0, l1, l2, hp, tm):
    m = o0.shape[0]
    ospec = pl.BlockSpec((tm, BRANCH_W), lambda i: (i, 0))
    lspec = pl.BlockSpec((tm, DIL_KVH * LANE), lambda i: (i, 0))
    return pl.pallas_call(
        _dil_merge_kernel,
        out_shape=jax.ShapeDtypeStruct((m, BRANCH_W), BF16),
        grid=(m // tm,),
        in_specs=[ospec, ospec, ospec, lspec, lspec, lspec,
                  pl.BlockSpec((tm, BRANCH_W), lambda i: (i, U_BG * LANE // BRANCH_W))],
        out_specs=ospec,
        compiler_params=_cparams(("parallel",)),
        name="dil_merge",
    )(o0, o1, o2, l0, l1, l2, hp)


def _lru_gates(cx, wa_ref, wx_ref, ba, bx, lam):
    ra, rx = [], []
    for b in range(LRU_BLOCKS):
        xb = cx[:, b * LRU_BS:(b + 1) * LRU_BS].astype(BF16)
        ra.append(jnp.dot(xb, wa_ref[b], preferred_element_type=F32))
        rx.append(jnp.dot(xb, wx_ref[b], preferred_element_type=F32))
    r = _sigmoid(jnp.concatenate(ra, axis=1) + ba)
    ig = _sigmoid(jnp.concatenate(rx, axis=1) + bx)
    log_a = -LRU_C * r * _softplus(-lam)
    a = jnp.exp(log_a)
    th = jnp.tanh(log_a)
    bterm = jnp.sqrt(-2.0 * th / (1.0 - th)) * (ig * cx)
    return a, bterm


def _lru_kernel(x_ref, g_ref, cw_ref, cb_ref, wa_ref, wx_ref, ba_ref, bx_ref, lam_ref,
                y_ref, hl_ref, prev_ref, h_ref, a_s, b_s):
    tb = x_ref.shape[0]

    @pl.when(pl.program_id(1) == 0)
    def _():
        prev_ref[...] = jnp.zeros_like(prev_ref)
        h_ref[...] = jnp.zeros_like(h_ref)

    x = x_ref[...]
    cx = _conv_rows(x, prev_ref[...], cw_ref[...]) + cb_ref[...]
    prev_ref[...] = x[tb - 8:tb]
    a, bterm = _lru_gates(cx, wa_ref, wx_ref, ba_ref[...], bx_ref[...], lam_ref[...])
    a_s[...] = a
    b_s[...] = bterm

    def step(t, h):
        h = a_s[pl.ds(t, 1), :] * h + b_s[pl.ds(t, 1), :]
        b_s[pl.ds(t, 1), :] = h
        return h

    h = lax.fori_loop(0, tb, step, h_ref[...], unroll=8)
    h_ref[...] = h
    hl_ref[0] = h
    y_ref[...] = (b_s[...] * _silu(g_ref[...])).astype(BF16)


def _lru_prompt(hp, n_seq, t_len, conv_w, conv_b, wa, wx, ba, bx, lam, tb):
    nb = t_len // tb
    vec = pl.BlockSpec((1, LRU_W), lambda n, c: (0, 0))
    wspec = pl.BlockSpec((LRU_BLOCKS, LRU_BS, LRU_BS), lambda n, c: (0, 0, 0))
    return pl.pallas_call(
        _lru_kernel,
        out_shape=(jax.ShapeDtypeStruct((n_seq * t_len, LRU_W), BF16),
                   jax.ShapeDtypeStruct((n_seq, 1, LRU_W), F32)),
        grid=(n_seq, nb),
        in_specs=[pl.BlockSpec((tb, LRU_W), lambda n, c: (n * nb + c, U_CX * LANE // LRU_W)),
                  pl.BlockSpec((tb, LRU_W), lambda n, c: (n * nb + c, U_CG * LANE // LRU_W)),
                  pl.BlockSpec((CONV_W, LRU_W), lambda n, c: (0, 0)),
                  vec, wspec, wspec, vec, vec, vec],
        out_specs=(pl.BlockSpec((tb, LRU_W), lambda n, c: (n * nb + c, 0)),
                   pl.BlockSpec((1, 1, LRU_W), lambda n, c: (n, 0, 0))),
        scratch_shapes=[pltpu.VMEM((8, LRU_W), F32), pltpu.VMEM((1, LRU_W), F32),
                        pltpu.VMEM((tb, LRU_W), F32), pltpu.VMEM((tb, LRU_W), F32)],
        compiler_params=_cparams(("parallel", "arbitrary")),
        name="lru_prompt",
    )(hp, hp, conv_w, conv_b, wa, wx, ba, bx, lam)


def _rope_kernel(q_ref, k_ref, cos_ref, sin_ref, rq_ref, rk_ref):
    cos = cos_ref[...]
    sin = sin_ref[...]
    k = k_ref[...]
    lane = lax.broadcasted_iota(jnp.int32, k.shape, 1)
    first = (lane & (SWA_HD - 1)) < SWA_HD // 2
    half = SWA_HD // 2
    ks = jnp.where(first, pltpu.roll(k, LANE - half, 1), pltpu.roll(k, half, 1))
    rk_ref[...] = k * cos + ks * sin
    q = q_ref[...]
    wq = q.shape[1]
    reps = wq // LANE
    cosq = jnp.concatenate([cos] * reps, axis=1)
    sinq = jnp.concatenate([sin] * reps, axis=1)
    firstq = jnp.concatenate([first] * reps, axis=1)
    qs = jnp.where(firstq, pltpu.roll(q, wq - half, 1), pltpu.roll(q, half, 1))
    rq_ref[...] = (q * cosq + qs * sinq).astype(rq_ref.dtype)


def _rope_prompt(hp, cos_t, sin_t, t_len, tm):
    m = hp.shape[0]
    nt = t_len // tm
    wq = SWA_QH * SWA_HD
    return pl.pallas_call(
        _rope_kernel,
        out_shape=(jax.ShapeDtypeStruct((m, wq), BF16), jax.ShapeDtypeStruct((m, LANE), F32)),
        grid=(m // tm,),
        in_specs=[pl.BlockSpec((tm, wq), lambda i: (i, U_DQ * LANE // wq)),
                  pl.BlockSpec((tm, LANE), lambda i: (i, U_DK)),
                  pl.BlockSpec((tm, LANE), lambda i: (i % nt, 0)),
                  pl.BlockSpec((tm, LANE), lambda i: (i % nt, 0))],
        out_specs=(pl.BlockSpec((tm, wq), lambda i: (i, 0)), pl.BlockSpec((tm, LANE), lambda i: (i, 0))),
        compiler_params=_cparams(("parallel",)),
        name="rope",
    )(hp, hp, cos_t, sin_t)


def _branch_kernel(ya_ref, yb_ref, yc_ref, yd_ref, g0_ref, g1_ref, g2_ref, g3_ref, wb_ref, o_ref, wbf_ref):
    acc = None
    @pl.when(pl.program_id(1) == 0)
    def _():
        wbf_ref[...] = wb_ref[...].astype(BF16)

    for b, (y_ref, g_ref) in enumerate(((ya_ref, g0_ref), (yb_ref, g1_ref), (yc_ref, g2_ref), (yd_ref, g3_ref))):
        br = jnp.dot(y_ref[...], wbf_ref[b], preferred_element_type=F32)
        term = _sigmoid(g_ref[...]) * br
        acc = term if acc is None else acc + term
    o_ref[...] = acc.astype(BF16)


def _branch_merge(ys, hp, wb, tm, tn):
    m = ys[0].shape[0]
    nj = D_MODEL // tn
    yspec = pl.BlockSpec((tm, BRANCH_W), lambda j, i: (i, 0))
    gspecs = [pl.BlockSpec((tm, tn), functools.partial(lambda j, i, b: (i, U_MG * LANE // tn + b * nj + j), b=b))
              for b in range(N_BRANCH)]
    return pl.pallas_call(
        _branch_kernel,
        out_shape=jax.ShapeDtypeStruct((m, D_MODEL), BF16),
        grid=(nj, m // tm),
        in_specs=[yspec, yspec, yspec, yspec] + gspecs +
                 [pl.BlockSpec((N_BRANCH, BRANCH_W, tn), lambda j, i: (0, 0, j))],
        out_specs=pl.BlockSpec((tm, tn), lambda j, i: (i, j)),
        scratch_shapes=[pltpu.VMEM((N_BRANCH, BRANCH_W, tn), BF16)],
        compiler_params=_cparams(("parallel", "arbitrary")),
        name="branch_merge",
    )(*ys, hp, hp, hp, hp, wb)


def _out_kernel(m_ref, x_ref, w_ref, g_ref, b_ref, o_ref, wbf_ref):
    @pl.when(pl.program_id(0) == 0)
    def _():
        wbf_ref[...] = w_ref[...].astype(BF16)

    f = jnp.dot(m_ref[...], wbf_ref[...], preferred_element_type=F32)
    z = DN_ALPHA * x_ref[...] + f
    mu = jnp.mean(z, axis=-1, keepdims=True)
    zc = z - mu
    var = jnp.mean(zc * zc, axis=-1, keepdims=True)
    o_ref[...] = zc * lax.rsqrt(var + LN_EPS) * g_ref[...] + b_ref[...]


def _out_proj(merged, x, w_out, ln_g, ln_b, tm):
    m = x.shape[0]
    vec = pl.BlockSpec((1, D_MODEL), lambda i: (0, 0))
    return pl.pallas_call(
        _out_kernel,
        out_shape=jax.ShapeDtypeStruct((m, D_MODEL), F32),
        grid=(m // tm,),
        in_specs=[pl.BlockSpec((tm, D_MODEL), lambda i: (i, 0)),
                  pl.BlockSpec((tm, D_MODEL), lambda i: (i, 0)),
                  pl.BlockSpec((D_MODEL, D_MODEL), lambda i: (0, 0), pipeline_mode=pl.Buffered(1)),
                  vec, vec],
        out_specs=pl.BlockSpec((tm, D_MODEL), lambda i: (i, 0)),
        scratch_shapes=[pltpu.VMEM((D_MODEL, D_MODEL), BF16)],
        compiler_params=_cparams(("arbitrary",)),
        name="out_proj",
    )(merged, x, w_out, ln_g, ln_b)


def _gdn_dec_pre_kernel(x_ref, b0_ref, b1_ref, b2_ref, cw_ref, ab_ref, arow_ref, dtb_ref,
                        qkv_ref, beta_ref, eg_ref):
    cw = cw_ref[...]
    y = b0_ref[...] * cw[0:1] + b1_ref[...] * cw[1:2] + b2_ref[...] * cw[2:3] + x_ref[...] * cw[3:4]
    act = _silu(y)
    for h in range(GDN_HEADS):
        sq = slice(h * GDN_DK, (h + 1) * GDN_DK)
        sk = slice((GDN_HEADS + h) * GDN_DK, (GDN_HEADS + h + 1) * GDN_DK)
        q = act[:, sq]
        k = act[:, sk]
        qkv_ref[:, sq] = q * lax.rsqrt(jnp.sum(q * q, axis=-1, keepdims=True) + 1e-6) * (GDN_DK ** -0.5)
        qkv_ref[:, sk] = k * lax.rsqrt(jnp.sum(k * k, axis=-1, keepdims=True) + 1e-6)
    qkv_ref[:, 2 * GDN_HEADS * GDN_DK:] = act[:, 2 * GDN_HEADS * GDN_DK:]
    ab = ab_ref[...]
    beta_ref[...] = _sigmoid(ab)
    eg_ref[...] = jnp.exp(-arow_ref[...] * _softplus(ab + dtb_ref[...]))


def _gdn_dec_pre(x, b0, b1, b2, conv_w, ab, arow, dtb):
    n = x.shape[0]
    full = lambda a: pl.BlockSpec(a.shape, lambda i: (0,) * a.ndim)
    ins = (x, b0, b1, b2, conv_w, ab, arow, dtb)
    return pl.pallas_call(
        _gdn_dec_pre_kernel,
        out_shape=(jax.ShapeDtypeStruct((n, GDN_QKV), F32), jax.ShapeDtypeStruct((n, LANE), F32),
                   jax.ShapeDtypeStruct((n, LANE), F32)),
        grid=(1,),
        in_specs=[full(a) for a in ins],
        out_specs=(pl.BlockSpec((n, GDN_QKV), lambda i: (0, 0)), pl.BlockSpec((n, LANE), lambda i: (0, 0)),
                   pl.BlockSpec((n, LANE), lambda i: (0, 0))),
        compiler_params=_cparams(("arbitrary",)),
        name="gdn_dec_pre",
    )(*ins)


def _gdn_dec_kernel(qt_ref, kt_ref, v_ref, z_ref, beta_ref, eg_ref, nw_ref, s_ref, so_ref, y_ref):
    qt = qt_ref[0]
    kt = kt_ref[0]
    v = v_ref[0]
    z = z_ref[0]
    beta = beta_ref[0]
    eg = eg_ref[0]
    nw = nw_ref[...]
    for h in range(GDN_HEADS):
        s = s_ref[0, h] * eg[:, GDN_HEADS + h:GDN_HEADS + h + 1]
        kcol = kt[:, h:h + 1]
        v_old = jnp.sum(s * kcol, axis=0, keepdims=True)
        delta = (v[h:h + 1, :] - v_old) * beta[:, h:h + 1]
        s = s + kcol * delta
        so_ref[0, h] = s
        o = jnp.sum(s * qt[:, h:h + 1], axis=0, keepdims=True)
        y = o * lax.rsqrt(jnp.mean(o * o, axis=-1, keepdims=True) + RMS_EPS) * nw
        y_ref[0, h:h + 1, :] = y * _silu(z[h:h + 1, :])


def _gdn_dec(qt, kt, v, z, beta, eg, norm_w, state, layer):
    n = qt.shape[0]
    tspec = pl.BlockSpec((1, GDN_DK, GDN_HEADS), lambda i: (i, 0, 0))
    hspec = pl.BlockSpec((1, GDN_HEADS, GDN_DV), lambda i: (i, 0, 0))
    rspec = pl.BlockSpec((1, 1, LANE), lambda i: (i, 0, 0))
    return pl.pallas_call(
        _gdn_dec_kernel,
        out_shape=(jax.ShapeDtypeStruct((n, GDN_HEADS, GDN_DK, GDN_DV), F32),
                   jax.ShapeDtypeStruct((n, GDN_HEADS, GDN_DV), F32)),
        grid=(n,),
        in_specs=[tspec, tspec, hspec, hspec, rspec, rspec,
                  pl.BlockSpec((1, GDN_DV), lambda i: (0, 0)),
                  pl.BlockSpec((None, 1, GDN_HEADS, GDN_DK, GDN_DV), lambda i: (layer, i, 0, 0, 0))],
        out_specs=(pl.BlockSpec((1, GDN_HEADS, GDN_DK, GDN_DV), lambda i: (i, 0, 0, 0)), hspec),
        compiler_params=_cparams(("parallel",)),
        name="gdn_dec",
    )(qt, kt, v, z, beta, eg, norm_w, state)


def _dec_attend(q, kc, vc, knew, vnew, n_g, scale, bias_c=None, bias_n=None, sink=None):
    hq = q.shape[0]
    row = lax.broadcasted_iota(jnp.int32, (hq, 1), 0)
    sc = None
    kn = None
    vn = None
    for kvh in range(len(kc)):
        sel = (row >= kvh * n_g) & (row < (kvh + 1) * n_g)
        s_k = _dot_nt_hi(q, kc[kvh]) * scale
        sc = s_k if sc is None else jnp.where(sel, s_k, sc)
        kn = jnp.broadcast_to(knew[kvh], q.shape) if kn is None else jnp.where(sel, knew[kvh], kn)
        vn = jnp.broadcast_to(vnew[kvh], q.shape) if vn is None else jnp.where(sel, vnew[kvh], vn)
    sn = jnp.sum(q * kn, axis=-1, keepdims=True) * scale
    if bias_c is not None:
        sc = sc + bias_c
        sn = sn + bias_n
    m = jnp.maximum(jnp.max(sc, axis=-1, keepdims=True), sn)
    if sink is not None:
        m = jnp.maximum(m, sink)
    pc = jnp.exp(sc - m)
    pn = jnp.exp(sn - m)
    den = jnp.sum(pc, axis=-1, keepdims=True) + pn
    if sink is not None:
        den = den + jnp.exp(sink - m)
    oc = None
    for kvh in range(len(kc)):
        sel = (row >= kvh * n_g) & (row < (kvh + 1) * n_g)
        o_k = _dot_hi(pc, vc[kvh])
        oc = o_k if oc is None else jnp.where(sel, o_k, oc)
    o = (oc + pn * vn) / den
    return o, m + jnp.log(den)


def _dil_dec_kernel(q_ref, kn_ref, vn_ref, g_ref, c0_ref, c1_ref, c2_ref, bc_ref, bn_ref, y_ref):
    q_all = q_ref[0]
    kn_all = kn_ref[0]
    vn_all = vn_ref[0]
    outs, lses = [], []
    kvw = DIL_KVH * DIL_HD
    for gi, c_ref in enumerate((c0_ref, c1_ref, c2_ref)):
        q = q_all[gi * DIL_QH:(gi + 1) * DIL_QH]
        kc = [c_ref[:, kvh * DIL_HD:(kvh + 1) * DIL_HD] for kvh in range(DIL_KVH)]
        vc = [c_ref[:, kvw + kvh * DIL_HD:kvw + (kvh + 1) * DIL_HD] for kvh in range(DIL_KVH)]
        knew = [kn_all[gi * DIL_KVH + kvh:gi * DIL_KVH + kvh + 1] for kvh in range(DIL_KVH)]
        vnew = [vn_all[gi * DIL_KVH + kvh:gi * DIL_KVH + kvh + 1] for kvh in range(DIL_KVH)]
        o, lse = _dec_attend(q, kc, vc, knew, vnew, DIL_G, DIL_HD ** -0.5, bc_ref[gi], bn_ref[gi])
        outs.append(o)
        lses.append(lse)
    m = jnp.maximum(jnp.maximum(lses[0], lses[1]), lses[2])
    ws = [jnp.exp(l - m) for l in lses]
    inv = 1.0 / (ws[0] + ws[1] + ws[2])
    o = (ws[0] * outs[0] + ws[1] * outs[1] + ws[2] * outs[2]) * inv
    y_ref[0] = o * _silu(g_ref[0])


def _dil_dec(q3, kn3, vn3, g3, views, layer, bias_c, bias_n):
    n = q3.shape[0]
    cspecs = [pl.BlockSpec((None, None) + v.shape[2:], lambda i: (layer, i, 0, 0)) for v in views]
    return pl.pallas_call(
        _dil_dec_kernel,
        out_shape=jax.ShapeDtypeStruct((n, DIL_QH, DIL_HD), F32),
        grid=(n,),
        in_specs=[pl.BlockSpec((1, N_DIL * DIL_QH, DIL_HD), lambda i: (i, 0, 0)),
                  pl.BlockSpec((1, N_DIL * DIL_KVH, DIL_HD), lambda i: (i, 0, 0)),
                  pl.BlockSpec((1, N_DIL * DIL_KVH, DIL_HD), lambda i: (i, 0, 0)),
                  pl.BlockSpec((1, DIL_QH, DIL_HD), lambda i: (i, 0, 0))] + cspecs +
                 [pl.BlockSpec(bias_c.shape, lambda i: (0, 0, 0)),
                  pl.BlockSpec(bias_n.shape, lambda i: (0, 0, 0))],
        out_specs=pl.BlockSpec((1, DIL_QH, DIL_HD), lambda i: (i, 0, 0)),
        compiler_params=_cparams(("parallel",)),
        name="dil_dec",
    )(q3, kn3, vn3, g3, *views, bias_c, bias_n)


def _swa_dec_kernel(q_ref, kn_ref, vn_ref, g_ref, c_ref, sink_ref, cos_ref, sin_ref, perm_ref, y_ref, rk_ref):
    cos = cos_ref[...]
    sin = sin_ref[...]
    perm = perm_ref[...]
    q = q_ref[0]
    kn = kn_ref[0]
    q = q * cos + _dot_hi(q, perm) * sin
    kn = kn * cos + _dot_hi(kn, perm) * sin
    rk_ref[0] = kn
    vn = vn_ref[0]
    kvw = SWA_KVH * SWA_HD
    kc = [c_ref[:, kvh * SWA_HD:(kvh + 1) * SWA_HD] for kvh in range(SWA_KVH)]
    vc = [c_ref[:, kvw + kvh * SWA_HD:kvw + (kvh + 1) * SWA_HD] for kvh in range(SWA_KVH)]
    knew = [kn[kvh:kvh + 1] for kvh in range(SWA_KVH)]
    vnew = [vn[kvh:kvh + 1] for kvh in range(SWA_KVH)]
    o, _ = _dec_attend(q, kc, vc, knew, vnew, SWA_G, SWA_HD ** -0.5, sink=sink_ref[...])
    y_ref[0] = o * _silu(g_ref[0])


def _swa_dec(q3, kn3, vn3, g3, view, layer, sink_col, cos_d, sin_d, perm):
    n = q3.shape[0]
    win, kvw2 = view.shape[2:]
    return pl.pallas_call(
        _swa_dec_kernel,
        out_shape=(jax.ShapeDtypeStruct((n, SWA_QH, SWA_HD), F32),
                   jax.ShapeDtypeStruct((n, SWA_KVH, SWA_HD), F32)),
        grid=(n,),
        in_specs=[pl.BlockSpec((1, SWA_QH, SWA_HD), lambda i: (i, 0, 0)),
                  pl.BlockSpec((1, SWA_KVH, SWA_HD), lambda i: (i, 0, 0)),
                  pl.BlockSpec((1, SWA_KVH, SWA_HD), lambda i: (i, 0, 0)),
                  pl.BlockSpec((1, SWA_QH, SWA_HD), lambda i: (i, 0, 0)),
                  pl.BlockSpec((None, None, win, kvw2), lambda i: (layer, i, 0, 0)),
                  pl.BlockSpec(sink_col.shape, lambda i: (0, 0)),
                  pl.BlockSpec(cos_d.shape, lambda i: (0, 0)),
                  pl.BlockSpec(sin_d.shape, lambda i: (0, 0)),
                  pl.BlockSpec(perm.shape, lambda i: (0, 0))],
        out_specs=(pl.BlockSpec((1, SWA_QH, SWA_HD), lambda i: (i, 0, 0)),
                   pl.BlockSpec((1, SWA_KVH, SWA_HD), lambda i: (i, 0, 0))),
        compiler_params=_cparams(("parallel",)),
        name="swa_dec",
    )(q3, kn3, vn3, g3, view, sink_col, cos_d, sin_d, perm)


def _lru_dec_kernel(x_ref, g_ref, b0_ref, b1_ref, b2_ref, h0_ref, cw_ref, cb_ref, wa_ref, wx_ref,
                    ba_ref, bx_ref, lam_ref, y_ref, h_ref):
    cw = cw_ref[...]
    cx = (b0_ref[...] * cw[0:1] + b1_ref[...] * cw[1:2] + b2_ref[...] * cw[2:3] + x_ref[...] * cw[3:4]
          + cb_ref[...])
    a, bterm = _lru_gates(cx, wa_ref, wx_ref, ba_ref[...], bx_ref[...], lam_ref[...])
    h = a * h0_ref[...] + bterm
    h_ref[...] = h
    y_ref[...] = h * _silu(g_ref[...])


def _lru_dec(x, g, b0, b1, b2, h0, conv_w, conv_b, wa, wx, ba, bx, lam):
    n = x.shape[0]
    ins = (x, g, b0, b1, b2, h0, conv_w, conv_b, wa, wx, ba, bx, lam)
    full = lambda a: pl.BlockSpec(a.shape, lambda i: (0,) * a.ndim)
    return pl.pallas_call(
        _lru_dec_kernel,
        out_shape=(jax.ShapeDtypeStruct((n, LRU_W), F32), jax.ShapeDtypeStruct((n, LRU_W), F32)),
        grid=(1,),
        in_specs=[full(a) for a in ins],
        out_specs=(pl.BlockSpec((n, LRU_W), lambda i: (0, 0)), pl.BlockSpec((n, LRU_W), lambda i: (0, 0))),
        compiler_params=_cparams(("arbitrary",)),
        name="lru_dec",
    )(*ins)


def _rel_bucket(dist):
    max_exact = REL_BUCKETS // 2
    n = dist.astype(F32)
    large = max_exact + (jnp.log(jnp.maximum(n, 1.0) / max_exact) / math.log(REL_MAX_DIST / max_exact)
                         * (REL_BUCKETS - max_exact)).astype(jnp.int32)
    large = jnp.minimum(large, REL_BUCKETS - 1)
    return jnp.where(dist < max_exact, dist, large)


def _offset_bias(rel_bias, gi, win, dil):
    j = win // dil + 1
    b = rel_bias[_rel_bucket(dil * jnp.arange(j, dtype=jnp.int32))]
    return b[:, gi * DIL_QH:(gi + 1) * DIL_QH].astype(F32)


def _rope_tables(pos):
    half = SWA_HD // 2
    inv = ROPE_THETA ** (-jnp.arange(half, dtype=F32) / half)
    ang = pos.astype(F32)[:, None] * inv[None, :]
    c, s = jnp.cos(ang), jnp.sin(ang)
    return jnp.concatenate([c, c], axis=1), jnp.concatenate([-s, s], axis=1)


SUBLANES = 8


def _shift_kernel(c_ref, n_ref, o_ref, *, k):
    rows = c_ref.shape[0]
    shifted = pltpu.roll(c_ref[...], rows - k, 0)
    row8 = lax.broadcasted_iota(jnp.int32, (SUBLANES, LANE), 0)
    o_ref[0:rows - SUBLANES, :] = shifted[0:rows - SUBLANES]
    o_ref[rows - SUBLANES:rows, :] = jnp.where(row8 >= SUBLANES - k, n_ref[...], shifted[rows - SUBLANES:rows])


def _shift_append(cache, new_rows):
    d, n, w = cache.shape[:3]
    k = int(np.prod(cache.shape[3:])) // LANE
    assert k * LANE == int(np.prod(cache.shape[3:])) and k < SUBLANES
    new8 = jnp.concatenate([jnp.zeros((d, n, SUBLANES - k, LANE), cache.dtype), new_rows.reshape(d, n, k, LANE)],
                           axis=2)
    out = pl.pallas_call(
        functools.partial(_shift_kernel, k=k),
        out_shape=jax.ShapeDtypeStruct((d, n, w * k, LANE), cache.dtype),
        grid=(d, n),
        in_specs=[pl.BlockSpec((None, None, w * k, LANE), lambda a, b: (a, b, 0, 0)),
                  pl.BlockSpec((None, None, SUBLANES, LANE), lambda a, b: (a, b, 0, 0))],
        out_specs=pl.BlockSpec((None, None, w * k, LANE), lambda a, b: (a, b, 0, 0)),
        compiler_params=_cparams(("parallel", "parallel")),
        name="shift_append",
    )(cache.reshape(d, n, w * k, LANE), new8)
    return out.reshape(cache.shape)


_W_PARTS = ((O_MG, O_END), (O_AZ, O_AB), (O_AQKV, O_AZ), (O_BQ, O_BK), (O_BG, O_CX), (O_CX, O_CG), (O_CG, O_DQ),
            (O_DQ, O_DK), (O_DG, O_MG), (O_BK, O_BV), (O_BV, O_BG), (O_DK, O_DV), (O_DV, O_DG))


def _reorder_kernel(w_ref, o_ref):
    c = 0
    for a, b in _W_PARTS:
        o_ref[:, c:c + b - a] = w_ref[:, a:b].astype(BF16)
        c += b - a
    slab = w_ref[:, O_AB:O_AB + LANE]
    lane = lax.broadcasted_iota(jnp.int32, slab.shape, 1)
    o_ref[:, c:c + LANE] = jnp.where(lane < 2 * GDN_HEADS, slab, 0.0).astype(BF16)
    o_ref[:, c + LANE:] = jnp.zeros((o_ref.shape[0], D_INP - c - LANE), BF16)


def _reorder_w_in(w, layer, tk=128):
    d_in = w.shape[2]
    return pl.pallas_call(
        _reorder_kernel,
        out_shape=jax.ShapeDtypeStruct((D_MODEL, D_INP), BF16),
        grid=(D_MODEL // tk,),
        in_specs=[pl.BlockSpec((None, tk, d_in), lambda i: (layer, i, 0))],
        out_specs=pl.BlockSpec((tk, D_INP), lambda i: (i, 0)),
        compiler_params=_cparams(("parallel",)),
        name="reorder_w_in",
    )(w)


def _cols(h, unit, width):
    return h[..., unit * LANE:unit * LANE + width]


def kernel(x_prompt, x_sample, state_gdn, state_gdn_conv, cache_dil_w128, cache_dil_w512, cache_dil_w2048,
           cache_swa, state_rglru, state_rglru_conv, w_in, gdn_conv_w, gdn_a_log, gdn_dt_bias, gdn_norm_w,
           lru_conv_w, lru_conv_b, lru_wa, lru_ba, lru_wx, lru_bx, lru_lambda, swa_sink, rel_bias, w_branch,
           w_out, ln_g, ln_b):
    nb, t_len, _ = x_prompt.shape
    ns = x_sample.shape[0]
    mp = nb * t_len
    caches = (cache_dil_w128, cache_dil_w512, cache_dil_w2048)
    xp = x_prompt.reshape(mp, D_MODEL)
    xs = x_sample.reshape(ns, D_MODEL)

    cos64, sin64 = _rope_tables(jnp.arange(t_len))
    cos_t = jnp.concatenate([cos64, cos64], axis=1)
    sin_t = jnp.concatenate([sin64, sin64], axis=1)
    cos_d, sin_d = _rope_tables(jnp.full((1,), PAST_LEN))
    perm = jnp.asarray(np.roll(np.eye(SWA_HD, dtype=np.float32), SWA_HD // 2, axis=0))
    qi = jnp.arange(BLOCK)[:, None]
    kj = jnp.arange(2 * BLOCK)[None, :]
    bias_p, bias_c, bias_n = [], [], []
    for gi, (win, dil) in enumerate(DIL_GROUPS):
        ob = _offset_bias(rel_bias, gi, win, dil)
        jw = win // dil
        onehot = (jnp.clip(qi + BLOCK - kj, 0, jw)[:, :, None] == jnp.arange(jw + 1)).astype(F32)
        bias_p.append(jnp.einsum('qkj,jh->hqk', onehot, ob, precision=lax.Precision.HIGHEST))
        bias_c.append(jnp.transpose(ob[::-1][:jw], (1, 0)))
        bias_n.append(ob[0][:, None])
    bias_c = jnp.stack(bias_c)
    bias_n = jnp.stack(bias_n)
    dil_views = [c[:, :, ::dil].reshape(DEPTH, ns, win // dil, 2 * DIL_KVH * DIL_HD)
                 for (win, dil), c in zip(DIL_GROUPS, caches)]
    swa_view = cache_swa.reshape(DEPTH, ns, cache_swa.shape[2], 2 * SWA_KVH * SWA_HD)

    new_p = [[] for _ in range(8)]
    new_s = [[] for _ in range(8)]
    for l in range(DEPTH):
        w_r = _reorder_w_in(w_in, l)
        wb = w_branch[l]
        wo = w_out[l]
        wa = lru_wa[l].astype(BF16)
        wx = lru_wx[l].astype(BF16)
        zpad = jnp.zeros((1, LANE - 2 * GDN_HEADS), F32)
        arow = jnp.concatenate([jnp.zeros((1, GDN_HEADS), F32), jnp.exp(gdn_a_log[l])[None], zpad], axis=1)
        dtb = jnp.concatenate([jnp.zeros((1, GDN_HEADS), F32), gdn_dt_bias[l][None], zpad], axis=1)
        norm_w = gdn_norm_w[l][None]
        conv_b = lru_conv_b[l][None]
        ba, bx, lam = lru_ba[l][None], lru_bx[l][None], lru_lambda[l][None]
        lng, lnb = ln_g[l][None], ln_b[l][None]
        sink = swa_sink[l].astype(F32)

        hp = _matmul(xp.astype(BF16), w_r, 2048, 1024)
        hp3 = hp.reshape(nb, t_len, D_INP)
        ya, s_p = _gdn_prompt(hp, nb, t_len, gdn_conv_w[l], arow, dtb, norm_w)
        outs, lses = [], []
        for gi in range(N_DIL):
            o_g, l_g = _dil_attn(hp, gi, nb, t_len, bias_p[gi])
            outs.append(o_g)
            lses.append(l_g)
        yb = _dil_merge(outs[0], outs[1], outs[2], lses[0], lses[1], lses[2], hp, 512)
        yc, h_p = _lru_prompt(hp, nb, t_len, lru_conv_w[l], conv_b, wa, wx, ba, bx, lam, 512)
        rq, rk = _rope_prompt(hp, cos_t, sin_t, t_len, 512)
        sink_row = jnp.concatenate([sink[None], jnp.zeros((1, LANE - SWA_QH), F32)], axis=1)
        yd = _swa_attn(rq, rk, hp, sink_row, nb, t_len)
        merged = _branch_merge((ya, yb, yc, yd), hp, wb, 512, 512)
        xp_new = _out_proj(merged, xp, wo, lng, lnb, 256)

        new_p[0].append(s_p)
        new_p[1].append(_cols(hp3, U_AQKV, GDN_QKV)[:, t_len - (CONV_W - 1):])
        for gi, (win, dil) in enumerate(DIL_GROUPS):
            kk = _cols(hp3, U_BK + gi * DIL_KVH, DIL_KVH * DIL_HD)[:, t_len - win:]
            vv = _cols(hp3, U_BV + gi * DIL_KVH, DIL_KVH * DIL_HD)[:, t_len - win:]
            new_p[2 + gi].append(jnp.stack([kk, vv], axis=2).reshape(nb, win, 2, DIL_KVH, DIL_HD))
        kk = rk.reshape(nb, t_len, LANE)[:, t_len - SWA_WINDOW:]
        vv = _cols(hp3, U_DV, LANE)[:, t_len - SWA_WINDOW:]
        new_p[5].append(jnp.stack([kk, vv], axis=2).reshape(nb, SWA_WINDOW, 2, SWA_KVH, SWA_HD))
        new_p[6].append(h_p.reshape(nb, LRU_W))
        new_p[7].append(_cols(hp3, U_CX, LRU_W)[:, t_len - (CONV_W - 1):])

        hs = _matmul(xs.astype(BF16), w_r, ns, 1024)
        gbuf = state_gdn_conv[l]
        a_qkv_s = _cols(hs, U_AQKV, GDN_QKV)
        qkv_n, beta_s, eg_s = _gdn_dec_pre(a_qkv_s, gbuf[:, 0], gbuf[:, 1], gbuf[:, 2], gdn_conv_w[l],
                                           _cols(hs, U_AB, LANE), arow, dtb)
        qkv4 = qkv_n.reshape(ns, 3, GDN_HEADS, GDN_DK)
        s_s, ya_s = _gdn_dec(jnp.swapaxes(qkv4[:, 0], 1, 2), jnp.swapaxes(qkv4[:, 1], 1, 2), qkv4[:, 2],
                             _cols(hs, U_AZ, BRANCH_W).reshape(ns, GDN_HEADS, GDN_DV),
                             beta_s[:, None], eg_s[:, None], norm_w, state_gdn, l)
        bk_s = _cols(hs, U_BK, N_DIL * DIL_KVH * DIL_HD)
        bv_s = _cols(hs, U_BV, N_DIL * DIL_KVH * DIL_HD)
        yb_s = _dil_dec(_cols(hs, U_BQ, N_DIL * BRANCH_W).reshape(ns, N_DIL * DIL_QH, DIL_HD),
                        bk_s.reshape(ns, N_DIL * DIL_KVH, DIL_HD), bv_s.reshape(ns, N_DIL * DIL_KVH, DIL_HD),
                        _cols(hs, U_BG, BRANCH_W).reshape(ns, DIL_QH, DIL_HD), dil_views, l, bias_c, bias_n)
        lbuf = state_rglru_conv[l]
        cx_s = _cols(hs, U_CX, LRU_W)
        yc_s, h_s = _lru_dec(cx_s, _cols(hs, U_CG, LRU_W), lbuf[:, 0], lbuf[:, 1], lbuf[:, 2], state_rglru[l],
                             lru_conv_w[l], conv_b, wa, wx, ba, bx, lam)
        dv_s = _cols(hs, U_DV, LANE)
        yd_s, rk_s = _swa_dec(_cols(hs, U_DQ, BRANCH_W).reshape(ns, SWA_QH, SWA_HD),
                              _cols(hs, U_DK, LANE).reshape(ns, SWA_KVH, SWA_HD),
                              dv_s.reshape(ns, SWA_KVH, SWA_HD),
                              _cols(hs, U_DG, BRANCH_W).reshape(ns, SWA_QH, SWA_HD),
                              swa_view, l, sink.reshape(SWA_QH, 1), cos_d, sin_d, perm)
        ys_s = (ya_s.reshape(ns, BRANCH_W).astype(BF16), yb_s.reshape(ns, BRANCH_W).astype(BF16),
                yc_s.astype(BF16), yd_s.reshape(ns, BRANCH_W).astype(BF16))
        merged_s = _branch_merge(ys_s, hs, wb, ns, 512)
        xs_new = _out_proj(merged_s, xs, wo, lng, lnb, ns)

        new_s[0].append(s_s)
        new_s[1].append(jnp.concatenate([gbuf[:, 1:], a_qkv_s[:, None]], axis=1))
        for gi in range(N_DIL):
            kk = bk_s[:, gi * DIL_KVH * DIL_HD:(gi + 1) * DIL_KVH * DIL_HD].reshape(ns, 1, DIL_KVH, DIL_HD)
            vv = bv_s[:, gi * DIL_KVH * DIL_HD:(gi + 1) * DIL_KVH * DIL_HD].reshape(ns, 1, DIL_KVH, DIL_HD)
            new_s[2 + gi].append(jnp.stack([kk, vv], axis=2))
        new_s[5].append(jnp.stack([rk_s[:, None], dv_s.reshape(ns, 1, SWA_KVH, SWA_HD)], axis=2))
        new_s[6].append(h_s)
        new_s[7].append(jnp.concatenate([lbuf[:, 1:], cx_s[:, None]], axis=1))

        xp, xs = xp_new, xs_new

    p = [jnp.stack(v) for v in new_p]
    s = [jnp.stack(new_s[i]) for i in (0, 1)]
    for i, c in zip((2, 3, 4, 5), caches + (cache_swa,)):
        s.append(_shift_append(c, jnp.stack(new_s[i])))
    s += [jnp.stack(new_s[i]) for i in (6, 7)]
    return (xp.reshape(nb, t_len, D_MODEL), xs.reshape(ns, 1, D_MODEL),
            p[0], s[0], p[1], s[1], p[2], s[2], p[3], s[3], p[4], s[4], p[5], s[5], p[6], s[6], p[7], s[7])
```

```python
import functools
import math

import jax
import jax.numpy as jnp
import numpy as np
from jax import lax
from jax.experimental import pallas as pl
from jax.experimental.pallas import tpu as pltpu

F32 = jnp.float32
BF16 = jnp.bfloat16
NEG_INF = -1e30

D_MODEL = 2048
DEPTH = 2
PAST_LEN = 16384
BRANCH_W = D_MODEL // 2
N_BRANCH = 4
CONV_W = 4
BLOCK = 128
GDN_DK = 128
GDN_DV = 128
GDN_HEADS = BRANCH_W // GDN_DV
GDN_QKV = GDN_HEADS * (2 * GDN_DK + GDN_DV)
DIL_GROUPS = ((128, 1), (512, 4), (2048, 16))
N_DIL = len(DIL_GROUPS)
DIL_HD = 128
DIL_QH = BRANCH_W // DIL_HD
DIL_KVH = 2
DIL_G = DIL_QH // DIL_KVH
LRU_W = BRANCH_W
LRU_BLOCKS = 8
LRU_BS = LRU_W // LRU_BLOCKS
LRU_C = 8.0
SWA_HD = 64
SWA_QH = BRANCH_W // SWA_HD
SWA_KVH = 2
SWA_G = SWA_QH // SWA_KVH
SWA_WINDOW = 128
ROPE_THETA = 150000.0
REL_BUCKETS = 32
REL_MAX_DIST = 2048
LN_EPS = 1e-5
RMS_EPS = 1e-6
DN_ALPHA = (2.0 * DEPTH) ** 0.25

IN_SIZES = (GDN_QKV, GDN_HEADS * GDN_DV, GDN_HEADS, GDN_HEADS,
            N_DIL * DIL_QH * DIL_HD, N_DIL * DIL_KVH * DIL_HD, N_DIL * DIL_KVH * DIL_HD, DIL_QH * DIL_HD,
            LRU_W, LRU_W,
            SWA_QH * SWA_HD, SWA_KVH * SWA_HD, SWA_KVH * SWA_HD, SWA_QH * SWA_HD,
            N_BRANCH * D_MODEL)
_OFF = [0]
for _s in IN_SIZES:
    _OFF.append(_OFF[-1] + _s)
(O_AQKV, O_AZ, O_AB, O_AA, O_BQ, O_BK, O_BV, O_BG, O_CX, O_CG, O_DQ, O_DK, O_DV, O_DG, O_MG, O_END) = _OFF

LANE = 128
U_MG, U_AZ, U_AQKV, U_BQ, U_BG, U_CX, U_CG, U_DQ, U_DG, U_BK, U_BV = (
    0, 64, 72, 96, 120, 128, 136, 144, 152, 160, 166)
N_UNITS = 172
D_INP = N_UNITS * LANE
T_DK, T_DV, T_AB = 0, 1, 2
D_TAIL = 3 * LANE
IN_TN = 512
_TILE_SRC = tuple(seg0 + IN_TN * t for seg0, width in (
    (O_MG, O_END - O_MG), (O_AZ, O_AB - O_AZ), (O_AQKV, O_AZ - O_AQKV), (O_BQ, O_BK - O_BQ), (O_BG, O_CX - O_BG),
    (O_CX, O_CG - O_CX), (O_CG, O_DQ - O_CG), (O_DQ, O_DK - O_DQ), (O_DG, O_MG - O_DG), (O_BK, O_BG - O_BK))
    for t in range(width // IN_TN))
_SRC_ALIGN = 2 * GDN_HEADS
assert len(_TILE_SRC) * IN_TN == D_INP and all(s % _SRC_ALIGN == 0 for s in _TILE_SRC)
GDN_CHUNK = 128
VMEM_LIMIT = 56 * 1024 * 1024


def _cparams(sem):
    return pltpu.CompilerParams(dimension_semantics=sem, vmem_limit_bytes=VMEM_LIMIT)


def _sigmoid(x):
    return 1.0 / (1.0 + jnp.exp(-x))


def _silu(x):
    return x * _sigmoid(x)


def _softplus(x):
    return jnp.maximum(x, 0.0) + jnp.log1p(jnp.exp(-jnp.abs(x)))


def _dot(a, b):
    return jnp.dot(a.astype(BF16), b.astype(BF16), preferred_element_type=F32)


def _dot_nt(a, b):
    return lax.dot_general(a.astype(BF16), b.astype(BF16), (((1,), (1,)), ((), ())),
                           preferred_element_type=F32)


def _dot_hi(a, b):
    return jnp.dot(a, b, preferred_element_type=F32, precision=lax.Precision.HIGHEST)


def _split2(a):
    hi = a.astype(BF16)
    return hi, (a - hi.astype(F32)).astype(BF16)


def _dot3(a, b):
    ah, al = _split2(a)
    bh, bl = _split2(b)
    return (jnp.dot(ah, bh, preferred_element_type=F32) + jnp.dot(ah, bl, preferred_element_type=F32)
            + jnp.dot(al, bh, preferred_element_type=F32))


def _dot_exact_lhs(a, b):
    a = a.astype(BF16)
    b0 = b.astype(BF16)
    r1 = b - b0.astype(F32)
    b1 = r1.astype(BF16)
    b2 = (r1 - b1.astype(F32)).astype(BF16)
    return (jnp.dot(a, b0, preferred_element_type=F32) + jnp.dot(a, b1, preferred_element_type=F32)
            + jnp.dot(a, b2, preferred_element_type=F32))


def _dot_nt_hi(a, b):
    return lax.dot_general(a, b, (((1,), (1,)), ((), ())), preferred_element_type=F32,
                           precision=lax.Precision.HIGHEST)


def _in_proj_kernel(tab_ref, x_ref, w_ref, o_ref):
    del tab_ref
    o_ref[...] = _dot_nt(x_ref[...], w_ref[0])


def _in_proj(x, wt, layer, tm):
    m, k = x.shape
    grid_spec = pltpu.PrefetchScalarGridSpec(
        num_scalar_prefetch=1,
        grid=(m // tm, D_INP // IN_TN),
        in_specs=[pl.BlockSpec((tm, k), lambda i, j, tab: (i, 0)),
                  pl.BlockSpec((pl.Element(1), pl.Element(IN_TN), pl.Element(k)),
                               lambda i, j, tab: (layer, tab[j] * _SRC_ALIGN, 0))],
        out_specs=pl.BlockSpec((tm, IN_TN), lambda i, j, tab: (i, j)))
    return pl.pallas_call(
        _in_proj_kernel,
        out_shape=jax.ShapeDtypeStruct((m, D_INP), F32),
        grid_spec=grid_spec,
        compiler_params=_cparams(("parallel", "arbitrary")),
        name="in_proj",
    )(jnp.asarray([s // _SRC_ALIGN for s in _TILE_SRC], jnp.int32), x, wt)


def _in_tail_kernel(x_ref, wkv_ref, wab_ref, o_ref):
    x = x_ref[...]
    o_ref[:, 0:2 * LANE] = _dot_nt(x, wkv_ref[0])
    o_ref[:, 2 * LANE:] = _dot_nt(x, wab_ref[0])


def _in_proj_tail(x, wt, layer, tm):
    m, k = x.shape
    return pl.pallas_call(
        _in_tail_kernel,
        out_shape=jax.ShapeDtypeStruct((m, D_TAIL), F32),
        grid=(m // tm,),
        in_specs=[pl.BlockSpec((tm, k), lambda i: (i, 0)),
                  pl.BlockSpec((pl.Element(1), pl.Element(2 * LANE), pl.Element(k)), lambda i: (layer, O_DK, 0)),
                  pl.BlockSpec((pl.Element(1), pl.Element(LANE), pl.Element(k)), lambda i: (layer, O_AB, 0))],
        out_specs=pl.BlockSpec((tm, D_TAIL), lambda i: (i, 0)),
        compiler_params=_cparams(("parallel",)),
        name="in_proj_tail",
    )(x, wt, wt)


def _conv_rows(x, prev8, cw):
    row8 = lax.broadcasted_iota(jnp.int32, (8, x.shape[1]), 0)
    y = x * cw[CONV_W - 1:CONV_W]
    for k in range(1, CONV_W):
        xk = pltpu.roll(x, k, 0)
        fk = pltpu.roll(prev8, k, 0)
        head = jnp.where(row8 < k, fk, xk[0:8])
        xk = jnp.concatenate([head, xk[8:]], axis=0)
        y = y + xk * cw[CONV_W - 1 - k:CONV_W - k]
    return y


def _tri_inv(lmat, ri, ci):
    c = lmat[0].shape[0]
    eye = (ri == ci).astype(F32)
    blk = (ri >> 4) == (ci >> 4)
    d = [jnp.where(blk, l, 0.0) for l in lmat]
    x = [eye - di for di in d]
    p = [_dot(di, di) for di in d]
    x = [xi + _dot(xi, pi) for xi, pi in zip(x, p)]
    p = [_dot(pi, pi) for pi in p]
    x = [xi + _dot(xi, pi) for xi, pi in zip(x, p)]
    p = [_dot(pi, pi) for pi in p]
    x = [xi + _dot(xi, pi) for xi, pi in zip(x, p)]
    sh = 4
    while (1 << sh) < c:
        lower_left = ((ri >> (sh + 1)) == (ci >> (sh + 1))) & (((ri >> sh) & 1) == 1) & (((ci >> sh) & 1) == 0)
        t = [_dot(jnp.where(lower_left, l, 0.0), xi) for l, xi in zip(lmat, x)]
        x = [xi - _dot(xi, ti) for xi, ti in zip(x, t)]
        sh += 1
    return x


def _tri_solve(lmat, rhs, ri, ci):
    tinv = _tri_inv(lmat, ri, ci)
    sol = [_dot(t, r) for t, r in zip(tinv, rhs)]
    resid = [r - s - _dot3(l, s) for r, s, l in zip(rhs, sol, lmat)]
    return [s + _dot(t, r) for s, t, r in zip(sol, tinv, resid)]


def _gdn_kernel(qkv_ref, z_ref, ab_ref, cw_ref, arow_ref, dtb_ref, nw_ref, y_ref, s_ref, prev_ref):
    cidx = pl.program_id(1)
    C = GDN_CHUNK

    @pl.when(cidx == 0)
    def _():
        s_ref[...] = jnp.zeros_like(s_ref)
        prev_ref[...] = jnp.zeros_like(prev_ref)

    x = qkv_ref[...]
    act = _silu(_conv_rows(x, prev_ref[...], cw_ref[...]))
    prev_ref[...] = x[C - 8:C]

    ab = ab_ref[...]
    beta_t = _sigmoid(ab)
    g_t = -arow_ref[...] * _softplus(ab + dtb_ref[...])
    ri = lax.broadcasted_iota(jnp.int32, (C, C), 0)
    ci = lax.broadcasted_iota(jnp.int32, (C, C), 1)
    incl = ri >= ci
    strict = ri > ci
    gc_t = _dot_exact_lhs(incl.astype(F32), g_t)
    gc_tt = gc_t.T
    egc_t = jnp.exp(gc_t)
    nw = nw_ref[...]
    z = z_ref[...]

    heads = range(GDN_HEADS)
    q, k, v, beta, gcc, egc, decay, glc = [], [], [], [], [], [], [], []
    for h in heads:
        qh = act[:, h * GDN_DK:(h + 1) * GDN_DK]
        kh = act[:, (GDN_HEADS + h) * GDN_DK:(GDN_HEADS + h + 1) * GDN_DK]
        q.append(qh * lax.rsqrt(jnp.sum(qh * qh, axis=-1, keepdims=True) + 1e-6) * (GDN_DK ** -0.5))
        k.append(kh * lax.rsqrt(jnp.sum(kh * kh, axis=-1, keepdims=True) + 1e-6))
        v.append(act[:, (2 * GDN_HEADS + h) * GDN_DK:(2 * GDN_HEADS + h + 1) * GDN_DK])
        beta.append(beta_t[:, h:h + 1])
        gcc.append(gc_t[:, GDN_HEADS + h:GDN_HEADS + h + 1])
        gcr = gc_tt[GDN_HEADS + h:GDN_HEADS + h + 1, :]
        egc.append(egc_t[:, GDN_HEADS + h:GDN_HEADS + h + 1])
        glc.append(gc_t[C - 1:C, GDN_HEADS + h:GDN_HEADS + h + 1])
        decay.append(jnp.where(incl, jnp.exp(jnp.where(incl, gcc[h] - gcr, 0.0)), 0.0))
    kb = [k[h] * beta[h] for h in heads]
    lmat = [jnp.where(strict, _dot_nt(kb[h], k[h]) * decay[h], 0.0) for h in heads]
    rhs = [jnp.concatenate([v[h] * beta[h], kb[h] * egc[h]], axis=1) for h in heads]
    sol = _tri_solve(lmat, rhs, ri, ci)
    qk = [_dot_nt(q[h], k[h]) * decay[h] for h in heads]
    s = [s_ref[0, h] for h in heads]
    v_new = [sol[h][:, :GDN_DV] - _dot(sol[h][:, GDN_DV:], s[h]) for h in heads]
    o = [_dot(q[h] * egc[h], s[h]) + _dot(qk[h], v_new[h]) for h in heads]
    for h in heads:
        kd = k[h] * jnp.exp(glc[h] - gcc[h])
        s_ref[0, h] = s[h] * jnp.exp(glc[h]) + _dot(kd.T, v_new[h])
    for h in heads:
        y = o[h] * lax.rsqrt(jnp.mean(o[h] * o[h], axis=-1, keepdims=True) + RMS_EPS) * nw
        y = y * _silu(z[:, h * GDN_DV:(h + 1) * GDN_DV])
        y_ref[:, h * GDN_DV:(h + 1) * GDN_DV] = y.astype(BF16)


def _gdn_prompt(hp, ht, n_seq, t_len, conv_w, arow, dtb, norm_w):
    C = GDN_CHUNK
    nc = t_len // C
    return pl.pallas_call(
        _gdn_kernel,
        out_shape=(jax.ShapeDtypeStruct((n_seq * t_len, BRANCH_W), BF16),
                   jax.ShapeDtypeStruct((n_seq, GDN_HEADS, GDN_DK, GDN_DV), F32)),
        grid=(n_seq, nc),
        in_specs=[pl.BlockSpec((C, GDN_QKV), lambda n, c: (n * nc + c, U_AQKV * LANE // GDN_QKV)),
                  pl.BlockSpec((C, BRANCH_W), lambda n, c: (n * nc + c, U_AZ * LANE // BRANCH_W)),
                  pl.BlockSpec((C, LANE), lambda n, c: (n * nc + c, T_AB)),
                  pl.BlockSpec((CONV_W, GDN_QKV), lambda n, c: (0, 0)),
                  pl.BlockSpec((1, LANE), lambda n, c: (0, 0)),
                  pl.BlockSpec((1, LANE), lambda n, c: (0, 0)),
                  pl.BlockSpec((1, GDN_DV), lambda n, c: (0, 0))],
        out_specs=(pl.BlockSpec((C, BRANCH_W), lambda n, c: (n * nc + c, 0)),
                   pl.BlockSpec((1, GDN_HEADS, GDN_DK, GDN_DV), lambda n, c: (n, 0, 0, 0))),
        scratch_shapes=[pltpu.VMEM((8, GDN_QKV), F32)],
        compiler_params=_cparams(("parallel", "arbitrary")),
        name="gdn_prompt",
    )(hp, hp, ht, conv_w, arow, dtb, norm_w)


def _band_mask(rows, cols, window, first_block):
    qi = lax.broadcasted_iota(jnp.int32, (rows, cols), 0) & (BLOCK - 1)
    kj = lax.broadcasted_iota(jnp.int32, (rows, cols), 1) & (2 * BLOCK - 1)
    off = qi + BLOCK - kj
    kmin = jnp.where(first_block, BLOCK, 0)
    return (off >= 0) & (off <= window) & (kj >= kmin)


def _dil_attn_kernel(*refs, window, dil, n_kv):
    hd, n_g = DIL_HD, DIL_G
    nq = n_kv * n_g
    q_refs = refs[:nq]
    kv_refs = refs[nq:nq + 4 * n_kv]
    bias_ref, o_ref, lse_ref = refs[nq + 4 * n_kv:nq + 4 * n_kv + 3]
    o_scr = refs[nq + 4 * n_kv + 3:]
    rows = n_g * BLOCK
    valid = _band_mask(rows, 2 * BLOCK, window, pl.program_id(1) == 0)
    lane = lax.broadcasted_iota(jnp.int32, (BLOCK, LANE), 1)

    def sub(r):
        return pl.ds(r, BLOCK, stride=dil) if dil > 1 else slice(None)

    chains = [(j, r) for r in range(dil) for j in range(n_kv)]
    for c0 in range(0, len(chains), 2):
        grp = chains[c0:c0 + 2]
        s = []
        for j, r in grp:
            kc_ref, kp_ref = kv_refs[4 * j], kv_refs[4 * j + 1]
            qh = jnp.concatenate([q_refs[j * n_g + g][sub(r), :] for g in range(n_g)], axis=0)
            kcat = jnp.concatenate([kp_ref[sub(r), :], kc_ref[sub(r), :]], axis=0)
            bias = bias_ref[j * n_g:(j + 1) * n_g].reshape(rows, 2 * BLOCK)
            s.append(jnp.where(valid, _dot_nt(qh, kcat) * (hd ** -0.5) + bias, NEG_INF))
        m = [jnp.max(sc, axis=-1, keepdims=True) for sc in s]
        p = [jnp.exp(sc - mc) for sc, mc in zip(s, m)]
        den = [jnp.sum(pc, axis=-1, keepdims=True) for pc in p]
        for (j, r), pc, mc, dc in zip(grp, p, m, den):
            vc_ref, vp_ref = kv_refs[4 * j + 2], kv_refs[4 * j + 3]
            vcat = jnp.concatenate([vp_ref[sub(r), :], vc_ref[sub(r), :]], axis=0)
            o = _dot(pc, vcat) / dc
            lse = mc + jnp.log(dc)
            lse_t = jnp.zeros((BLOCK, LANE), F32)
            for g in range(n_g):
                o_scr[j * n_g + g][sub(r), :] = o[g * BLOCK:(g + 1) * BLOCK]
                lse_t = jnp.where(lane == g, lse[g * BLOCK:(g + 1) * BLOCK], lse_t)
            lse_ref[sub(r), j * LANE:(j + 1) * LANE] = lse_t
    for q in range(nq):
        o_ref[:, q * hd:(q + 1) * hd] = o_scr[q][...]


def _swa_attn_kernel(q_ref, kc_ref, kp_ref, vc_ref, vp_ref, sink_ref, gate_ref, y_ref):
    half = SWA_HD
    n_slab = SWA_QH // 2
    slab_per_kvh = n_slab // SWA_KVH
    k2 = jnp.concatenate([kp_ref[...], kc_ref[...]], axis=0)
    v2 = jnp.concatenate([vp_ref[...], vc_ref[...]], axis=0)
    lane = lax.broadcasted_iota(jnp.int32, k2.shape, 1)
    lo = lane < half
    k2r = pltpu.roll(k2, half, 1)
    v2r = pltpu.roll(v2, half, 1)
    kk = [jnp.concatenate([jnp.where(lo, k2, 0.0), jnp.where(lo, 0.0, k2r)], axis=0),
          jnp.concatenate([jnp.where(lo, k2r, 0.0), jnp.where(lo, 0.0, k2)], axis=0)]
    vv = [jnp.concatenate([jnp.where(lo, v2, 0.0), jnp.where(lo, 0.0, v2r)], axis=0),
          jnp.concatenate([jnp.where(lo, v2r, 0.0), jnp.where(lo, 0.0, v2)], axis=0)]
    kk = [a.astype(BF16) for a in kk]
    vv = [a.astype(BF16) for a in vv]
    valid = _band_mask(BLOCK, 4 * BLOCK, SWA_WINDOW, pl.program_id(1) == 0)
    sink = sink_ref[...]
    lane_o = lax.broadcasted_iota(jnp.int32, (BLOCK, LANE), 1) < half
    slabs = range(n_slab)
    s = []
    for a in slabs:
        sa = lax.dot_general(q_ref[:, a * LANE:(a + 1) * LANE], kk[a // slab_per_kvh],
                             (((1,), (1,)), ((), ())), preferred_element_type=F32) * (SWA_HD ** -0.5)
        s.append(jnp.where(valid, sa, NEG_INF))
    heads = range(SWA_QH)
    sh = [s[h // 2][:, (h % 2) * 2 * BLOCK:(h % 2 + 1) * 2 * BLOCK] for h in heads]
    snk = [sink[:, h:h + 1] for h in heads]
    m = [jnp.maximum(jnp.max(sh[h], axis=-1, keepdims=True), snk[h]) for h in heads]
    pe = [jnp.exp(sh[h] - m[h]) for h in heads]
    dh = [jnp.sum(pe[h], axis=-1, keepdims=True) + jnp.exp(snk[h] - m[h]) for h in heads]
    for a in slabs:
        p = jnp.concatenate([pe[2 * a], pe[2 * a + 1]], axis=1).astype(BF16)
        den = jnp.where(lane_o, dh[2 * a], dh[2 * a + 1])
        o = jnp.dot(p, vv[a // slab_per_kvh], preferred_element_type=F32) / den
        y_ref[:, a * LANE:(a + 1) * LANE] = (o * _silu(gate_ref[:, a * LANE:(a + 1) * LANE])).astype(BF16)


def _row_specs(rows, nblk, width, cb):
    cur = pl.BlockSpec((rows, width), lambda n, i, *_: (n * nblk + i, cb))
    prev = pl.BlockSpec((rows, width), lambda n, i, *_: (n * nblk + jnp.maximum(i - 1, 0), cb))
    return cur, prev


def _dil_attn(hp, gi, n_seq, t_len, bias):
    win, dil = DIL_GROUPS[gi]
    n_kv = DIL_KVH if dil == 1 else 1
    nq = n_kv * DIL_G
    rows = BLOCK * dil
    nblk = t_len // rows
    m = n_seq * t_len

    def head_spec(unit0, j, back):
        return pl.BlockSpec((rows, DIL_HD),
                            lambda n, i, h: (n * nblk + jnp.maximum(i - back, 0), unit0 + h * n_kv + j))

    qspecs = [pl.BlockSpec((rows, DIL_HD), functools.partial(
        lambda n, i, h, q: (n * nblk + i, U_BQ + gi * DIL_QH + h * nq + q), q=q)) for q in range(nq)]
    kvspecs = []
    for j in range(n_kv):
        kvspecs += [head_spec(U_BK + gi * DIL_KVH, j, 0), head_spec(U_BK + gi * DIL_KVH, j, 1),
                    head_spec(U_BV + gi * DIL_KVH, j, 0), head_spec(U_BV + gi * DIL_KVH, j, 1)]
    return pl.pallas_call(
        functools.partial(_dil_attn_kernel, window=win // dil, dil=dil, n_kv=n_kv),
        out_shape=(jax.ShapeDtypeStruct((m, DIL_QH * DIL_HD), F32),
                   jax.ShapeDtypeStruct((m, DIL_KVH * LANE), F32)),
        grid=(n_seq, nblk, DIL_KVH // n_kv),
        in_specs=qspecs + kvspecs + [pl.BlockSpec((nq, BLOCK, 2 * BLOCK), lambda n, i, h: (h, 0, 0))],
        out_specs=(pl.BlockSpec((rows, nq * DIL_HD), lambda n, i, h: (n * nblk + i, h)),
                   pl.BlockSpec((rows, n_kv * LANE), lambda n, i, h: (n * nblk + i, h))),
        scratch_shapes=[pltpu.VMEM((rows, DIL_HD), F32)] * nq,
        compiler_params=_cparams(("parallel", "arbitrary", "arbitrary")), name=f"dil_attn{gi}",
    )(*([hp] * (nq + 4 * n_kv)), bias)


def _swa_attn(rq, rk, hp, ht, sink_row, n_seq, t_len):
    wq, wk = SWA_QH * SWA_HD, SWA_KVH * SWA_HD
    nblk = t_len // BLOCK
    qspec, _ = _row_specs(BLOCK, nblk, wq, 0)
    kcur, kprev = _row_specs(BLOCK, nblk, wk, 0)
    vcur, vprev = _row_specs(BLOCK, nblk, wk, T_DV)
    gspec, _ = _row_specs(BLOCK, nblk, wq, U_DG * LANE // wq)
    return pl.pallas_call(
        _swa_attn_kernel,
        out_shape=jax.ShapeDtypeStruct((n_seq * t_len, wq), BF16),
        grid=(n_seq, nblk),
        in_specs=[qspec, kcur, kprev, vcur, vprev, pl.BlockSpec(sink_row.shape, lambda n, i: (0, 0)), gspec],
        out_specs=qspec,
        compiler_params=_cparams(("parallel", "arbitrary")), name="swa_attn",
    )(rq, rk, rk, ht, ht, sink_row, hp)


def _dil_merge_kernel(o0_ref, o1_ref, o2_ref, l0_ref, l1_ref, l2_ref, g_ref, y_ref):
    l0 = l0_ref[...]
    l1 = l1_ref[...]
    l2 = l2_ref[...]
    m = jnp.maximum(jnp.maximum(l0, l1), l2)
    w0 = jnp.exp(l0 - m)
    w1 = jnp.exp(l1 - m)
    w2 = jnp.exp(l2 - m)
    inv = 1.0 / (w0 + w1 + w2)
    w0 = w0 * inv
    w1 = w1 * inv
    w2 = w2 * inv
    for h in range(DIL_QH):
        sl = slice(h * DIL_HD, (h + 1) * DIL_HD)
        c = (h // DIL_G) * LANE + h % DIL_G
        o = (w0[:, c:c + 1] * o0_ref[:, sl] + w1[:, c:c + 1] * o1_ref[:, sl] + w2[:, c:c + 1] * o2_ref[:, sl])
        y_ref[:, sl] = (o * _silu(g_ref[:, sl])).astype(BF16)


def _dil_merge(o0, o1, o2, l0, l1, l2, hp, tm):
    m = o0.shape[0]
    ospec = pl.BlockSpec((tm, BRANCH_W), lambda i: (i, 0))
    lspec = pl.BlockSpec((tm, DIL_KVH * LANE), lambda i: (i, 0))
    return pl.pallas_call(
        _dil_merge_kernel,
        out_shape=jax.ShapeDtypeStruct((m, BRANCH_W), BF16),
        grid=(m // tm,),
        in_specs=[ospec, ospec, ospec, lspec, lspec, lspec,
                  pl.BlockSpec((tm, BRANCH_W), lambda i: (i, U_BG * LANE // BRANCH_W))],
        out_specs=ospec,
        compiler_params=_cparams(("parallel",)),
        name="dil_merge",
    )(o0, o1, o2, l0, l1, l2, hp)


def _lru_gates(cx, wa_ref, wx_ref, ba, bx, lam):
    ra, rx = [], []
    for b in range(LRU_BLOCKS):
        xb = cx[:, b * LRU_BS:(b + 1) * LRU_BS].astype(BF16)
        ra.append(jnp.dot(xb, wa_ref[b], preferred_element_type=F32))
        rx.append(jnp.dot(xb, wx_ref[b], preferred_element_type=F32))
    r = _sigmoid(jnp.concatenate(ra, axis=1) + ba)
    ig = _sigmoid(jnp.concatenate(rx, axis=1) + bx)
    log_a = -LRU_C * r * _softplus(-lam)
    a = jnp.exp(log_a)
    th = jnp.tanh(log_a)
    bterm = jnp.sqrt(-2.0 * th / (1.0 - th)) * (ig * cx)
    return a, bterm


def _lru_kernel(x_ref, g_ref, cw_ref, cb_ref, wa_ref, wx_ref, ba_ref, bx_ref, lam_ref,
                y_ref, hl_ref, prev_ref, h_ref, a_s, b_s):
    tb = x_ref.shape[0]

    @pl.when(pl.program_id(1) == 0)
    def _():
        prev_ref[...] = jnp.zeros_like(prev_ref)
        h_ref[...] = jnp.zeros_like(h_ref)

    x = x_ref[...]
    cx = _conv_rows(x, prev_ref[...], cw_ref[...]) + cb_ref[...]
    prev_ref[...] = x[tb - 8:tb]
    a, bterm = _lru_gates(cx, wa_ref, wx_ref, ba_ref[...], bx_ref[...], lam_ref[...])
    a_s[...] = a
    b_s[...] = bterm

    def step(t, h):
        h = a_s[pl.ds(t, 1), :] * h + b_s[pl.ds(t, 1), :]
        b_s[pl.ds(t, 1), :] = h
        return h

    h = lax.fori_loop(0, tb, step, h_ref[...], unroll=8)
    h_ref[...] = h
    hl_ref[0] = h
    y_ref[...] = (b_s[...] * _silu(g_ref[...])).astype(BF16)


def _lru_prompt(hp, n_seq, t_len, conv_w, conv_b, wa, wx, ba, bx, lam, tb):
    nb = t_len // tb
    vec = pl.BlockSpec((1, LRU_W), lambda n, c: (0, 0))
    wspec = pl.BlockSpec((LRU_BLOCKS, LRU_BS, LRU_BS), lambda n, c: (0, 0, 0))
    return pl.pallas_call(
        _lru_kernel,
        out_shape=(jax.ShapeDtypeStruct((n_seq * t_len, LRU_W), BF16),
                   jax.ShapeDtypeStruct((n_seq, 1, LRU_W), F32)),
        grid=(n_seq, nb),
        in_specs=[pl.BlockSpec((tb, LRU_W), lambda n, c: (n * nb + c, U_CX * LANE // LRU_W)),
                  pl.BlockSpec((tb, LRU_W), lambda n, c: (n * nb + c, U_CG * LANE // LRU_W)),
                  pl.BlockSpec((CONV_W, LRU_W), lambda n, c: (0, 0)),
                  vec, wspec, wspec, vec, vec, vec],
        out_specs=(pl.BlockSpec((tb, LRU_W), lambda n, c: (n * nb + c, 0)),
                   pl.BlockSpec((1, 1, LRU_W), lambda n, c: (n, 0, 0))),
        scratch_shapes=[pltpu.VMEM((8, LRU_W), F32), pltpu.VMEM((1, LRU_W), F32),
                        pltpu.VMEM((tb, LRU_W), F32), pltpu.VMEM((tb, LRU_W), F32)],
        compiler_params=_cparams(("parallel", "arbitrary")),
        name="lru_prompt",
    )(hp, hp, conv_w, conv_b, wa, wx, ba, bx, lam)


def _rope_kernel(q_ref, k_ref, cos_ref, sin_ref, rq_ref, rk_ref):
    cos = cos_ref[...]
    sin = sin_ref[...]
    k = k_ref[...]
    lane = lax.broadcasted_iota(jnp.int32, k.shape, 1)
    first = (lane & (SWA_HD - 1)) < SWA_HD // 2
    half = SWA_HD // 2
    ks = jnp.where(first, pltpu.roll(k, LANE - half, 1), pltpu.roll(k, half, 1))
    rk_ref[...] = k * cos + ks * sin
    q = q_ref[...]
    wq = q.shape[1]
    reps = wq // LANE
    cosq = jnp.concatenate([cos] * reps, axis=1)
    sinq = jnp.concatenate([sin] * reps, axis=1)
    firstq = jnp.concatenate([first] * reps, axis=1)
    qs = jnp.where(firstq, pltpu.roll(q, wq - half, 1), pltpu.roll(q, half, 1))
    rq_ref[...] = (q * cosq + qs * sinq).astype(rq_ref.dtype)


def _rope_prompt(hp, ht, cos_t, sin_t, t_len, tm):
    m = hp.shape[0]
    nt = t_len // tm
    wq = SWA_QH * SWA_HD
    return pl.pallas_call(
        _rope_kernel,
        out_shape=(jax.ShapeDtypeStruct((m, wq), BF16), jax.ShapeDtypeStruct((m, LANE), F32)),
        grid=(m // tm,),
        in_specs=[pl.BlockSpec((tm, wq), lambda i: (i, U_DQ * LANE // wq)),
                  pl.BlockSpec((tm, LANE), lambda i: (i, T_DK)),
                  pl.BlockSpec((tm, LANE), lambda i: (i % nt, 0)),
                  pl.BlockSpec((tm, LANE), lambda i: (i % nt, 0))],
        out_specs=(pl.BlockSpec((tm, wq), lambda i: (i, 0)), pl.BlockSpec((tm, LANE), lambda i: (i, 0))),
        compiler_params=_cparams(("parallel",)),
        name="rope",
    )(hp, ht, cos_t, sin_t)


def _branch_kernel(ya_ref, yb_ref, yc_ref, yd_ref, g0_ref, g1_ref, g2_ref, g3_ref, wb_ref, o_ref, wbf_ref):
    acc = None
    @pl.when(pl.program_id(1) == 0)
    def _():
        wbf_ref[...] = wb_ref[...].astype(BF16)

    for b, (y_ref, g_ref) in enumerate(((ya_ref, g0_ref), (yb_ref, g1_ref), (yc_ref, g2_ref), (yd_ref, g3_ref))):
        br = jnp.dot(y_ref[...], wbf_ref[b], preferred_element_type=F32)
        term = _sigmoid(g_ref[...]) * br
        acc = term if acc is None else acc + term
    o_ref[...] = acc.astype(BF16)


def _branch_merge(ys, hp, wb, tm, tn):
    m = ys[0].shape[0]
    nj = D_MODEL // tn
    yspec = pl.BlockSpec((tm, BRANCH_W), lambda j, i: (i, 0))
    gspecs = [pl.BlockSpec((tm, tn), functools.partial(lambda j, i, b: (i, U_MG * LANE // tn + b * nj + j), b=b))
              for b in range(N_BRANCH)]
    return pl.pallas_call(
        _branch_kernel,
        out_shape=jax.ShapeDtypeStruct((m, D_MODEL), BF16),
        grid=(nj, m // tm),
        in_specs=[yspec, yspec, yspec, yspec] + gspecs +
                 [pl.BlockSpec((N_BRANCH, BRANCH_W, tn), lambda j, i: (0, 0, j))],
        out_specs=pl.BlockSpec((tm, tn), lambda j, i: (i, j)),
        scratch_shapes=[pltpu.VMEM((N_BRANCH, BRANCH_W, tn), BF16)],
        compiler_params=_cparams(("parallel", "arbitrary")),
        name="branch_merge",
    )(*ys, hp, hp, hp, hp, wb)


def _out_kernel(m_ref, x_ref, w_ref, g_ref, b_ref, o_ref, ob_ref, wbf_ref):
    @pl.when(pl.program_id(0) == 0)
    def _():
        wbf_ref[...] = w_ref[...].astype(BF16)

    f = jnp.dot(m_ref[...], wbf_ref[...], preferred_element_type=F32)
    z = DN_ALPHA * x_ref[...] + f
    mu = jnp.mean(z, axis=-1, keepdims=True)
    zc = z - mu
    var = jnp.mean(zc * zc, axis=-1, keepdims=True)
    y = zc * lax.rsqrt(var + LN_EPS) * g_ref[...] + b_ref[...]
    o_ref[...] = y
    ob_ref[...] = y.astype(BF16)


def _out_proj(merged, x, w_out, ln_g, ln_b, tm):
    m = x.shape[0]
    vec = pl.BlockSpec((1, D_MODEL), lambda i: (0, 0))
    ospec = pl.BlockSpec((tm, D_MODEL), lambda i: (i, 0))
    return pl.pallas_call(
        _out_kernel,
        out_shape=(jax.ShapeDtypeStruct((m, D_MODEL), F32), jax.ShapeDtypeStruct((m, D_MODEL), BF16)),
        grid=(m // tm,),
        in_specs=[pl.BlockSpec((tm, D_MODEL), lambda i: (i, 0)),
                  pl.BlockSpec((tm, D_MODEL), lambda i: (i, 0)),
                  pl.BlockSpec((D_MODEL, D_MODEL), lambda i: (0, 0), pipeline_mode=pl.Buffered(1)),
                  vec, vec],
        out_specs=(ospec, ospec),
        scratch_shapes=[pltpu.VMEM((D_MODEL, D_MODEL), BF16)],
        compiler_params=_cparams(("arbitrary",)),
        name="out_proj",
    )(merged, x, w_out, ln_g, ln_b)


def _gdn_dec_pre_kernel(x_ref, b0_ref, b1_ref, b2_ref, cw_ref, ab_ref, arow_ref, dtb_ref,
                        qkv_ref, beta_ref, eg_ref):
    cw = cw_ref[...]
    y = b0_ref[...] * cw[0:1] + b1_ref[...] * cw[1:2] + b2_ref[...] * cw[2:3] + x_ref[...] * cw[3:4]
    act = _silu(y)
    for h in range(GDN_HEADS):
        sq = slice(h * GDN_DK, (h + 1) * GDN_DK)
        sk = slice((GDN_HEADS + h) * GDN_DK, (GDN_HEADS + h + 1) * GDN_DK)
        q = act[:, sq]
        k = act[:, sk]
        qkv_ref[:, sq] = q * lax.rsqrt(jnp.sum(q * q, axis=-1, keepdims=True) + 1e-6) * (GDN_DK ** -0.5)
        qkv_ref[:, sk] = k * lax.rsqrt(jnp.sum(k * k, axis=-1, keepdims=True) + 1e-6)
    qkv_ref[:, 2 * GDN_HEADS * GDN_DK:] = act[:, 2 * GDN_HEADS * GDN_DK:]
    ab = ab_ref[...]
    beta_ref[...] = _sigmoid(ab)
    eg_ref[...] = jnp.exp(-arow_ref[...] * _softplus(ab + dtb_ref[...]))


def _gdn_dec_pre(x, b0, b1, b2, conv_w, ab, arow, dtb):
    n = x.shape[0]
    full = lambda a: pl.BlockSpec(a.shape, lambda i: (0,) * a.ndim)
    ins = (x, b0, b1, b2, conv_w, ab, arow, dtb)
    return pl.pallas_call(
        _gdn_dec_pre_kernel,
        out_shape=(jax.ShapeDtypeStruct((n, GDN_QKV), F32), jax.ShapeDtypeStruct((n, LANE), F32),
                   jax.ShapeDtypeStruct((n, LANE), F32)),
        grid=(1,),
        in_specs=[full(a) for a in ins],
        out_specs=(pl.BlockSpec((n, GDN_QKV), lambda i: (0, 0)), pl.BlockSpec((n, LANE), lambda i: (0, 0)),
                   pl.BlockSpec((n, LANE), lambda i: (0, 0))),
        compiler_params=_cparams(("arbitrary",)),
        name="gdn_dec_pre",
    )(*ins)


def _gdn_dec_kernel(qt_ref, kt_ref, v_ref, z_ref, beta_ref, eg_ref, nw_ref, s_ref, so_ref, y_ref):
    qt = qt_ref[0]
    kt = kt_ref[0]
    v = v_ref[0]
    z = z_ref[0]
    beta = beta_ref[0]
    eg = eg_ref[0]
    nw = nw_ref[...]
    heads = range(GDN_HEADS)
    s = [s_ref[0, h] * eg[:, GDN_HEADS + h:GDN_HEADS + h + 1] for h in heads]
    kcol = [kt[:, h:h + 1] for h in heads]
    v_old = [jnp.sum(s[h] * kcol[h], axis=0, keepdims=True) for h in heads]
    delta = [(v[h:h + 1, :] - v_old[h]) * beta[:, h:h + 1] for h in heads]
    s = [s[h] + kcol[h] * delta[h] for h in heads]
    o = [jnp.sum(s[h] * qt[:, h:h + 1], axis=0, keepdims=True) for h in heads]
    for h in heads:
        so_ref[0, h] = s[h]
        y = o[h] * lax.rsqrt(jnp.mean(o[h] * o[h], axis=-1, keepdims=True) + RMS_EPS) * nw
        y_ref[0, h:h + 1, :] = y * _silu(z[h:h + 1, :])


def _gdn_dec(qt, kt, v, z, beta, eg, norm_w, state, layer):
    n = qt.shape[0]
    tspec = pl.BlockSpec((1, GDN_DK, GDN_HEADS), lambda i: (i, 0, 0))
    hspec = pl.BlockSpec((1, GDN_HEADS, GDN_DV), lambda i: (i, 0, 0))
    rspec = pl.BlockSpec((1, 1, LANE), lambda i: (i, 0, 0))
    return pl.pallas_call(
        _gdn_dec_kernel,
        out_shape=(jax.ShapeDtypeStruct((n, GDN_HEADS, GDN_DK, GDN_DV), F32),
                   jax.ShapeDtypeStruct((n, GDN_HEADS, GDN_DV), F32)),
        grid=(n,),
        in_specs=[tspec, tspec, hspec, hspec, rspec, rspec,
                  pl.BlockSpec((1, GDN_DV), lambda i: (0, 0)),
                  pl.BlockSpec((None, 1, GDN_HEADS, GDN_DK, GDN_DV), lambda i: (layer, i, 0, 0, 0))],
        out_specs=(pl.BlockSpec((1, GDN_HEADS, GDN_DK, GDN_DV), lambda i: (i, 0, 0, 0)), hspec),
        compiler_params=_cparams(("parallel",)),
        name="gdn_dec",
    )(qt, kt, v, z, beta, eg, norm_w, state)


def _dec_attend(problems, n_g, scale):
    hq = problems[0][0].shape[0]
    row = lax.broadcasted_iota(jnp.int32, (hq, 1), 0)
    n_kv = len(problems[0][1])
    sel = [(row >= kvh * n_g) & (row < (kvh + 1) * n_g) for kvh in range(n_kv)]

    def pick(parts):
        out = parts[0]
        for kvh in range(1, n_kv):
            out = jnp.where(sel[kvh], parts[kvh], out)
        return out

    s_parts = [[_dot_nt_hi(q, kc[kvh]) * scale for kvh in range(n_kv)] for q, kc, *_ in problems]
    soft = []
    for (q, kc, vc, knew, vnew, bias_c, bias_n, sink), parts in zip(problems, s_parts):
        sc = pick(parts)
        kn = pick([jnp.broadcast_to(k1, q.shape) for k1 in knew])
        sn = jnp.sum(q * kn, axis=-1, keepdims=True) * scale
        if bias_c is not None:
            sc = sc + bias_c
            sn = sn + bias_n
        m = jnp.maximum(jnp.max(sc, axis=-1, keepdims=True), sn)
        if sink is not None:
            m = jnp.maximum(m, sink)
        pc = jnp.exp(sc - m)
        pn = jnp.exp(sn - m)
        den = jnp.sum(pc, axis=-1, keepdims=True) + pn
        if sink is not None:
            den = den + jnp.exp(sink - m)
        soft.append((pc, pn, den, m))
    o_parts = [[_dot_hi(pc, vc[kvh]) for kvh in range(n_kv)] for (q, kc, vc, *_), (pc, *_) in zip(problems, soft)]
    res = []
    for (q, kc, vc, knew, vnew, *_), (pc, pn, den, m), parts in zip(problems, soft, o_parts):
        vn = pick([jnp.broadcast_to(v1, q.shape) for v1 in vnew])
        res.append(((pick(parts) + pn * vn) / den, m + jnp.log(den)))
    return res


def _dil_dec_kernel(q_ref, kn_ref, vn_ref, g_ref, c0_ref, c1_ref, c2_ref, bc_ref, bn_ref, y_ref):
    q_all = q_ref[0]
    kn_all = kn_ref[0]
    vn_all = vn_ref[0]
    problems = []
    kvw = DIL_KVH * DIL_HD
    for gi, c_ref in enumerate((c0_ref, c1_ref, c2_ref)):
        q = q_all[gi * DIL_QH:(gi + 1) * DIL_QH]
        kc = [c_ref[:, kvh * DIL_HD:(kvh + 1) * DIL_HD] for kvh in range(DIL_KVH)]
        vc = [c_ref[:, kvw + kvh * DIL_HD:kvw + (kvh + 1) * DIL_HD] for kvh in range(DIL_KVH)]
        knew = [kn_all[gi * DIL_KVH + kvh:gi * DIL_KVH + kvh + 1] for kvh in range(DIL_KVH)]
        vnew = [vn_all[gi * DIL_KVH + kvh:gi * DIL_KVH + kvh + 1] for kvh in range(DIL_KVH)]
        problems.append((q, kc, vc, knew, vnew, bc_ref[gi], bn_ref[gi], None))
    res = _dec_attend(problems, DIL_G, DIL_HD ** -0.5)
    outs = [o for o, _ in res]
    lses = [l for _, l in res]
    m = jnp.maximum(jnp.maximum(lses[0], lses[1]), lses[2])
    ws = [jnp.exp(l - m) for l in lses]
    inv = 1.0 / (ws[0] + ws[1] + ws[2])
    o = (ws[0] * outs[0] + ws[1] * outs[1] + ws[2] * outs[2]) * inv
    y_ref[0] = o * _silu(g_ref[0])


def _dil_dec(q3, kn3, vn3, g3, views, layer, bias_c, bias_n):
    n = q3.shape[0]
    cspecs = [pl.BlockSpec((None, None) + v.shape[2:], lambda i: (layer, i, 0, 0)) for v in views]
    return pl.pallas_call(
        _dil_dec_kernel,
        out_shape=jax.ShapeDtypeStruct((n, DIL_QH, DIL_HD), F32),
        grid=(n,),
        in_specs=[pl.BlockSpec((1, N_DIL * DIL_QH, DIL_HD), lambda i: (i, 0, 0)),
                  pl.BlockSpec((1, N_DIL * DIL_KVH, DIL_HD), lambda i: (i, 0, 0)),
                  pl.BlockSpec((1, N_DIL * DIL_KVH, DIL_HD), lambda i: (i, 0, 0)),
                  pl.BlockSpec((1, DIL_QH, DIL_HD), lambda i: (i, 0, 0))] + cspecs +
                 [pl.BlockSpec(bias_c.shape, lambda i: (0, 0, 0)),
                  pl.BlockSpec(bias_n.shape, lambda i: (0, 0, 0))],
        out_specs=pl.BlockSpec((1, DIL_QH, DIL_HD), lambda i: (i, 0, 0)),
        compiler_params=_cparams(("parallel",)),
        name="dil_dec",
    )(q3, kn3, vn3, g3, *views, bias_c, bias_n)


def _swa_dec_kernel(q_ref, kn_ref, vn_ref, g_ref, c_ref, sink_ref, cos_ref, sin_ref, perm_ref, y_ref, rk_ref):
    cos = cos_ref[...]
    sin = sin_ref[...]
    perm = perm_ref[...]
    q = q_ref[0]
    kn = kn_ref[0]
    q = q * cos + _dot_hi(q, perm) * sin
    kn = kn * cos + _dot_hi(kn, perm) * sin
    rk_ref[0] = kn
    vn = vn_ref[0]
    kvw = SWA_KVH * SWA_HD
    kc = [c_ref[:, kvh * SWA_HD:(kvh + 1) * SWA_HD] for kvh in range(SWA_KVH)]
    vc = [c_ref[:, kvw + kvh * SWA_HD:kvw + (kvh + 1) * SWA_HD] for kvh in range(SWA_KVH)]
    knew = [kn[kvh:kvh + 1] for kvh in range(SWA_KVH)]
    vnew = [vn[kvh:kvh + 1] for kvh in range(SWA_KVH)]
    (o, _), = _dec_attend([(q, kc, vc, knew, vnew, None, None, sink_ref[...])], SWA_G, SWA_HD ** -0.5)
    y_ref[0] = o * _silu(g_ref[0])


def _swa_dec(q3, kn3, vn3, g3, view, layer, sink_col, cos_d, sin_d, perm):
    n = q3.shape[0]
    win, kvw2 = view.shape[2:]
    return pl.pallas_call(
        _swa_dec_kernel,
        out_shape=(jax.ShapeDtypeStruct((n, SWA_QH, SWA_HD), F32),
                   jax.ShapeDtypeStruct((n, SWA_KVH, SWA_HD), F32)),
        grid=(n,),
        in_specs=[pl.BlockSpec((1, SWA_QH, SWA_HD), lambda i: (i, 0, 0)),
                  pl.BlockSpec((1, SWA_KVH, SWA_HD), lambda i: (i, 0, 0)),
                  pl.BlockSpec((1, SWA_KVH, SWA_HD), lambda i: (i, 0, 0)),
                  pl.BlockSpec((1, SWA_QH, SWA_HD), lambda i: (i, 0, 0)),
                  pl.BlockSpec((None, None, win, kvw2), lambda i: (layer, i, 0, 0)),
                  pl.BlockSpec(sink_col.shape, lambda i: (0, 0)),
                  pl.BlockSpec(cos_d.shape, lambda i: (0, 0)),
                  pl.BlockSpec(sin_d.shape, lambda i: (0, 0)),
                  pl.BlockSpec(perm.shape, lambda i: (0, 0))],
        out_specs=(pl.BlockSpec((1, SWA_QH, SWA_HD), lambda i: (i, 0, 0)),
                   pl.BlockSpec((1, SWA_KVH, SWA_HD), lambda i: (i, 0, 0))),
        compiler_params=_cparams(("parallel",)),
        name="swa_dec",
    )(q3, kn3, vn3, g3, view, sink_col, cos_d, sin_d, perm)


def _lru_dec_kernel(x_ref, g_ref, b0_ref, b1_ref, b2_ref, h0_ref, cw_ref, cb_ref, wa_ref, wx_ref,
                    ba_ref, bx_ref, lam_ref, y_ref, h_ref):
    cw = cw_ref[...]
    cx = (b0_ref[...] * cw[0:1] + b1_ref[...] * cw[1:2] + b2_ref[...] * cw[2:3] + x_ref[...] * cw[3:4]
          + cb_ref[...])
    a, bterm = _lru_gates(cx, wa_ref, wx_ref, ba_ref[...], bx_ref[...], lam_ref[...])
    h = a * h0_ref[...] + bterm
    h_ref[...] = h
    y_ref[...] = h * _silu(g_ref[...])


def _lru_dec(x, g, b0, b1, b2, h0, conv_w, conv_b, wa, wx, ba, bx, lam):
    n = x.shape[0]
    ins = (x, g, b0, b1, b2, h0, conv_w, conv_b, wa, wx, ba, bx, lam)
    full = lambda a: pl.BlockSpec(a.shape, lambda i: (0,) * a.ndim)
    return pl.pallas_call(
        _lru_dec_kernel,
        out_shape=(jax.ShapeDtypeStruct((n, LRU_W), F32), jax.ShapeDtypeStruct((n, LRU_W), F32)),
        grid=(1,),
        in_specs=[full(a) for a in ins],
        out_specs=(pl.BlockSpec((n, LRU_W), lambda i: (0, 0)), pl.BlockSpec((n, LRU_W), lambda i: (0, 0))),
        compiler_params=_cparams(("arbitrary",)),
        name="lru_dec",
    )(*ins)


def _rel_bucket(dist):
    max_exact = REL_BUCKETS // 2
    n = dist.astype(F32)
    large = max_exact + (jnp.log(jnp.maximum(n, 1.0) / max_exact) / math.log(REL_MAX_DIST / max_exact)
                         * (REL_BUCKETS - max_exact)).astype(jnp.int32)
    large = jnp.minimum(large, REL_BUCKETS - 1)
    return jnp.where(dist < max_exact, dist, large)


def _offset_bias(rel_bias, gi, win, dil):
    j = win // dil + 1
    b = rel_bias[_rel_bucket(dil * jnp.arange(j, dtype=jnp.int32))]
    return b[:, gi * DIL_QH:(gi + 1) * DIL_QH].astype(F32)


def _rope_tables(pos):
    half = SWA_HD // 2
    inv = ROPE_THETA ** (-jnp.arange(half, dtype=F32) / half)
    ang = pos.astype(F32)[:, None] * inv[None, :]
    c, s = jnp.cos(ang), jnp.sin(ang)
    return jnp.concatenate([c, c], axis=1), jnp.concatenate([-s, s], axis=1)


SUBLANES = 8


def _shift_kernel(c_ref, n_ref, o_ref, *, k):
    rows = c_ref.shape[0]
    shifted = pltpu.roll(c_ref[...], rows - k, 0)
    row8 = lax.broadcasted_iota(jnp.int32, (SUBLANES, LANE), 0)
    o_ref[0:rows - SUBLANES, :] = shifted[0:rows - SUBLANES]
    o_ref[rows - SUBLANES:rows, :] = jnp.where(row8 >= SUBLANES - k, n_ref[...], shifted[rows - SUBLANES:rows])


def _shift_append(cache, new_rows):
    d, n, w = cache.shape[:3]
    k = int(np.prod(cache.shape[3:])) // LANE
    assert k * LANE == int(np.prod(cache.shape[3:])) and k < SUBLANES
    new8 = jnp.concatenate([jnp.zeros((d, n, SUBLANES - k, LANE), cache.dtype), new_rows.reshape(d, n, k, LANE)],
                           axis=2)
    out = pl.pallas_call(
        functools.partial(_shift_kernel, k=k),
        out_shape=jax.ShapeDtypeStruct((d, n, w * k, LANE), cache.dtype),
        grid=(d, n),
        in_specs=[pl.BlockSpec((None, None, w * k, LANE), lambda a, b: (a, b, 0, 0)),
                  pl.BlockSpec((None, None, SUBLANES, LANE), lambda a, b: (a, b, 0, 0))],
        out_specs=pl.BlockSpec((None, None, w * k, LANE), lambda a, b: (a, b, 0, 0)),
        compiler_params=_cparams(("parallel", "parallel")),
        name="shift_append",
    )(cache.reshape(d, n, w * k, LANE), new8)
    return out.reshape(cache.shape)


def _cols(h, unit, width):
    return h[..., unit * LANE:unit * LANE + width]


def kernel(x_prompt, x_sample, state_gdn, state_gdn_conv, cache_dil_w128, cache_dil_w512, cache_dil_w2048,
           cache_swa, state_rglru, state_rglru_conv, w_in, gdn_conv_w, gdn_a_log, gdn_dt_bias, gdn_norm_w,
           lru_conv_w, lru_conv_b, lru_wa, lru_ba, lru_wx, lru_bx, lru_lambda, swa_sink, rel_bias, w_branch,
           w_out, ln_g, ln_b):
    nb, t_len, _ = x_prompt.shape
    ns = x_sample.shape[0]
    mp = nb * t_len
    caches = (cache_dil_w128, cache_dil_w512, cache_dil_w2048)
    xp = x_prompt.reshape(mp, D_MODEL)
    xs = x_sample.reshape(ns, D_MODEL)

    cos64, sin64 = _rope_tables(jnp.arange(t_len))
    cos_t = jnp.concatenate([cos64, cos64], axis=1)
    sin_t = jnp.concatenate([sin64, sin64], axis=1)
    cos_d, sin_d = _rope_tables(jnp.full((1,), PAST_LEN))
    perm = jnp.asarray(np.roll(np.eye(SWA_HD, dtype=np.float32), SWA_HD // 2, axis=0))
    qi = jnp.arange(BLOCK)[:, None]
    kj = jnp.arange(2 * BLOCK)[None, :]
    bias_p, bias_c, bias_n = [], [], []
    for gi, (win, dil) in enumerate(DIL_GROUPS):
        ob = _offset_bias(rel_bias, gi, win, dil)
        jw = win // dil
        onehot = (jnp.clip(qi + BLOCK - kj, 0, jw)[:, :, None] == jnp.arange(jw + 1)).astype(F32)
        bias_p.append(jnp.einsum('qkj,jh->hqk', onehot, ob, precision=lax.Precision.HIGHEST))
        bias_c.append(jnp.transpose(ob[::-1][:jw], (1, 0)))
        bias_n.append(ob[0][:, None])
    bias_c = jnp.stack(bias_c)
    bias_n = jnp.stack(bias_n)
    dil_views = [c[:, :, ::dil].reshape(DEPTH, ns, win // dil, 2 * DIL_KVH * DIL_HD)
                 for (win, dil), c in zip(DIL_GROUPS, caches)]
    swa_view = cache_swa.reshape(DEPTH, ns, cache_swa.shape[2], 2 * SWA_KVH * SWA_HD)
    wt = jnp.swapaxes(w_in, 1, 2)
    xp_b, xs_b = xp.astype(BF16), xs.astype(BF16)

    new_p = [[] for _ in range(8)]
    new_s = [[] for _ in range(8)]
    for l in range(DEPTH):
        wb = w_branch[l]
        wo = w_out[l]
        wa = lru_wa[l].astype(BF16)
        wx = lru_wx[l].astype(BF16)
        zpad = jnp.zeros((1, LANE - 2 * GDN_HEADS), F32)
        arow = jnp.concatenate([jnp.zeros((1, GDN_HEADS), F32), jnp.exp(gdn_a_log[l])[None], zpad], axis=1)
        dtb = jnp.concatenate([jnp.zeros((1, GDN_HEADS), F32), gdn_dt_bias[l][None], zpad], axis=1)
        norm_w = gdn_norm_w[l][None]
        conv_b = lru_conv_b[l][None]
        ba, bx, lam = lru_ba[l][None], lru_bx[l][None], lru_lambda[l][None]
        lng, lnb = ln_g[l][None], ln_b[l][None]
        sink = swa_sink[l].astype(F32)

        hp = _in_proj(xp_b, wt, l, 2048)
        ht = _in_proj_tail(xp_b, wt, l, 2048)
        hp3 = hp.reshape(nb, t_len, D_INP)
        ht3 = ht.reshape(nb, t_len, D_TAIL)
        ya, s_p = _gdn_prompt(hp, ht, nb, t_len, gdn_conv_w[l], arow, dtb, norm_w)
        outs, lses = [], []
        for gi in range(N_DIL):
            o_g, l_g = _dil_attn(hp, gi, nb, t_len, bias_p[gi])
            outs.append(o_g)
            lses.append(l_g)
        yb = _dil_merge(outs[0], outs[1], outs[2], lses[0], lses[1], lses[2], hp, 512)
        yc, h_p = _lru_prompt(hp, nb, t_len, lru_conv_w[l], conv_b, wa, wx, ba, bx, lam, 512)
        rq, rk = _rope_prompt(hp, ht, cos_t, sin_t, t_len, 512)
        sink_row = jnp.concatenate([sink[None], jnp.zeros((1, LANE - SWA_QH), F32)], axis=1)
        yd = _swa_attn(rq, rk, hp, ht, sink_row, nb, t_len)
        merged = _branch_merge((ya, yb, yc, yd), hp, wb, 512, 512)
        xp_new, xp_b_new = _out_proj(merged, xp, wo, lng, lnb, 512)

        new_p[0].append(s_p)
        new_p[1].append(_cols(hp3, U_AQKV, GDN_QKV)[:, t_len - (CONV_W - 1):])
        for gi, (win, dil) in enumerate(DIL_GROUPS):
            kk = _cols(hp3, U_BK + gi * DIL_KVH, DIL_KVH * DIL_HD)[:, t_len - win:]
            vv = _cols(hp3, U_BV + gi * DIL_KVH, DIL_KVH * DIL_HD)[:, t_len - win:]
            new_p[2 + gi].append(jnp.stack([kk, vv], axis=2).reshape(nb, win, 2, DIL_KVH, DIL_HD))
        kk = rk.reshape(nb, t_len, LANE)[:, t_len - SWA_WINDOW:]
        vv = _cols(ht3, T_DV, LANE)[:, t_len - SWA_WINDOW:]
        new_p[5].append(jnp.stack([kk, vv], axis=2).reshape(nb, SWA_WINDOW, 2, SWA_KVH, SWA_HD))
        new_p[6].append(h_p.reshape(nb, LRU_W))
        new_p[7].append(_cols(hp3, U_CX, LRU_W)[:, t_len - (CONV_W - 1):])

        hs = _in_proj(xs_b, wt, l, ns)
        hst = _in_proj_tail(xs_b, wt, l, ns)
        gbuf = state_gdn_conv[l]
        a_qkv_s = _cols(hs, U_AQKV, GDN_QKV)
        qkv_n, beta_s, eg_s = _gdn_dec_pre(a_qkv_s, gbuf[:, 0], gbuf[:, 1], gbuf[:, 2], gdn_conv_w[l],
                                           _cols(hst, T_AB, LANE), arow, dtb)
        qkv4 = qkv_n.reshape(ns, 3, GDN_HEADS, GDN_DK)
        s_s, ya_s = _gdn_dec(jnp.swapaxes(qkv4[:, 0], 1, 2), jnp.swapaxes(qkv4[:, 1], 1, 2), qkv4[:, 2],
                             _cols(hs, U_AZ, BRANCH_W).reshape(ns, GDN_HEADS, GDN_DV),
                             beta_s[:, None], eg_s[:, None], norm_w, state_gdn, l)
        bk_s = _cols(hs, U_BK, N_DIL * DIL_KVH * DIL_HD)
        bv_s = _cols(hs, U_BV, N_DIL * DIL_KVH * DIL_HD)
        yb_s = _dil_dec(_cols(hs, U_BQ, N_DIL * BRANCH_W).reshape(ns, N_DIL * DIL_QH, DIL_HD),
                        bk_s.reshape(ns, N_DIL * DIL_KVH, DIL_HD), bv_s.reshape(ns, N_DIL * DIL_KVH, DIL_HD),
                        _cols(hs, U_BG, BRANCH_W).reshape(ns, DIL_QH, DIL_HD), dil_views, l, bias_c, bias_n)
        lbuf = state_rglru_conv[l]
        cx_s = _cols(hs, U_CX, LRU_W)
        yc_s, h_s = _lru_dec(cx_s, _cols(hs, U_CG, LRU_W), lbuf[:, 0], lbuf[:, 1], lbuf[:, 2], state_rglru[l],
                             lru_conv_w[l], conv_b, wa, wx, ba, bx, lam)
        dv_s = _cols(hst, T_DV, LANE)
        yd_s, rk_s = _swa_dec(_cols(hs, U_DQ, BRANCH_W).reshape(ns, SWA_QH, SWA_HD),
                              _cols(hst, T_DK, LANE).reshape(ns, SWA_KVH, SWA_HD),
                              dv_s.reshape(ns, SWA_KVH, SWA_HD),
                              _cols(hs, U_DG, BRANCH_W).reshape(ns, SWA_QH, SWA_HD),
                              swa_view, l, sink.reshape(SWA_QH, 1), cos_d, sin_d, perm)
        ys_s = (ya_s.reshape(ns, BRANCH_W).astype(BF16), yb_s.reshape(ns, BRANCH_W).astype(BF16),
                yc_s.astype(BF16), yd_s.reshape(ns, BRANCH_W).astype(BF16))
        merged_s = _branch_merge(ys_s, hs, wb, ns, 512)
        xs_new, xs_b_new = _out_proj(merged_s, xs, wo, lng, lnb, ns)

        new_s[0].append(s_s)
        new_s[1].append(jnp.concatenate([gbuf[:, 1:], a_qkv_s[:, None]], axis=1))
        for gi in range(N_DIL):
            kk = bk_s[:, gi * DIL_KVH * DIL_HD:(gi + 1) * DIL_KVH * DIL_HD].reshape(ns, 1, DIL_KVH, DIL_HD)
            vv = bv_s[:, gi * DIL_KVH * DIL_HD:(gi + 1) * DIL_KVH * DIL_HD].reshape(ns, 1, DIL_KVH, DIL_HD)
            new_s[2 + gi].append(jnp.stack([kk, vv], axis=2))
        new_s[5].append(jnp.stack([rk_s[:, None], dv_s.reshape(ns, 1, SWA_KVH, SWA_HD)], axis=2))
        new_s[6].append(h_s)
        new_s[7].append(jnp.concatenate([lbuf[:, 1:], cx_s[:, None]], axis=1))

        xp, xs, xp_b, xs_b = xp_new, xs_new, xp_b_new, xs_b_new

    p = [jnp.stack(v) for v in new_p]
    s = [jnp.stack(new_s[i]) for i in (0, 1)]
    for i, c in zip((2, 3, 4, 5), caches + (cache_swa,)):
        s.append(_shift_append(c, jnp.stack(new_s[i])))
    s += [jnp.stack(new_s[i]) for i in (6, 7)]
    return (xp.reshape(nb, t_len, D_MODEL), xs.reshape(ns, 1, D_MODEL),
            p[0], s[0], p[1], s[1], p[2], s[2], p[3], s[3], p[4], s[4], p[5], s[5], p[6], s[6], p[7], s[7])
```

```python
import functools
import math

import jax
import jax.numpy as jnp
import numpy as np
from jax import lax
from jax.experimental import pallas as pl
from jax.experimental.pallas import tpu as pltpu

F32 = jnp.float32
BF16 = jnp.bfloat16
NEG_INF = -1e30

D_MODEL = 2048
DEPTH = 2
PAST_LEN = 16384
BRANCH_W = D_MODEL // 2
N_BRANCH = 4
CONV_W = 4
BLOCK = 128
GDN_DK = 128
GDN_DV = 128
GDN_HEADS = BRANCH_W // GDN_DV
GDN_QKV = GDN_HEADS * (2 * GDN_DK + GDN_DV)
DIL_GROUPS = ((128, 1), (512, 4), (2048, 16))
N_DIL = len(DIL_GROUPS)
DIL_HD = 128
DIL_QH = BRANCH_W // DIL_HD
DIL_KVH = 2
DIL_G = DIL_QH // DIL_KVH
LRU_W = BRANCH_W
LRU_BLOCKS = 8
LRU_BS = LRU_W // LRU_BLOCKS
LRU_C = 8.0
SWA_HD = 64
SWA_QH = BRANCH_W // SWA_HD
SWA_KVH = 2
SWA_G = SWA_QH // SWA_KVH
SWA_WINDOW = 128
ROPE_THETA = 150000.0
REL_BUCKETS = 32
REL_MAX_DIST = 2048
LN_EPS = 1e-5
RMS_EPS = 1e-6
DN_ALPHA = (2.0 * DEPTH) ** 0.25

IN_SIZES = (GDN_QKV, GDN_HEADS * GDN_DV, GDN_HEADS, GDN_HEADS,
            N_DIL * DIL_QH * DIL_HD, N_DIL * DIL_KVH * DIL_HD, N_DIL * DIL_KVH * DIL_HD, DIL_QH * DIL_HD,
            LRU_W, LRU_W,
            SWA_QH * SWA_HD, SWA_KVH * SWA_HD, SWA_KVH * SWA_HD, SWA_QH * SWA_HD,
            N_BRANCH * D_MODEL)
_OFF = [0]
for _s in IN_SIZES:
    _OFF.append(_OFF[-1] + _s)
(O_AQKV, O_AZ, O_AB, O_AA, O_BQ, O_BK, O_BV, O_BG, O_CX, O_CG, O_DQ, O_DK, O_DV, O_DG, O_MG, O_END) = _OFF

LANE = 128
U_MG, U_AZ, U_AQKV, U_BQ, U_BG, U_CX, U_CG, U_DQ, U_DG, U_BK, U_BV = (
    0, 64, 72, 96, 120, 128, 136, 144, 152, 160, 166)
N_UNITS = 172
D_INP = N_UNITS * LANE
T_DK, T_DV, T_AB = 0, 1, 2
D_TAIL = 3 * LANE
IN_TN = 512
_TILE_SRC = tuple(seg0 + IN_TN * t for seg0, width in (
    (O_MG, O_END - O_MG), (O_AZ, O_AB - O_AZ), (O_AQKV, O_AZ - O_AQKV), (O_BQ, O_BK - O_BQ), (O_BG, O_CX - O_BG),
    (O_CX, O_CG - O_CX), (O_CG, O_DQ - O_CG), (O_DQ, O_DK - O_DQ), (O_DG, O_MG - O_DG), (O_BK, O_BG - O_BK))
    for t in range(width // IN_TN))
_SRC_ALIGN = 2 * GDN_HEADS
assert len(_TILE_SRC) * IN_TN == D_INP and all(s % _SRC_ALIGN == 0 for s in _TILE_SRC)
GDN_CHUNK = 128
VMEM_LIMIT = 56 * 1024 * 1024


def _cparams(sem):
    return pltpu.CompilerParams(dimension_semantics=sem, vmem_limit_bytes=VMEM_LIMIT)


def _sigmoid(x):
    return 0.5 * jnp.tanh(0.5 * x) + 0.5


def _silu(x):
    return x * _sigmoid(x)


def _softplus(x):
    return jnp.maximum(x, 0.0) + jnp.log1p(jnp.exp(-jnp.abs(x)))


def _dot(a, b):
    return jnp.dot(a.astype(BF16), b.astype(BF16), preferred_element_type=F32)


def _dot_nt(a, b):
    return lax.dot_general(a.astype(BF16), b.astype(BF16), (((1,), (1,)), ((), ())),
                           preferred_element_type=F32)


def _dot_hi(a, b):
    return jnp.dot(a, b, preferred_element_type=F32, precision=lax.Precision.HIGHEST)


def _split2(a):
    hi = a.astype(BF16)
    return hi, (a - hi.astype(F32)).astype(BF16)


def _dot3(a, b):
    ah, al = _split2(a)
    bh, bl = _split2(b)
    return (jnp.dot(ah, bh, preferred_element_type=F32) + jnp.dot(ah, bl, preferred_element_type=F32)
            + jnp.dot(al, bh, preferred_element_type=F32))


def _dot_exact_lhs(a, b):
    a = a.astype(BF16)
    b0 = b.astype(BF16)
    r1 = b - b0.astype(F32)
    b1 = r1.astype(BF16)
    b2 = (r1 - b1.astype(F32)).astype(BF16)
    return (jnp.dot(a, b0, preferred_element_type=F32) + jnp.dot(a, b1, preferred_element_type=F32)
            + jnp.dot(a, b2, preferred_element_type=F32))


def _dot_nt_hi(a, b):
    return lax.dot_general(a, b, (((1,), (1,)), ((), ())), preferred_element_type=F32,
                           precision=lax.Precision.HIGHEST)


def _in_proj_kernel(tab_ref, x_ref, w_ref, o_ref):
    del tab_ref
    o_ref[...] = _dot_nt(x_ref[...], w_ref[0])


def _in_proj(x, wt, layer, tm):
    m, k = x.shape
    grid_spec = pltpu.PrefetchScalarGridSpec(
        num_scalar_prefetch=1,
        grid=(m // tm, D_INP // IN_TN),
        in_specs=[pl.BlockSpec((tm, k), lambda i, j, tab: (i, 0)),
                  pl.BlockSpec((pl.Element(1), pl.Element(IN_TN), pl.Element(k)),
                               lambda i, j, tab: (layer, tab[j] * _SRC_ALIGN, 0))],
        out_specs=pl.BlockSpec((tm, IN_TN), lambda i, j, tab: (i, j)))
    return pl.pallas_call(
        _in_proj_kernel,
        out_shape=jax.ShapeDtypeStruct((m, D_INP), F32),
        grid_spec=grid_spec,
        compiler_params=_cparams(("parallel", "arbitrary")),
        name="in_proj",
    )(jnp.asarray([s // _SRC_ALIGN for s in _TILE_SRC], jnp.int32), x, wt)


def _in_tail_kernel(x_ref, wkv_ref, wab_ref, o_ref):
    x = x_ref[...]
    o_ref[:, 0:2 * LANE] = _dot_nt(x, wkv_ref[0])
    o_ref[:, 2 * LANE:] = _dot_nt(x, wab_ref[0])


def _in_proj_tail(x, wt, layer, tm):
    m, k = x.shape
    return pl.pallas_call(
        _in_tail_kernel,
        out_shape=jax.ShapeDtypeStruct((m, D_TAIL), F32),
        grid=(m // tm,),
        in_specs=[pl.BlockSpec((tm, k), lambda i: (i, 0)),
                  pl.BlockSpec((pl.Element(1), pl.Element(2 * LANE), pl.Element(k)), lambda i: (layer, O_DK, 0)),
                  pl.BlockSpec((pl.Element(1), pl.Element(LANE), pl.Element(k)), lambda i: (layer, O_AB, 0))],
        out_specs=pl.BlockSpec((tm, D_TAIL), lambda i: (i, 0)),
        compiler_params=_cparams(("parallel",)),
        name="in_proj_tail",
    )(x, wt, wt)


def _conv_rows(x, prev8, cw):
    row8 = lax.broadcasted_iota(jnp.int32, (8, x.shape[1]), 0)
    y = x * cw[CONV_W - 1:CONV_W]
    for k in range(1, CONV_W):
        xk = pltpu.roll(x, k, 0)
        fk = pltpu.roll(prev8, k, 0)
        head = jnp.where(row8 < k, fk, xk[0:8])
        xk = jnp.concatenate([head, xk[8:]], axis=0)
        y = y + xk * cw[CONV_W - 1 - k:CONV_W - k]
    return y


def _tri_inv(lmat, ri, ci):
    c = lmat[0].shape[0]
    eye = (ri == ci).astype(F32)
    blk = (ri >> 4) == (ci >> 4)
    d = [jnp.where(blk, l, 0.0) for l in lmat]
    x = [eye - di for di in d]
    p = [_dot(di, di) for di in d]
    x = [xi + _dot(xi, pi) for xi, pi in zip(x, p)]
    p = [_dot(pi, pi) for pi in p]
    x = [xi + _dot(xi, pi) for xi, pi in zip(x, p)]
    p = [_dot(pi, pi) for pi in p]
    x = [xi + _dot(xi, pi) for xi, pi in zip(x, p)]
    sh = 4
    while (1 << sh) < c:
        lower_left = ((ri >> (sh + 1)) == (ci >> (sh + 1))) & (((ri >> sh) & 1) == 1) & (((ci >> sh) & 1) == 0)
        t = [_dot(jnp.where(lower_left, l, 0.0), xi) for l, xi in zip(lmat, x)]
        x = [xi - _dot(xi, ti) for xi, ti in zip(x, t)]
        sh += 1
    return x


def _tri_solve(lmat, rhs, ri, ci):
    tinv = _tri_inv(lmat, ri, ci)
    sol = [_dot(t, r) for t, r in zip(tinv, rhs)]
    resid = [r - s - _dot3(l, s) for r, s, l in zip(rhs, sol, lmat)]
    return [s + _dot(t, r) for s, t, r in zip(sol, tinv, resid)]


def _gdn_kernel(qkv_ref, z_ref, ab_ref, cw_ref, arow_ref, dtb_ref, nw_ref, y_ref, s_ref, prev_ref):
    cidx = pl.program_id(1)
    C = GDN_CHUNK

    @pl.when(cidx == 0)
    def _():
        s_ref[...] = jnp.zeros_like(s_ref)
        prev_ref[...] = jnp.zeros_like(prev_ref)

    x = qkv_ref[...]
    act = _silu(_conv_rows(x, prev_ref[...], cw_ref[...]))
    prev_ref[...] = x[C - 8:C]

    ab = ab_ref[...]
    beta_t = _sigmoid(ab)
    g_t = -arow_ref[...] * _softplus(ab + dtb_ref[...])
    ri = lax.broadcasted_iota(jnp.int32, (C, C), 0)
    ci = lax.broadcasted_iota(jnp.int32, (C, C), 1)
    incl = ri >= ci
    strict = ri > ci
    gc_t = _dot_exact_lhs(incl.astype(F32), g_t)
    gc_tt = gc_t.T
    egc_t = jnp.exp(gc_t)
    nw = nw_ref[...]
    z = z_ref[...]

    heads = range(GDN_HEADS)
    q, k, v, beta, gcc, egc, decay, glc = [], [], [], [], [], [], [], []
    for h in heads:
        qh = act[:, h * GDN_DK:(h + 1) * GDN_DK]
        kh = act[:, (GDN_HEADS + h) * GDN_DK:(GDN_HEADS + h + 1) * GDN_DK]
        q.append(qh * lax.rsqrt(jnp.sum(qh * qh, axis=-1, keepdims=True) + 1e-6) * (GDN_DK ** -0.5))
        k.append(kh * lax.rsqrt(jnp.sum(kh * kh, axis=-1, keepdims=True) + 1e-6))
        v.append(act[:, (2 * GDN_HEADS + h) * GDN_DK:(2 * GDN_HEADS + h + 1) * GDN_DK])
        beta.append(beta_t[:, h:h + 1])
        gcc.append(gc_t[:, GDN_HEADS + h:GDN_HEADS + h + 1])
        gcr = gc_tt[GDN_HEADS + h:GDN_HEADS + h + 1, :]
        egc.append(egc_t[:, GDN_HEADS + h:GDN_HEADS + h + 1])
        glc.append(gc_t[C - 1:C, GDN_HEADS + h:GDN_HEADS + h + 1])
        decay.append(jnp.where(incl, jnp.exp(jnp.where(incl, gcc[h] - gcr, 0.0)), 0.0))
    kb = [k[h] * beta[h] for h in heads]
    lmat = [jnp.where(strict, _dot_nt(kb[h], k[h]) * decay[h], 0.0) for h in heads]
    rhs = [jnp.concatenate([v[h] * beta[h], kb[h] * egc[h]], axis=1) for h in heads]
    sol = _tri_solve(lmat, rhs, ri, ci)
    qk = [_dot_nt(q[h], k[h]) * decay[h] for h in heads]
    s = [s_ref[0, h] for h in heads]
    v_new = [sol[h][:, :GDN_DV] - _dot(sol[h][:, GDN_DV:], s[h]) for h in heads]
    o = [_dot(q[h] * egc[h], s[h]) + _dot(qk[h], v_new[h]) for h in heads]
    for h in heads:
        kd = k[h] * jnp.exp(glc[h] - gcc[h])
        s_ref[0, h] = s[h] * jnp.exp(glc[h]) + _dot(kd.T, v_new[h])
    for h in heads:
        y = o[h] * lax.rsqrt(jnp.mean(o[h] * o[h], axis=-1, keepdims=True) + RMS_EPS) * nw
        y = y * _silu(z[:, h * GDN_DV:(h + 1) * GDN_DV])
        y_ref[:, h * GDN_DV:(h + 1) * GDN_DV] = y.astype(BF16)


def _gdn_prompt(hp, ht, n_seq, t_len, conv_w, arow, dtb, norm_w):
    C = GDN_CHUNK
    nc = t_len // C
    return pl.pallas_call(
        _gdn_kernel,
        out_shape=(jax.ShapeDtypeStruct((n_seq * t_len, BRANCH_W), BF16),
                   jax.ShapeDtypeStruct((n_seq, GDN_HEADS, GDN_DK, GDN_DV), F32)),
        grid=(n_seq, nc),
        in_specs=[pl.BlockSpec((C, GDN_QKV), lambda n, c: (n * nc + c, U_AQKV * LANE // GDN_QKV)),
                  pl.BlockSpec((C, BRANCH_W), lambda n, c: (n * nc + c, U_AZ * LANE // BRANCH_W)),
                  pl.BlockSpec((C, LANE), lambda n, c: (n * nc + c, T_AB)),
                  pl.BlockSpec((CONV_W, GDN_QKV), lambda n, c: (0, 0)),
                  pl.BlockSpec((1, LANE), lambda n, c: (0, 0)),
                  pl.BlockSpec((1, LANE), lambda n, c: (0, 0)),
                  pl.BlockSpec((1, GDN_DV), lambda n, c: (0, 0))],
        out_specs=(pl.BlockSpec((C, BRANCH_W), lambda n, c: (n * nc + c, 0)),
                   pl.BlockSpec((1, GDN_HEADS, GDN_DK, GDN_DV), lambda n, c: (n, 0, 0, 0))),
        scratch_shapes=[pltpu.VMEM((8, GDN_QKV), F32)],
        compiler_params=_cparams(("parallel", "arbitrary")),
        name="gdn_prompt",
    )(hp, hp, ht, conv_w, arow, dtb, norm_w)


def _band_mask(rows, cols, window, first_block):
    qi = lax.broadcasted_iota(jnp.int32, (rows, cols), 0) & (BLOCK - 1)
    kj = lax.broadcasted_iota(jnp.int32, (rows, cols), 1) & (2 * BLOCK - 1)
    off = qi + BLOCK - kj
    kmin = jnp.where(first_block, BLOCK, 0)
    return (off >= 0) & (off <= window) & (kj >= kmin)


def _dil_attn_kernel(*refs, window, dil, n_kv):
    hd, n_g = DIL_HD, DIL_G
    nq = n_kv * n_g
    q_refs = refs[:nq]
    kv_refs = refs[nq:nq + 4 * n_kv]
    bias_ref, o_ref, lse_ref = refs[nq + 4 * n_kv:nq + 4 * n_kv + 3]
    o_scr = refs[nq + 4 * n_kv + 3:]
    rows = n_g * BLOCK
    valid = _band_mask(rows, 2 * BLOCK, window, pl.program_id(1) == 0)
    lane = lax.broadcasted_iota(jnp.int32, (BLOCK, LANE), 1)

    def sub(r):
        return pl.ds(r, BLOCK, stride=dil) if dil > 1 else slice(None)

    bias_m = [jnp.where(valid, bias_ref[j * n_g:(j + 1) * n_g].reshape(rows, 2 * BLOCK), NEG_INF)
              for j in range(n_kv)]
    chains = [(j, r) for r in range(dil) for j in range(n_kv)]
    for c0 in range(0, len(chains), 2):
        grp = chains[c0:c0 + 2]
        s = []
        for j, r in grp:
            kc_ref, kp_ref = kv_refs[4 * j], kv_refs[4 * j + 1]
            qh = jnp.concatenate([q_refs[j * n_g + g][sub(r), :] for g in range(n_g)], axis=0)
            kcat = jnp.concatenate([kp_ref[sub(r), :], kc_ref[sub(r), :]], axis=0)
            s.append(_dot_nt(qh, kcat) * (hd ** -0.5) + bias_m[j])
        m = [jnp.max(sc, axis=-1, keepdims=True) for sc in s]
        p = [jnp.exp(sc - mc) for sc, mc in zip(s, m)]
        den = [jnp.sum(pc, axis=-1, keepdims=True) for pc in p]
        for (j, r), pc, mc, dc in zip(grp, p, m, den):
            vc_ref, vp_ref = kv_refs[4 * j + 2], kv_refs[4 * j + 3]
            vcat = jnp.concatenate([vp_ref[sub(r), :], vc_ref[sub(r), :]], axis=0)
            o = _dot(pc, vcat) / dc
            lse = mc + jnp.log(dc)
            lse_t = jnp.zeros((BLOCK, LANE), F32)
            for g in range(n_g):
                o_scr[j * n_g + g][sub(r), :] = o[g * BLOCK:(g + 1) * BLOCK]
                lse_t = jnp.where(lane == g, lse[g * BLOCK:(g + 1) * BLOCK], lse_t)
            lse_ref[sub(r), j * LANE:(j + 1) * LANE] = lse_t
    for q in range(nq):
        o_ref[:, q * hd:(q + 1) * hd] = o_scr[q][...]


def _swa_attn_kernel(q_ref, kc_ref, kp_ref, vc_ref, vp_ref, sink_ref, gate_ref, y_ref):
    half = SWA_HD
    n_slab = SWA_QH // 2
    slab_per_kvh = n_slab // SWA_KVH
    k2 = jnp.concatenate([kp_ref[...], kc_ref[...]], axis=0)
    v2 = jnp.concatenate([vp_ref[...], vc_ref[...]], axis=0)
    lane = lax.broadcasted_iota(jnp.int32, k2.shape, 1)
    lo = lane < half
    k2r = pltpu.roll(k2, half, 1)
    v2r = pltpu.roll(v2, half, 1)
    kk = [jnp.concatenate([jnp.where(lo, k2, 0.0), jnp.where(lo, 0.0, k2r)], axis=0),
          jnp.concatenate([jnp.where(lo, k2r, 0.0), jnp.where(lo, 0.0, k2)], axis=0)]
    vv = [jnp.concatenate([jnp.where(lo, v2, 0.0), jnp.where(lo, 0.0, v2r)], axis=0),
          jnp.concatenate([jnp.where(lo, v2r, 0.0), jnp.where(lo, 0.0, v2)], axis=0)]
    kk = [a.astype(BF16) for a in kk]
    vv = [a.astype(BF16) for a in vv]
    valid = _band_mask(BLOCK, 4 * BLOCK, SWA_WINDOW, pl.program_id(1) == 0)
    sink = sink_ref[...]
    lane_o = lax.broadcasted_iota(jnp.int32, (BLOCK, LANE), 1) < half
    slabs = range(n_slab)
    s = []
    for a in slabs:
        sa = lax.dot_general(q_ref[:, a * LANE:(a + 1) * LANE] * (SWA_HD ** -0.5), kk[a // slab_per_kvh],
                             (((1,), (1,)), ((), ())), preferred_element_type=F32)
        s.append(jnp.where(valid, sa, NEG_INF))
    heads = range(SWA_QH)
    sh = [s[h // 2][:, (h % 2) * 2 * BLOCK:(h % 2 + 1) * 2 * BLOCK] for h in heads]
    snk = [sink[:, h:h + 1] for h in heads]
    m = [jnp.maximum(jnp.max(sh[h], axis=-1, keepdims=True), snk[h]) for h in heads]
    pe = [jnp.exp(sh[h] - m[h]) for h in heads]
    dh = [jnp.sum(pe[h], axis=-1, keepdims=True) + jnp.exp(snk[h] - m[h]) for h in heads]
    for a in slabs:
        p = jnp.concatenate([pe[2 * a], pe[2 * a + 1]], axis=1).astype(BF16)
        den = jnp.where(lane_o, dh[2 * a], dh[2 * a + 1])
        o = jnp.dot(p, vv[a // slab_per_kvh], preferred_element_type=F32) / den
        y_ref[:, a * LANE:(a + 1) * LANE] = (o * _silu(gate_ref[:, a * LANE:(a + 1) * LANE])).astype(BF16)


def _row_specs(rows, nblk, width, cb):
    cur = pl.BlockSpec((rows, width), lambda n, i, *_: (n * nblk + i, cb))
    prev = pl.BlockSpec((rows, width), lambda n, i, *_: (n * nblk + jnp.maximum(i - 1, 0), cb))
    return cur, prev


def _dil_attn(hp, gi, n_seq, t_len, bias):
    win, dil = DIL_GROUPS[gi]
    n_kv = DIL_KVH if dil == 1 else 1
    nq = n_kv * DIL_G
    rows = BLOCK * dil
    nblk = t_len // rows
    m = n_seq * t_len

    def head_spec(unit0, j, back):
        return pl.BlockSpec((rows, DIL_HD),
                            lambda n, i, h: (n * nblk + jnp.maximum(i - back, 0), unit0 + h * n_kv + j))

    qspecs = [pl.BlockSpec((rows, DIL_HD), functools.partial(
        lambda n, i, h, q: (n * nblk + i, U_BQ + gi * DIL_QH + h * nq + q), q=q)) for q in range(nq)]
    kvspecs = []
    for j in range(n_kv):
        kvspecs += [head_spec(U_BK + gi * DIL_KVH, j, 0), head_spec(U_BK + gi * DIL_KVH, j, 1),
                    head_spec(U_BV + gi * DIL_KVH, j, 0), head_spec(U_BV + gi * DIL_KVH, j, 1)]
    return pl.pallas_call(
        functools.partial(_dil_attn_kernel, window=win // dil, dil=dil, n_kv=n_kv),
        out_shape=(jax.ShapeDtypeStruct((m, DIL_QH * DIL_HD), F32),
                   jax.ShapeDtypeStruct((m, DIL_KVH * LANE), F32)),
        grid=(n_seq, nblk, DIL_KVH // n_kv),
        in_specs=qspecs + kvspecs + [pl.BlockSpec((nq, BLOCK, 2 * BLOCK), lambda n, i, h: (h, 0, 0))],
        out_specs=(pl.BlockSpec((rows, nq * DIL_HD), lambda n, i, h: (n * nblk + i, h)),
                   pl.BlockSpec((rows, n_kv * LANE), lambda n, i, h: (n * nblk + i, h))),
        scratch_shapes=[pltpu.VMEM((rows, DIL_HD), F32)] * nq,
        compiler_params=_cparams(("parallel", "arbitrary", "arbitrary")), name=f"dil_attn{gi}",
    )(*([hp] * (nq + 4 * n_kv)), bias)


def _swa_attn(rq, rk, hp, ht, sink_row, n_seq, t_len):
    wq, wk = SWA_QH * SWA_HD, SWA_KVH * SWA_HD
    nblk = t_len // BLOCK
    qspec, _ = _row_specs(BLOCK, nblk, wq, 0)
    kcur, kprev = _row_specs(BLOCK, nblk, wk, 0)
    vcur, vprev = _row_specs(BLOCK, nblk, wk, T_DV)
    gspec, _ = _row_specs(BLOCK, nblk, wq, U_DG * LANE // wq)
    return pl.pallas_call(
        _swa_attn_kernel,
        out_shape=jax.ShapeDtypeStruct((n_seq * t_len, wq), BF16),
        grid=(n_seq, nblk),
        in_specs=[qspec, kcur, kprev, vcur, vprev, pl.BlockSpec(sink_row.shape, lambda n, i: (0, 0)), gspec],
        out_specs=qspec,
        compiler_params=_cparams(("parallel", "arbitrary")), name="swa_attn",
    )(rq, rk, rk, ht, ht, sink_row, hp)


def _dil_merge_kernel(o0_ref, o1_ref, o2_ref, l0_ref, l1_ref, l2_ref, g_ref, y_ref):
    l0 = l0_ref[...]
    l1 = l1_ref[...]
    l2 = l2_ref[...]
    m = jnp.maximum(jnp.maximum(l0, l1), l2)
    w0 = jnp.exp(l0 - m)
    w1 = jnp.exp(l1 - m)
    w2 = jnp.exp(l2 - m)
    inv = 1.0 / (w0 + w1 + w2)
    w0 = w0 * inv
    w1 = w1 * inv
    w2 = w2 * inv
    for h in range(DIL_QH):
        sl = slice(h * DIL_HD, (h + 1) * DIL_HD)
        c = (h // DIL_G) * LANE + h % DIL_G
        o = (w0[:, c:c + 1] * o0_ref[:, sl] + w1[:, c:c + 1] * o1_ref[:, sl] + w2[:, c:c + 1] * o2_ref[:, sl])
        y_ref[:, sl] = (o * _silu(g_ref[:, sl])).astype(BF16)


def _dil_merge(o0, o1, o2, l0, l1, l2, hp, tm):
    m = o0.shape[0]
    ospec = pl.BlockSpec((tm, BRANCH_W), lambda i: (i, 0))
    lspec = pl.BlockSpec((tm, DIL_KVH * LANE), lambda i: (i, 0))
    return pl.pallas_call(
        _dil_merge_kernel,
        out_shape=jax.ShapeDtypeStruct((m, BRANCH_W), BF16),
        grid=(m // tm,),
        in_specs=[ospec, ospec, ospec, lspec, lspec, lspec,
                  pl.BlockSpec((tm, BRANCH_W), lambda i: (i, U_BG * LANE // BRANCH_W))],
        out_specs=ospec,
        compiler_params=_cparams(("parallel",)),
        name="dil_merge",
    )(o0, o1, o2, l0, l1, l2, hp)


def _lru_gates(cx, wa_ref, wx_ref, ba, bx, lam):
    ra, rx = [], []
    for b in range(LRU_BLOCKS):
        xb = cx[:, b * LRU_BS:(b + 1) * LRU_BS].astype(BF16)
        ra.append(jnp.dot(xb, wa_ref[b], preferred_element_type=F32))
        rx.append(jnp.dot(xb, wx_ref[b], preferred_element_type=F32))
    r = _sigmoid(jnp.concatenate(ra, axis=1) + ba)
    ig = _sigmoid(jnp.concatenate(rx, axis=1) + bx)
    log_a = -LRU_C * r * _softplus(-lam)
    a = jnp.exp(log_a)
    th = jnp.tanh(log_a)
    bterm = jnp.sqrt(-2.0 * th / (1.0 - th)) * (ig * cx)
    return a, bterm


def _lru_kernel(x_ref, g_ref, cw_ref, cb_ref, wa_ref, wx_ref, ba_ref, bx_ref, lam_ref,
                y_ref, hl_ref, prev_ref, h_ref, a_s, b_s):
    tb = x_ref.shape[0]

    @pl.when(pl.program_id(1) == 0)
    def _():
        prev_ref[...] = jnp.zeros_like(prev_ref)
        h_ref[...] = jnp.zeros_like(h_ref)

    x = x_ref[...]
    cx = _conv_rows(x, prev_ref[...], cw_ref[...]) + cb_ref[...]
    prev_ref[...] = x[tb - 8:tb]
    a, bterm = _lru_gates(cx, wa_ref, wx_ref, ba_ref[...], bx_ref[...], lam_ref[...])
    a_s[...] = a
    b_s[...] = bterm

    def step(t, h):
        h = a_s[pl.ds(t, 1), :] * h + b_s[pl.ds(t, 1), :]
        b_s[pl.ds(t, 1), :] = h
        return h

    h = lax.fori_loop(0, tb, step, h_ref[...], unroll=8)
    h_ref[...] = h
    hl_ref[0] = h
    y_ref[...] = (b_s[...] * _silu(g_ref[...])).astype(BF16)


def _lru_prompt(hp, n_seq, t_len, conv_w, conv_b, wa, wx, ba, bx, lam, tb):
    nb = t_len // tb
    vec = pl.BlockSpec((1, LRU_W), lambda n, c: (0, 0))
    wspec = pl.BlockSpec((LRU_BLOCKS, LRU_BS, LRU_BS), lambda n, c: (0, 0, 0))
    return pl.pallas_call(
        _lru_kernel,
        out_shape=(jax.ShapeDtypeStruct((n_seq * t_len, LRU_W), BF16),
                   jax.ShapeDtypeStruct((n_seq, 1, LRU_W), F32)),
        grid=(n_seq, nb),
        in_specs=[pl.BlockSpec((tb, LRU_W), lambda n, c: (n * nb + c, U_CX * LANE // LRU_W)),
                  pl.BlockSpec((tb, LRU_W), lambda n, c: (n * nb + c, U_CG * LANE // LRU_W)),
                  pl.BlockSpec((CONV_W, LRU_W), lambda n, c: (0, 0)),
                  vec, wspec, wspec, vec, vec, vec],
        out_specs=(pl.BlockSpec((tb, LRU_W), lambda n, c: (n * nb + c, 0)),
                   pl.BlockSpec((1, 1, LRU_W), lambda n, c: (n, 0, 0))),
        scratch_shapes=[pltpu.VMEM((8, LRU_W), F32), pltpu.VMEM((1, LRU_W), F32),
                        pltpu.VMEM((tb, LRU_W), F32), pltpu.VMEM((tb, LRU_W), F32)],
        compiler_params=_cparams(("parallel", "arbitrary")),
        name="lru_prompt",
    )(hp, hp, conv_w, conv_b, wa, wx, ba, bx, lam)


def _rope_kernel(q_ref, k_ref, cos_ref, sin_ref, rq_ref, rk_ref):
    cos = cos_ref[...]
    sin = sin_ref[...]
    k = k_ref[...]
    lane = lax.broadcasted_iota(jnp.int32, k.shape, 1)
    first = (lane & (SWA_HD - 1)) < SWA_HD // 2
    half = SWA_HD // 2
    ks = jnp.where(first, pltpu.roll(k, LANE - half, 1), pltpu.roll(k, half, 1))
    rk_ref[...] = k * cos + ks * sin
    q = q_ref[...]
    wq = q.shape[1]
    reps = wq // LANE
    cosq = jnp.concatenate([cos] * reps, axis=1)
    sinq = jnp.concatenate([sin] * reps, axis=1)
    firstq = jnp.concatenate([first] * reps, axis=1)
    qs = jnp.where(firstq, pltpu.roll(q, wq - half, 1), pltpu.roll(q, half, 1))
    rq_ref[...] = (q * cosq + qs * sinq).astype(rq_ref.dtype)


def _rope_prompt(hp, ht, cos_t, sin_t, t_len, tm):
    m = hp.shape[0]
    nt = t_len // tm
    wq = SWA_QH * SWA_HD
    return pl.pallas_call(
        _rope_kernel,
        out_shape=(jax.ShapeDtypeStruct((m, wq), BF16), jax.ShapeDtypeStruct((m, LANE), F32)),
        grid=(m // tm,),
        in_specs=[pl.BlockSpec((tm, wq), lambda i: (i, U_DQ * LANE // wq)),
                  pl.BlockSpec((tm, LANE), lambda i: (i, T_DK)),
                  pl.BlockSpec((tm, LANE), lambda i: (i % nt, 0)),
                  pl.BlockSpec((tm, LANE), lambda i: (i % nt, 0))],
        out_specs=(pl.BlockSpec((tm, wq), lambda i: (i, 0)), pl.BlockSpec((tm, LANE), lambda i: (i, 0))),
        compiler_params=_cparams(("parallel",)),
        name="rope",
    )(hp, ht, cos_t, sin_t)


def _branch_kernel(ya_ref, yb_ref, yc_ref, yd_ref, g0_ref, g1_ref, g2_ref, g3_ref, wb_ref, o_ref, wbf_ref):
    i, j = pl.program_id(0), pl.program_id(1)

    @pl.when(i == 0)
    def _():
        wbf_ref[j] = wb_ref[...].astype(BF16)

    acc = None
    for b, (y_ref, g_ref) in enumerate(((ya_ref, g0_ref), (yb_ref, g1_ref), (yc_ref, g2_ref), (yd_ref, g3_ref))):
        br = jnp.dot(y_ref[...], wbf_ref[j, b], preferred_element_type=F32)
        term = _sigmoid(g_ref[...]) * br
        acc = term if acc is None else acc + term
    o_ref[...] = acc.astype(BF16)


def _branch_merge(ys, hp, w_branch, layer, tm, tn):
    m = ys[0].shape[0]
    nj = D_MODEL // tn
    yspec = pl.BlockSpec((tm, BRANCH_W), lambda i, j: (i, 0))
    gspecs = [pl.BlockSpec((tm, tn), functools.partial(lambda i, j, b: (i, U_MG * LANE // tn + b * nj + j), b=b))
              for b in range(N_BRANCH)]
    wspec = pl.BlockSpec((None, N_BRANCH, BRANCH_W, tn),
                         lambda i, j: (layer, 0, 0, jnp.where(i == 0, j, nj - 1)), pipeline_mode=pl.Buffered(1))
    return pl.pallas_call(
        _branch_kernel,
        out_shape=jax.ShapeDtypeStruct((m, D_MODEL), BF16),
        grid=(m // tm, nj),
        in_specs=[yspec, yspec, yspec, yspec] + gspecs + [wspec],
        out_specs=pl.BlockSpec((tm, tn), lambda i, j: (i, j)),
        scratch_shapes=[pltpu.VMEM((nj, N_BRANCH, BRANCH_W, tn), BF16)],
        compiler_params=_cparams(("arbitrary", "arbitrary")),
        name="branch_merge",
    )(*ys, hp, hp, hp, hp, w_branch)


def _out_kernel(m_ref, x_ref, w_ref, g_ref, b_ref, o_ref, ob_ref, wbf_ref):
    @pl.when(pl.program_id(0) == 0)
    def _():
        wbf_ref[...] = w_ref[...].astype(BF16)

    f = jnp.dot(m_ref[...], wbf_ref[...], preferred_element_type=F32)
    z = DN_ALPHA * x_ref[...] + f
    mu = jnp.mean(z, axis=-1, keepdims=True)
    zc = z - mu
    var = jnp.mean(zc * zc, axis=-1, keepdims=True)
    y = zc * lax.rsqrt(var + LN_EPS) * g_ref[...] + b_ref[...]
    o_ref[...] = y
    ob_ref[...] = y.astype(BF16)


def _out_proj(merged, x, w_out, layer, ln_g, ln_b, tm):
    m = x.shape[0]
    vec = pl.BlockSpec((1, D_MODEL), lambda i: (0, 0))
    ospec = pl.BlockSpec((tm, D_MODEL), lambda i: (i, 0))
    return pl.pallas_call(
        _out_kernel,
        out_shape=(jax.ShapeDtypeStruct((m, D_MODEL), F32), jax.ShapeDtypeStruct((m, D_MODEL), BF16)),
        grid=(m // tm,),
        in_specs=[pl.BlockSpec((tm, D_MODEL), lambda i: (i, 0)),
                  pl.BlockSpec((tm, D_MODEL), lambda i: (i, 0)),
                  pl.BlockSpec((None, D_MODEL, D_MODEL), lambda i: (layer, 0, 0), pipeline_mode=pl.Buffered(1)),
                  vec, vec],
        out_specs=(ospec, ospec),
        scratch_shapes=[pltpu.VMEM((D_MODEL, D_MODEL), BF16)],
        compiler_params=_cparams(("arbitrary",)),
        name="out_proj",
    )(merged, x, w_out, ln_g, ln_b)


def _gdn_dec_pre_kernel(x_ref, b0_ref, b1_ref, b2_ref, cw_ref, ab_ref, arow_ref, dtb_ref,
                        qkv_ref, beta_ref, eg_ref):
    cw = cw_ref[...]
    y = b0_ref[...] * cw[0:1] + b1_ref[...] * cw[1:2] + b2_ref[...] * cw[2:3] + x_ref[...] * cw[3:4]
    act = _silu(y)
    for h in range(GDN_HEADS):
        sq = slice(h * GDN_DK, (h + 1) * GDN_DK)
        sk = slice((GDN_HEADS + h) * GDN_DK, (GDN_HEADS + h + 1) * GDN_DK)
        q = act[:, sq]
        k = act[:, sk]
        qkv_ref[:, sq] = q * lax.rsqrt(jnp.sum(q * q, axis=-1, keepdims=True) + 1e-6) * (GDN_DK ** -0.5)
        qkv_ref[:, sk] = k * lax.rsqrt(jnp.sum(k * k, axis=-1, keepdims=True) + 1e-6)
    qkv_ref[:, 2 * GDN_HEADS * GDN_DK:] = act[:, 2 * GDN_HEADS * GDN_DK:]
    ab = ab_ref[...]
    beta_ref[...] = _sigmoid(ab)
    eg_ref[...] = jnp.exp(-arow_ref[...] * _softplus(ab + dtb_ref[...]))


def _gdn_dec_pre(x, b0, b1, b2, conv_w, ab, arow, dtb):
    n = x.shape[0]
    full = lambda a: pl.BlockSpec(a.shape, lambda i: (0,) * a.ndim)
    ins = (x, b0, b1, b2, conv_w, ab, arow, dtb)
    return pl.pallas_call(
        _gdn_dec_pre_kernel,
        out_shape=(jax.ShapeDtypeStruct((n, GDN_QKV), F32), jax.ShapeDtypeStruct((n, LANE), F32),
                   jax.ShapeDtypeStruct((n, LANE), F32)),
        grid=(1,),
        in_specs=[full(a) for a in ins],
        out_specs=(pl.BlockSpec((n, GDN_QKV), lambda i: (0, 0)), pl.BlockSpec((n, LANE), lambda i: (0, 0)),
                   pl.BlockSpec((n, LANE), lambda i: (0, 0))),
        compiler_params=_cparams(("arbitrary",)),
        name="gdn_dec_pre",
    )(*ins)


def _gdn_dec_kernel(qt_ref, kt_ref, v_ref, z_ref, beta_ref, eg_ref, nw_ref, s_ref, so_ref, y_ref):
    qt = qt_ref[0]
    kt = kt_ref[0]
    v = v_ref[0]
    z = z_ref[0]
    beta = beta_ref[0]
    eg = eg_ref[0]
    nw = nw_ref[...]
    heads = range(GDN_HEADS)
    s = [s_ref[0, h] * eg[:, GDN_HEADS + h:GDN_HEADS + h + 1] for h in heads]
    kcol = [kt[:, h:h + 1] for h in heads]
    v_old = [jnp.sum(s[h] * kcol[h], axis=0, keepdims=True) for h in heads]
    delta = [(v[h:h + 1, :] - v_old[h]) * beta[:, h:h + 1] for h in heads]
    s = [s[h] + kcol[h] * delta[h] for h in heads]
    o = [jnp.sum(s[h] * qt[:, h:h + 1], axis=0, keepdims=True) for h in heads]
    for h in heads:
        so_ref[0, h] = s[h]
        y = o[h] * lax.rsqrt(jnp.mean(o[h] * o[h], axis=-1, keepdims=True) + RMS_EPS) * nw
        y_ref[0, h:h + 1, :] = y * _silu(z[h:h + 1, :])


def _gdn_dec(qt, kt, v, z, beta, eg, norm_w, state, layer):
    n = qt.shape[0]
    tspec = pl.BlockSpec((1, GDN_DK, GDN_HEADS), lambda i: (i, 0, 0))
    hspec = pl.BlockSpec((1, GDN_HEADS, GDN_DV), lambda i: (i, 0, 0))
    rspec = pl.BlockSpec((1, 1, LANE), lambda i: (i, 0, 0))
    return pl.pallas_call(
        _gdn_dec_kernel,
        out_shape=(jax.ShapeDtypeStruct((n, GDN_HEADS, GDN_DK, GDN_DV), F32),
                   jax.ShapeDtypeStruct((n, GDN_HEADS, GDN_DV), F32)),
        grid=(n,),
        in_specs=[tspec, tspec, hspec, hspec, rspec, rspec,
                  pl.BlockSpec((1, GDN_DV), lambda i: (0, 0)),
                  pl.BlockSpec((None, 1, GDN_HEADS, GDN_DK, GDN_DV), lambda i: (layer, i, 0, 0, 0))],
        out_specs=(pl.BlockSpec((1, GDN_HEADS, GDN_DK, GDN_DV), lambda i: (i, 0, 0, 0)), hspec),
        compiler_params=_cparams(("parallel",)),
        name="gdn_dec",
    )(qt, kt, v, z, beta, eg, norm_w, state)


def _dec_attend(problems, n_g, scale):
    hq = problems[0][0].shape[0]
    row = lax.broadcasted_iota(jnp.int32, (hq, 1), 0)
    n_kv = len(problems[0][1])
    sel = [(row >= kvh * n_g) & (row < (kvh + 1) * n_g) for kvh in range(n_kv)]

    def pick(parts):
        out = parts[0]
        for kvh in range(1, n_kv):
            out = jnp.where(sel[kvh], parts[kvh], out)
        return out

    s_parts = [[_dot_nt_hi(q, kc[kvh]) * scale for kvh in range(n_kv)] for q, kc, *_ in problems]
    soft = []
    for (q, kc, vc, knew, vnew, bias_c, bias_n, sink), parts in zip(problems, s_parts):
        sc = pick(parts)
        kn = pick([jnp.broadcast_to(k1, q.shape) for k1 in knew])
        sn = jnp.sum(q * kn, axis=-1, keepdims=True) * scale
        if bias_c is not None:
            sc = sc + bias_c
            sn = sn + bias_n
        m = jnp.maximum(jnp.max(sc, axis=-1, keepdims=True), sn)
        if sink is not None:
            m = jnp.maximum(m, sink)
        pc = jnp.exp(sc - m)
        pn = jnp.exp(sn - m)
        den = jnp.sum(pc, axis=-1, keepdims=True) + pn
        if sink is not None:
            den = den + jnp.exp(sink - m)
        soft.append((pc, pn, den, m))
    o_parts = [[_dot_hi(pc, vc[kvh]) for kvh in range(n_kv)] for (q, kc, vc, *_), (pc, *_) in zip(problems, soft)]
    res = []
    for (q, kc, vc, knew, vnew, *_), (pc, pn, den, m), parts in zip(problems, soft, o_parts):
        vn = pick([jnp.broadcast_to(v1, q.shape) for v1 in vnew])
        res.append(((pick(parts) + pn * vn) / den, m + jnp.log(den)))
    return res


def _dil_dec_kernel(q_ref, kn_ref, vn_ref, g_ref, c0_ref, c1_ref, c2_ref, bc_ref, bn_ref, y_ref):
    q_all = q_ref[0]
    kn_all = kn_ref[0]
    vn_all = vn_ref[0]
    problems = []
    kvw = DIL_KVH * DIL_HD
    for gi, c_ref in enumerate((c0_ref, c1_ref, c2_ref)):
        q = q_all[gi * DIL_QH:(gi + 1) * DIL_QH]
        kc = [c_ref[:, kvh * DIL_HD:(kvh + 1) * DIL_HD] for kvh in range(DIL_KVH)]
        vc = [c_ref[:, kvw + kvh * DIL_HD:kvw + (kvh + 1) * DIL_HD] for kvh in range(DIL_KVH)]
        knew = [kn_all[gi * DIL_KVH + kvh:gi * DIL_KVH + kvh + 1] for kvh in range(DIL_KVH)]
        vnew = [vn_all[gi * DIL_KVH + kvh:gi * DIL_KVH + kvh + 1] for kvh in range(DIL_KVH)]
        problems.append((q, kc, vc, knew, vnew, bc_ref[gi], bn_ref[gi], None))
    res = _dec_attend(problems, DIL_G, DIL_HD ** -0.5)
    outs = [o for o, _ in res]
    lses = [l for _, l in res]
    m = jnp.maximum(jnp.maximum(lses[0], lses[1]), lses[2])
    ws = [jnp.exp(l - m) for l in lses]
    inv = 1.0 / (ws[0] + ws[1] + ws[2])
    o = (ws[0] * outs[0] + ws[1] * outs[1] + ws[2] * outs[2]) * inv
    y_ref[0] = o * _silu(g_ref[0])


def _dil_dec(q3, kn3, vn3, g3, views, layer, bias_c, bias_n):
    n = q3.shape[0]
    cspecs = [pl.BlockSpec((None, None) + v.shape[2:], lambda i: (layer, i, 0, 0)) for v in views]
    return pl.pallas_call(
        _dil_dec_kernel,
        out_shape=jax.ShapeDtypeStruct((n, DIL_QH, DIL_HD), F32),
        grid=(n,),
        in_specs=[pl.BlockSpec((1, N_DIL * DIL_QH, DIL_HD), lambda i: (i, 0, 0)),
                  pl.BlockSpec((1, N_DIL * DIL_KVH, DIL_HD), lambda i: (i, 0, 0)),
                  pl.BlockSpec((1, N_DIL * DIL_KVH, DIL_HD), lambda i: (i, 0, 0)),
                  pl.BlockSpec((1, DIL_QH, DIL_HD), lambda i: (i, 0, 0))] + cspecs +
                 [pl.BlockSpec(bias_c.shape, lambda i: (0, 0, 0)),
                  pl.BlockSpec(bias_n.shape, lambda i: (0, 0, 0))],
        out_specs=pl.BlockSpec((1, DIL_QH, DIL_HD), lambda i: (i, 0, 0)),
        compiler_params=_cparams(("parallel",)),
        name="dil_dec",
    )(q3, kn3, vn3, g3, *views, bias_c, bias_n)


def _swa_dec_kernel(q_ref, kn_ref, vn_ref, g_ref, c_ref, sink_ref, cos_ref, sin_ref, perm_ref, y_ref, rk_ref):
    cos = cos_ref[...]
    sin = sin_ref[...]
    perm = perm_ref[...]
    q = q_ref[0]
    kn = kn_ref[0]
    q = q * cos + _dot_hi(q, perm) * sin
    kn = kn * cos + _dot_hi(kn, perm) * sin
    rk_ref[0] = kn
    vn = vn_ref[0]
    kvw = SWA_KVH * SWA_HD
    kc = [c_ref[:, kvh * SWA_HD:(kvh + 1) * SWA_HD] for kvh in range(SWA_KVH)]
    vc = [c_ref[:, kvw + kvh * SWA_HD:kvw + (kvh + 1) * SWA_HD] for kvh in range(SWA_KVH)]
    knew = [kn[kvh:kvh + 1] for kvh in range(SWA_KVH)]
    vnew = [vn[kvh:kvh + 1] for kvh in range(SWA_KVH)]
    (o, _), = _dec_attend([(q, kc, vc, knew, vnew, None, None, sink_ref[...])], SWA_G, SWA_HD ** -0.5)
    y_ref[0] = o * _silu(g_ref[0])


def _swa_dec(q3, kn3, vn3, g3, view, layer, sink_col, cos_d, sin_d, perm):
    n = q3.shape[0]
    win, kvw2 = view.shape[2:]
    return pl.pallas_call(
        _swa_dec_kernel,
        out_shape=(jax.ShapeDtypeStruct((n, SWA_QH, SWA_HD), F32),
                   jax.ShapeDtypeStruct((n, SWA_KVH, SWA_HD), F32)),
        grid=(n,),
        in_specs=[pl.BlockSpec((1, SWA_QH, SWA_HD), lambda i: (i, 0, 0)),
                  pl.BlockSpec((1, SWA_KVH, SWA_HD), lambda i: (i, 0, 0)),
                  pl.BlockSpec((1, SWA_KVH, SWA_HD), lambda i: (i, 0, 0)),
                  pl.BlockSpec((1, SWA_QH, SWA_HD), lambda i: (i, 0, 0)),
                  pl.BlockSpec((None, None, win, kvw2), lambda i: (layer, i, 0, 0)),
                  pl.BlockSpec(sink_col.shape, lambda i: (0, 0)),
                  pl.BlockSpec(cos_d.shape, lambda i: (0, 0)),
                  pl.BlockSpec(sin_d.shape, lambda i: (0, 0)),
                  pl.BlockSpec(perm.shape, lambda i: (0, 0))],
        out_specs=(pl.BlockSpec((1, SWA_QH, SWA_HD), lambda i: (i, 0, 0)),
                   pl.BlockSpec((1, SWA_KVH, SWA_HD), lambda i: (i, 0, 0))),
        compiler_params=_cparams(("parallel",)),
        name="swa_dec",
    )(q3, kn3, vn3, g3, view, sink_col, cos_d, sin_d, perm)


def _lru_dec_kernel(x_ref, g_ref, b0_ref, b1_ref, b2_ref, h0_ref, cw_ref, cb_ref, wa_ref, wx_ref,
                    ba_ref, bx_ref, lam_ref, y_ref, h_ref):
    cw = cw_ref[...]
    cx = (b0_ref[...] * cw[0:1] + b1_ref[...] * cw[1:2] + b2_ref[...] * cw[2:3] + x_ref[...] * cw[3:4]
          + cb_ref[...])
    a, bterm = _lru_gates(cx, wa_ref, wx_ref, ba_ref[...], bx_ref[...], lam_ref[...])
    h = a * h0_ref[...] + bterm
    h_ref[...] = h
    y_ref[...] = h * _silu(g_ref[...])


def _lru_dec(x, g, b0, b1, b2, h0, conv_w, conv_b, wa, wx, ba, bx, lam):
    n = x.shape[0]
    ins = (x, g, b0, b1, b2, h0, conv_w, conv_b, wa, wx, ba, bx, lam)
    full = lambda a: pl.BlockSpec(a.shape, lambda i: (0,) * a.ndim)
    return pl.pallas_call(
        _lru_dec_kernel,
        out_shape=(jax.ShapeDtypeStruct((n, LRU_W), F32), jax.ShapeDtypeStruct((n, LRU_W), F32)),
        grid=(1,),
        in_specs=[full(a) for a in ins],
        out_specs=(pl.BlockSpec((n, LRU_W), lambda i: (0, 0)), pl.BlockSpec((n, LRU_W), lambda i: (0, 0))),
        compiler_params=_cparams(("arbitrary",)),
        name="lru_dec",
    )(*ins)


def _rel_bucket(dist):
    max_exact = REL_BUCKETS // 2
    n = dist.astype(F32)
    large = max_exact + (jnp.log(jnp.maximum(n, 1.0) / max_exact) / math.log(REL_MAX_DIST / max_exact)
                         * (REL_BUCKETS - max_exact)).astype(jnp.int32)
    large = jnp.minimum(large, REL_BUCKETS - 1)
    return jnp.where(dist < max_exact, dist, large)


def _offset_bias(rel_bias, gi, win, dil):
    j = win // dil + 1
    b = rel_bias[_rel_bucket(dil * jnp.arange(j, dtype=jnp.int32))]
    return b[:, gi * DIL_QH:(gi + 1) * DIL_QH].astype(F32)


def _rope_tables(pos):
    half = SWA_HD // 2
    inv = ROPE_THETA ** (-jnp.arange(half, dtype=F32) / half)
    ang = pos.astype(F32)[:, None] * inv[None, :]
    c, s = jnp.cos(ang), jnp.sin(ang)
    return jnp.concatenate([c, c], axis=1), jnp.concatenate([-s, s], axis=1)


SUBLANES = 8


def _shift_kernel(c_ref, n_ref, o_ref, *, k):
    rows = c_ref.shape[0]
    shifted = pltpu.roll(c_ref[...], rows - k, 0)
    row8 = lax.broadcasted_iota(jnp.int32, (SUBLANES, LANE), 0)
    o_ref[0:rows - SUBLANES, :] = shifted[0:rows - SUBLANES]
    o_ref[rows - SUBLANES:rows, :] = jnp.where(row8 >= SUBLANES - k, n_ref[...], shifted[rows - SUBLANES:rows])


def _shift_rows_kernel(c_ref, n_ref, o_ref):
    w = c_ref.shape[0]
    o_ref[0:w - 1] = c_ref[1:w]
    o_ref[w - 1:w] = n_ref[...]


def _shift_append(cache, new_rows):
    d, n, w = cache.shape[:3]
    if cache.shape[-1] != LANE:
        inner = cache.shape[3:]
        zeros = (0,) * len(inner)
        return pl.pallas_call(
            _shift_rows_kernel,
            out_shape=jax.ShapeDtypeStruct(cache.shape, cache.dtype),
            grid=(d, n),
            in_specs=[pl.BlockSpec((None, None, w) + inner, lambda a, b: (a, b, 0) + zeros),
                      pl.BlockSpec((None, None, 1) + inner, lambda a, b: (a, b, 0) + zeros)],
            out_specs=pl.BlockSpec((None, None, w) + inner, lambda a, b: (a, b, 0) + zeros),
            compiler_params=_cparams(("parallel", "parallel")),
            name="shift_append_rows",
        )(cache, new_rows)
    k = int(np.prod(cache.shape[3:])) // LANE
    assert k < SUBLANES
    new8 = jnp.concatenate([jnp.zeros((d, n, SUBLANES - k, LANE), cache.dtype), new_rows.reshape(d, n, k, LANE)],
                           axis=2)
    out = pl.pallas_call(
        functools.partial(_shift_kernel, k=k),
        out_shape=jax.ShapeDtypeStruct((d, n, w * k, LANE), cache.dtype),
        grid=(d, n),
        in_specs=[pl.BlockSpec((None, None, w * k, LANE), lambda a, b: (a, b, 0, 0)),
                  pl.BlockSpec((None, None, SUBLANES, LANE), lambda a, b: (a, b, 0, 0))],
        out_specs=pl.BlockSpec((None, None, w * k, LANE), lambda a, b: (a, b, 0, 0)),
        compiler_params=_cparams(("parallel", "parallel")),
        name="shift_append",
    )(cache.reshape(d, n, w * k, LANE), new8)
    return out.reshape(cache.shape)


def _cols(h, unit, width):
    return h[..., unit * LANE:unit * LANE + width]


def kernel(x_prompt, x_sample, state_gdn, state_gdn_conv, cache_dil_w128, cache_dil_w512, cache_dil_w2048,
           cache_swa, state_rglru, state_rglru_conv, w_in, gdn_conv_w, gdn_a_log, gdn_dt_bias, gdn_norm_w,
           lru_conv_w, lru_conv_b, lru_wa, lru_ba, lru_wx, lru_bx, lru_lambda, swa_sink, rel_bias, w_branch,
           w_out, ln_g, ln_b):
    nb, t_len, _ = x_prompt.shape
    ns = x_sample.shape[0]
    mp = nb * t_len
    caches = (cache_dil_w128, cache_dil_w512, cache_dil_w2048)
    xp = x_prompt.reshape(mp, D_MODEL)
    xs = x_sample.reshape(ns, D_MODEL)

    cos64, sin64 = _rope_tables(jnp.arange(t_len))
    cos_t = jnp.concatenate([cos64, cos64], axis=1)
    sin_t = jnp.concatenate([sin64, sin64], axis=1)
    cos_d, sin_d = _rope_tables(jnp.full((1,), PAST_LEN))
    perm = jnp.asarray(np.roll(np.eye(SWA_HD, dtype=np.float32), SWA_HD // 2, axis=0))
    qi = jnp.arange(BLOCK)[:, None]
    kj = jnp.arange(2 * BLOCK)[None, :]
    bias_p, bias_c, bias_n = [], [], []
    for gi, (win, dil) in enumerate(DIL_GROUPS):
        ob = _offset_bias(rel_bias, gi, win, dil)
        jw = win // dil
        onehot = (jnp.clip(qi + BLOCK - kj, 0, jw)[:, :, None] == jnp.arange(jw + 1)).astype(F32)
        bias_p.append(jnp.einsum('qkj,jh->hqk', onehot, ob, precision=lax.Precision.HIGHEST))
        bias_c.append(jnp.transpose(ob[::-1][:jw], (1, 0)))
        bias_n.append(ob[0][:, None])
    bias_c = jnp.stack(bias_c)
    bias_n = jnp.stack(bias_n)
    dil_views = [c[:, :, ::dil].reshape(DEPTH, ns, win // dil, 2 * DIL_KVH * DIL_HD)
                 for (win, dil), c in zip(DIL_GROUPS, caches)]
    swa_view = cache_swa.reshape(DEPTH, ns, cache_swa.shape[2], 2 * SWA_KVH * SWA_HD)
    wt = jnp.swapaxes(w_in, 1, 2)
    xp_b, xs_b = xp.astype(BF16), xs.astype(BF16)

    new_p = [[] for _ in range(8)]
    new_s = [[] for _ in range(8)]
    for l in range(DEPTH):
        wa = lru_wa[l].astype(BF16)
        wx = lru_wx[l].astype(BF16)
        zpad = jnp.zeros((1, LANE - 2 * GDN_HEADS), F32)
        arow = jnp.concatenate([jnp.zeros((1, GDN_HEADS), F32), jnp.exp(gdn_a_log[l])[None], zpad], axis=1)
        dtb = jnp.concatenate([jnp.zeros((1, GDN_HEADS), F32), gdn_dt_bias[l][None], zpad], axis=1)
        norm_w = gdn_norm_w[l][None]
        conv_b = lru_conv_b[l][None]
        ba, bx, lam = lru_ba[l][None], lru_bx[l][None], lru_lambda[l][None]
        lng, lnb = ln_g[l][None], ln_b[l][None]
        sink = swa_sink[l].astype(F32)

        hp = _in_proj(xp_b, wt, l, 2048)
        ht = _in_proj_tail(xp_b, wt, l, 2048)
        hp3 = hp.reshape(nb, t_len, D_INP)
        ht3 = ht.reshape(nb, t_len, D_TAIL)
        ya, s_p = _gdn_prompt(hp, ht, nb, t_len, gdn_conv_w[l], arow, dtb, norm_w)
        outs, lses = [], []
        for gi in range(N_DIL):
            o_g, l_g = _dil_attn(hp, gi, nb, t_len, bias_p[gi])
            outs.append(o_g)
            lses.append(l_g)
        yb = _dil_merge(outs[0], outs[1], outs[2], lses[0], lses[1], lses[2], hp, 512)
        yc, h_p = _lru_prompt(hp, nb, t_len, lru_conv_w[l], conv_b, wa, wx, ba, bx, lam, 512)
        rq, rk = _rope_prompt(hp, ht, cos_t, sin_t, t_len, 512)
        sink_row = jnp.concatenate([sink[None], jnp.zeros((1, LANE - SWA_QH), F32)], axis=1)
        yd = _swa_attn(rq, rk, hp, ht, sink_row, nb, t_len)
        merged = _branch_merge((ya, yb, yc, yd), hp, w_branch, l, 512, 512)
        xp_new, xp_b_new = _out_proj(merged, xp, w_out, l, lng, lnb, 512)

        new_p[0].append(s_p)
        new_p[1].append(_cols(hp3, U_AQKV, GDN_QKV)[:, t_len - (CONV_W - 1):])
        for gi, (win, dil) in enumerate(DIL_GROUPS):
            kk = _cols(hp3, U_BK + gi * DIL_KVH, DIL_KVH * DIL_HD)[:, t_len - win:]
            vv = _cols(hp3, U_BV + gi * DIL_KVH, DIL_KVH * DIL_HD)[:, t_len - win:]
            new_p[2 + gi].append(jnp.stack([kk, vv], axis=2).reshape(nb, win, 2, DIL_KVH, DIL_HD))
        kk = rk.reshape(nb, t_len, LANE)[:, t_len - SWA_WINDOW:]
        vv = _cols(ht3, T_DV, LANE)[:, t_len - SWA_WINDOW:]
        new_p[5].append(jnp.stack([kk, vv], axis=2).reshape(nb, SWA_WINDOW, 2, SWA_KVH, SWA_HD))
        new_p[6].append(h_p.reshape(nb, LRU_W))
        new_p[7].append(_cols(hp3, U_CX, LRU_W)[:, t_len - (CONV_W - 1):])

        hs = _in_proj(xs_b, wt, l, ns)
        hst = _in_proj_tail(xs_b, wt, l, ns)
        gbuf = state_gdn_conv[l]
        a_qkv_s = _cols(hs, U_AQKV, GDN_QKV)
        qkv_n, beta_s, eg_s = _gdn_dec_pre(a_qkv_s, gbuf[:, 0], gbuf[:, 1], gbuf[:, 2], gdn_conv_w[l],
                                           _cols(hst, T_AB, LANE), arow, dtb)
        qkv4 = qkv_n.reshape(ns, 3, GDN_HEADS, GDN_DK)
        s_s, ya_s = _gdn_dec(jnp.swapaxes(qkv4[:, 0], 1, 2), jnp.swapaxes(qkv4[:, 1], 1, 2), qkv4[:, 2],
                             _cols(hs, U_AZ, BRANCH_W).reshape(ns, GDN_HEADS, GDN_DV),
                             beta_s[:, None], eg_s[:, None], norm_w, state_gdn, l)
        bk_s = _cols(hs, U_BK, N_DIL * DIL_KVH * DIL_HD)
        bv_s = _cols(hs, U_BV, N_DIL * DIL_KVH * DIL_HD)
        yb_s = _dil_dec(_cols(hs, U_BQ, N_DIL * BRANCH_W).reshape(ns, N_DIL * DIL_QH, DIL_HD),
                        bk_s.reshape(ns, N_DIL * DIL_KVH, DIL_HD), bv_s.reshape(ns, N_DIL * DIL_KVH, DIL_HD),
                        _cols(hs, U_BG, BRANCH_W).reshape(ns, DIL_QH, DIL_HD), dil_views, l, bias_c, bias_n)
        lbuf = state_rglru_conv[l]
        cx_s = _cols(hs, U_CX, LRU_W)
        yc_s, h_s = _lru_dec(cx_s, _cols(hs, U_CG, LRU_W), lbuf[:, 0], lbuf[:, 1], lbuf[:, 2], state_rglru[l],
                             lru_conv_w[l], conv_b, wa, wx, ba, bx, lam)
        dv_s = _cols(hst, T_DV, LANE)
        yd_s, rk_s = _swa_dec(_cols(hs, U_DQ, BRANCH_W).reshape(ns, SWA_QH, SWA_HD),
                              _cols(hst, T_DK, LANE).reshape(ns, SWA_KVH, SWA_HD),
                              dv_s.reshape(ns, SWA_KVH, SWA_HD),
                              _cols(hs, U_DG, BRANCH_W).reshape(ns, SWA_QH, SWA_HD),
                              swa_view, l, sink.reshape(SWA_QH, 1), cos_d, sin_d, perm)
        ys_s = (ya_s.reshape(ns, BRANCH_W).astype(BF16), yb_s.reshape(ns, BRANCH_W).astype(BF16),
                yc_s.astype(BF16), yd_s.reshape(ns, BRANCH_W).astype(BF16))
        merged_s = _branch_merge(ys_s, hs, w_branch, l, ns, 512)
        xs_new, xs_b_new = _out_proj(merged_s, xs, w_out, l, lng, lnb, ns)

        new_s[0].append(s_s)
        new_s[1].append(jnp.concatenate([gbuf[:, 1:], a_qkv_s[:, None]], axis=1))
        for gi in range(N_DIL):
            kk = bk_s[:, gi * DIL_KVH * DIL_HD:(gi + 1) * DIL_KVH * DIL_HD].reshape(ns, 1, DIL_KVH, DIL_HD)
            vv = bv_s[:, gi * DIL_KVH * DIL_HD:(gi + 1) * DIL_KVH * DIL_HD].reshape(ns, 1, DIL_KVH, DIL_HD)
            new_s[2 + gi].append(jnp.stack([kk, vv], axis=2))
        new_s[5].append(jnp.stack([rk_s[:, None], dv_s.reshape(ns, 1, SWA_KVH, SWA_HD)], axis=2))
        new_s[6].append(h_s)
        new_s[7].append(jnp.concatenate([lbuf[:, 1:], cx_s[:, None]], axis=1))

        xp, xs, xp_b, xs_b = xp_new, xs_new, xp_b_new, xs_b_new

    p = [jnp.stack(v) for v in new_p]
    s = [jnp.stack(new_s[i]) for i in (0, 1)]
    for i, c in zip((2, 3, 4, 5), caches + (cache_swa,)):
        s.append(_shift_append(c, jnp.stack(new_s[i])))
    s += [jnp.stack(new_s[i]) for i in (6, 7)]
    return (xp.reshape(nb, t_len, D_MODEL), xs.reshape(ns, 1, D_MODEL),
            p[0], s[0], p[1], s[1], p[2], s[2], p[3], s[3], p[4], s[4], p[5], s[5], p[6], s[6], p[7], s[7])
```

```python
import functools
import math

import jax
import jax.numpy as jnp
import numpy as np
from jax import lax
from jax.experimental import pallas as pl
from jax.experimental.pallas import tpu as pltpu

F32 = jnp.float32
BF16 = jnp.bfloat16
NEG_INF = -1e30

D_MODEL = 2048
DEPTH = 2
PAST_LEN = 16384
BRANCH_W = D_MODEL // 2
N_BRANCH = 4
CONV_W = 4
BLOCK = 128
GDN_DK = 128
GDN_DV = 128
GDN_HEADS = BRANCH_W // GDN_DV
GDN_QKV = GDN_HEADS * (2 * GDN_DK + GDN_DV)
DIL_GROUPS = ((128, 1), (512, 4), (2048, 16))
N_DIL = len(DIL_GROUPS)
DIL_HD = 128
DIL_QH = BRANCH_W // DIL_HD
DIL_KVH = 2
DIL_G = DIL_QH // DIL_KVH
LRU_W = BRANCH_W
LRU_BLOCKS = 8
LRU_BS = LRU_W // LRU_BLOCKS
LRU_C = 8.0
SWA_HD = 64
SWA_QH = BRANCH_W // SWA_HD
SWA_KVH = 2
SWA_G = SWA_QH // SWA_KVH
SWA_WINDOW = 128
ROPE_THETA = 150000.0
REL_BUCKETS = 32
REL_MAX_DIST = 2048
LN_EPS = 1e-5
RMS_EPS = 1e-6
DN_ALPHA = (2.0 * DEPTH) ** 0.25

IN_SIZES = (GDN_QKV, GDN_HEADS * GDN_DV, GDN_HEADS, GDN_HEADS,
            N_DIL * DIL_QH * DIL_HD, N_DIL * DIL_KVH * DIL_HD, N_DIL * DIL_KVH * DIL_HD, DIL_QH * DIL_HD,
            LRU_W, LRU_W,
            SWA_QH * SWA_HD, SWA_KVH * SWA_HD, SWA_KVH * SWA_HD, SWA_QH * SWA_HD,
            N_BRANCH * D_MODEL)
_OFF = [0]
for _s in IN_SIZES:
    _OFF.append(_OFF[-1] + _s)
(O_AQKV, O_AZ, O_AB, O_AA, O_BQ, O_BK, O_BV, O_BG, O_CX, O_CG, O_DQ, O_DK, O_DV, O_DG, O_MG, O_END) = _OFF

LANE = 128
U_MG, U_AZ, U_AQKV, U_BQ, U_BG, U_CX, U_CG, U_DQ, U_DG, U_BK, U_BV = (
    0, 64, 72, 96, 120, 128, 136, 144, 152, 160, 166)
N_UNITS = 172
D_INP = N_UNITS * LANE
T_DK, T_DV, T_AB = 0, 1, 2
D_TAIL = 3 * LANE
IN_TN = 512
_TILE_SRC = tuple(seg0 + IN_TN * t for seg0, width in (
    (O_MG, O_END - O_MG), (O_AZ, O_AB - O_AZ), (O_AQKV, O_AZ - O_AQKV), (O_BQ, O_BK - O_BQ), (O_BG, O_CX - O_BG),
    (O_CX, O_CG - O_CX), (O_CG, O_DQ - O_CG), (O_DQ, O_DK - O_DQ), (O_DG, O_MG - O_DG), (O_BK, O_BG - O_BK))
    for t in range(width // IN_TN))
_SRC_ALIGN = 2 * GDN_HEADS
assert len(_TILE_SRC) * IN_TN == D_INP and all(s % _SRC_ALIGN == 0 for s in _TILE_SRC)
GDN_CHUNK = 128
VMEM_LIMIT = 56 * 1024 * 1024


def _cparams(sem):
    return pltpu.CompilerParams(dimension_semantics=sem, vmem_limit_bytes=VMEM_LIMIT)


def _sigmoid(x):
    return 0.5 * jnp.tanh(0.5 * x) + 0.5


def _silu(x):
    return x * _sigmoid(x)


def _softplus(x):
    return jnp.maximum(x, 0.0) + jnp.log1p(jnp.exp(-jnp.abs(x)))


def _dot(a, b):
    return jnp.dot(a.astype(BF16), b.astype(BF16), preferred_element_type=F32)


def _dot_nt(a, b):
    return lax.dot_general(a.astype(BF16), b.astype(BF16), (((1,), (1,)), ((), ())),
                           preferred_element_type=F32)


def _dot_hi(a, b):
    return jnp.dot(a, b, preferred_element_type=F32, precision=lax.Precision.HIGHEST)


def _split2(a):
    hi = a.astype(BF16)
    return hi, (a - hi.astype(F32)).astype(BF16)


def _dot3(a, b):
    ah, al = _split2(a)
    bh, bl = _split2(b)
    return (jnp.dot(ah, bh, preferred_element_type=F32) + jnp.dot(ah, bl, preferred_element_type=F32)
            + jnp.dot(al, bh, preferred_element_type=F32))


def _dot_exact_lhs(a, b):
    a = a.astype(BF16)
    b0 = b.astype(BF16)
    r1 = b - b0.astype(F32)
    b1 = r1.astype(BF16)
    b2 = (r1 - b1.astype(F32)).astype(BF16)
    return (jnp.dot(a, b0, preferred_element_type=F32) + jnp.dot(a, b1, preferred_element_type=F32)
            + jnp.dot(a, b2, preferred_element_type=F32))


def _dot_nt_hi(a, b):
    return lax.dot_general(a, b, (((1,), (1,)), ((), ())), preferred_element_type=F32,
                           precision=lax.Precision.HIGHEST)


def _in_proj_kernel(tab_ref, x_ref, xs_ref, w_ref, o_ref, os_ref):
    del tab_ref
    w = w_ref[0].astype(BF16)
    o_ref[...] = _dot_nt(x_ref[...], w)

    @pl.when(pl.program_id(0) == 0)
    def _():
        os_ref[...] = _dot_nt(xs_ref[...], w)


def _in_proj(x, xs, wt, layer, tm):
    m, k = x.shape
    ns = xs.shape[0]
    nj = D_INP // IN_TN
    grid_spec = pltpu.PrefetchScalarGridSpec(
        num_scalar_prefetch=1,
        grid=(m // tm, nj),
        in_specs=[pl.BlockSpec((tm, k), lambda i, j, tab: (i, 0)),
                  pl.BlockSpec((ns, k), lambda i, j, tab: (0, 0)),
                  pl.BlockSpec((pl.Element(1), pl.Element(IN_TN), pl.Element(k)),
                               lambda i, j, tab: (layer, tab[j] * _SRC_ALIGN, 0))],
        out_specs=(pl.BlockSpec((tm, IN_TN), lambda i, j, tab: (i, j)),
                   pl.BlockSpec((ns, IN_TN), lambda i, j, tab: (0, jnp.where(i == 0, j, nj - 1)))))
    return pl.pallas_call(
        _in_proj_kernel,
        out_shape=(jax.ShapeDtypeStruct((m, D_INP), F32), jax.ShapeDtypeStruct((ns, D_INP), F32)),
        grid_spec=grid_spec,
        compiler_params=_cparams(("arbitrary", "arbitrary")),
        name="in_proj",
    )(jnp.asarray([s // _SRC_ALIGN for s in _TILE_SRC], jnp.int32), x, xs, wt)


def _in_tail_kernel(x_ref, wkv_ref, wab_ref, o_ref):
    x = x_ref[...]
    o_ref[:, 0:2 * LANE] = _dot_nt(x, wkv_ref[0])
    o_ref[:, 2 * LANE:] = _dot_nt(x, wab_ref[0])


def _in_proj_tail(x, wt, layer, tm):
    m, k = x.shape
    return pl.pallas_call(
        _in_tail_kernel,
        out_shape=jax.ShapeDtypeStruct((m, D_TAIL), F32),
        grid=(m // tm,),
        in_specs=[pl.BlockSpec((tm, k), lambda i: (i, 0)),
                  pl.BlockSpec((pl.Element(1), pl.Element(2 * LANE), pl.Element(k)), lambda i: (layer, O_DK, 0)),
                  pl.BlockSpec((pl.Element(1), pl.Element(LANE), pl.Element(k)), lambda i: (layer, O_AB, 0))],
        out_specs=pl.BlockSpec((tm, D_TAIL), lambda i: (i, 0)),
        compiler_params=_cparams(("parallel",)),
        name="in_proj_tail",
    )(x, wt, wt)


def _conv_rows(x, prev8, cw):
    row8 = lax.broadcasted_iota(jnp.int32, (8, x.shape[1]), 0)
    y = x * cw[CONV_W - 1:CONV_W]
    for k in range(1, CONV_W):
        xk = pltpu.roll(x, k, 0)
        fk = pltpu.roll(prev8, k, 0)
        head = jnp.where(row8 < k, fk, xk[0:8])
        xk = jnp.concatenate([head, xk[8:]], axis=0)
        y = y + xk * cw[CONV_W - 1 - k:CONV_W - k]
    return y


def _tri_inv(lmat, ri, ci):
    c = lmat[0].shape[0]
    eye = (ri == ci).astype(F32)
    blk = (ri >> 4) == (ci >> 4)
    d = [jnp.where(blk, l, 0.0) for l in lmat]
    x = [eye - di for di in d]
    p = [_dot(di, di) for di in d]
    x = [xi + _dot(xi, pi) for xi, pi in zip(x, p)]
    p = [_dot(pi, pi) for pi in p]
    x = [xi + _dot(xi, pi) for xi, pi in zip(x, p)]
    p = [_dot(pi, pi) for pi in p]
    x = [xi + _dot(xi, pi) for xi, pi in zip(x, p)]
    sh = 4
    while (1 << sh) < c:
        lower_left = ((ri >> (sh + 1)) == (ci >> (sh + 1))) & (((ri >> sh) & 1) == 1) & (((ci >> sh) & 1) == 0)
        t = [_dot(jnp.where(lower_left, l, 0.0), xi) for l, xi in zip(lmat, x)]
        x = [xi - _dot(xi, ti) for xi, ti in zip(x, t)]
        sh += 1
    return x


def _tri_solve(lmat, rhs, ri, ci):
    tinv = _tri_inv(lmat, ri, ci)
    sol = [_dot(t, r) for t, r in zip(tinv, rhs)]
    resid = [r - s - _dot3(l, s) for r, s, l in zip(rhs, sol, lmat)]
    return [s + _dot(t, r) for s, t, r in zip(sol, tinv, resid)]


def _gdn_kernel(qkv_ref, z_ref, ab_ref, cw_ref, arow_ref, dtb_ref, nw_ref, y_ref, s_ref, prev_ref):
    cidx = pl.program_id(1)
    C = GDN_CHUNK

    @pl.when(cidx == 0)
    def _():
        s_ref[...] = jnp.zeros_like(s_ref)
        prev_ref[...] = jnp.zeros_like(prev_ref)

    x = qkv_ref[...]
    act = _silu(_conv_rows(x, prev_ref[...], cw_ref[...]))
    prev_ref[...] = x[C - 8:C]

    ab = ab_ref[...]
    beta_t = _sigmoid(ab)
    g_t = -arow_ref[...] * _softplus(ab + dtb_ref[...])
    ri = lax.broadcasted_iota(jnp.int32, (C, C), 0)
    ci = lax.broadcasted_iota(jnp.int32, (C, C), 1)
    incl = ri >= ci
    strict = ri > ci
    gc_t = _dot_exact_lhs(incl.astype(F32), g_t)
    gc_tt = gc_t.T
    egc_t = jnp.exp(gc_t)
    nw = nw_ref[...]
    z = z_ref[...]

    heads = range(GDN_HEADS)
    q, k, v, beta, gcc, egc, decay, glc = [], [], [], [], [], [], [], []
    for h in heads:
        qh = act[:, h * GDN_DK:(h + 1) * GDN_DK]
        kh = act[:, (GDN_HEADS + h) * GDN_DK:(GDN_HEADS + h + 1) * GDN_DK]
        q.append(qh * lax.rsqrt(jnp.sum(qh * qh, axis=-1, keepdims=True) + 1e-6) * (GDN_DK ** -0.5))
        k.append(kh * lax.rsqrt(jnp.sum(kh * kh, axis=-1, keepdims=True) + 1e-6))
        v.append(act[:, (2 * GDN_HEADS + h) * GDN_DK:(2 * GDN_HEADS + h + 1) * GDN_DK])
        beta.append(beta_t[:, h:h + 1])
        gcc.append(gc_t[:, GDN_HEADS + h:GDN_HEADS + h + 1])
        gcr = gc_tt[GDN_HEADS + h:GDN_HEADS + h + 1, :]
        egc.append(egc_t[:, GDN_HEADS + h:GDN_HEADS + h + 1])
        glc.append(gc_t[C - 1:C, GDN_HEADS + h:GDN_HEADS + h + 1])
        decay.append(jnp.where(incl, jnp.exp(jnp.where(incl, gcc[h] - gcr, 0.0)), 0.0))
    kb = [k[h] * beta[h] for h in heads]
    lmat = [jnp.where(strict, _dot_nt(kb[h], k[h]) * decay[h], 0.0) for h in heads]
    rhs = [jnp.concatenate([v[h] * beta[h], kb[h] * egc[h]], axis=1) for h in heads]
    sol = _tri_solve(lmat, rhs, ri, ci)
    qk = [_dot_nt(q[h], k[h]) * decay[h] for h in heads]
    s = [s_ref[0, h] for h in heads]
    v_new = [sol[h][:, :GDN_DV] - _dot(sol[h][:, GDN_DV:], s[h]) for h in heads]
    o = [_dot(q[h] * egc[h], s[h]) + _dot(qk[h], v_new[h]) for h in heads]
    for h in heads:
        kd = k[h] * jnp.exp(glc[h] - gcc[h])
        s_ref[0, h] = s[h] * jnp.exp(glc[h]) + _dot(kd.T, v_new[h])
    for h in heads:
        y = o[h] * lax.rsqrt(jnp.mean(o[h] * o[h], axis=-1, keepdims=True) + RMS_EPS) * nw
        y = y * _silu(z[:, h * GDN_DV:(h + 1) * GDN_DV])
        y_ref[:, h * GDN_DV:(h + 1) * GDN_DV] = y.astype(BF16)


def _gdn_prompt(hp, ht, n_seq, t_len, conv_w, arow, dtb, norm_w):
    C = GDN_CHUNK
    nc = t_len // C
    return pl.pallas_call(
        _gdn_kernel,
        out_shape=(jax.ShapeDtypeStruct((n_seq * t_len, BRANCH_W), BF16),
                   jax.ShapeDtypeStruct((n_seq, GDN_HEADS, GDN_DK, GDN_DV), F32)),
        grid=(n_seq, nc),
        in_specs=[pl.BlockSpec((C, GDN_QKV), lambda n, c: (n * nc + c, U_AQKV * LANE // GDN_QKV)),
                  pl.BlockSpec((C, BRANCH_W), lambda n, c: (n * nc + c, U_AZ * LANE // BRANCH_W)),
                  pl.BlockSpec((C, LANE), lambda n, c: (n * nc + c, T_AB)),
                  pl.BlockSpec((CONV_W, GDN_QKV), lambda n, c: (0, 0)),
                  pl.BlockSpec((1, LANE), lambda n, c: (0, 0)),
                  pl.BlockSpec((1, LANE), lambda n, c: (0, 0)),
                  pl.BlockSpec((1, GDN_DV), lambda n, c: (0, 0))],
        out_specs=(pl.BlockSpec((C, BRANCH_W), lambda n, c: (n * nc + c, 0)),
                   pl.BlockSpec((1, GDN_HEADS, GDN_DK, GDN_DV), lambda n, c: (n, 0, 0, 0))),
        scratch_shapes=[pltpu.VMEM((8, GDN_QKV), F32)],
        compiler_params=_cparams(("parallel", "arbitrary")),
        name="gdn_prompt",
    )(hp, hp, ht, conv_w, arow, dtb, norm_w)


def _band_mask(rows, cols, window, first_block):
    qi = lax.broadcasted_iota(jnp.int32, (rows, cols), 0) & (BLOCK - 1)
    kj = lax.broadcasted_iota(jnp.int32, (rows, cols), 1) & (2 * BLOCK - 1)
    off = qi + BLOCK - kj
    kmin = jnp.where(first_block, BLOCK, 0)
    return (off >= 0) & (off <= window) & (kj >= kmin)


def _dil_attn_kernel(*refs, window, dil, n_kv):
    hd, n_g = DIL_HD, DIL_G
    nq = n_kv * n_g
    q_refs = refs[:nq]
    kv_refs = refs[nq:nq + 4 * n_kv]
    bias_ref, o_ref, lse_ref = refs[nq + 4 * n_kv:nq + 4 * n_kv + 3]
    o_scr = refs[nq + 4 * n_kv + 3:]
    rows = n_g * BLOCK
    valid = _band_mask(rows, 2 * BLOCK, window, pl.program_id(1) == 0)
    lane = lax.broadcasted_iota(jnp.int32, (BLOCK, LANE), 1)

    def sub(r):
        return pl.ds(r, BLOCK, stride=dil) if dil > 1 else slice(None)

    bias_m = [jnp.where(valid, bias_ref[j * n_g:(j + 1) * n_g].reshape(rows, 2 * BLOCK), NEG_INF)
              for j in range(n_kv)]
    chains = [(j, r) for r in range(dil) for j in range(n_kv)]
    for c0 in range(0, len(chains), 2):
        grp = chains[c0:c0 + 2]
        s = []
        for j, r in grp:
            kc_ref, kp_ref = kv_refs[4 * j], kv_refs[4 * j + 1]
            qh = jnp.concatenate([q_refs[j * n_g + g][sub(r), :] for g in range(n_g)], axis=0)
            kcat = jnp.concatenate([kp_ref[sub(r), :], kc_ref[sub(r), :]], axis=0)
            s.append(_dot_nt(qh, kcat) * (hd ** -0.5) + bias_m[j])
        m = [jnp.max(sc, axis=-1, keepdims=True) for sc in s]
        p = [jnp.exp(sc - mc) for sc, mc in zip(s, m)]
        den = [jnp.sum(pc, axis=-1, keepdims=True) for pc in p]
        for (j, r), pc, mc, dc in zip(grp, p, m, den):
            vc_ref, vp_ref = kv_refs[4 * j + 2], kv_refs[4 * j + 3]
            vcat = jnp.concatenate([vp_ref[sub(r), :], vc_ref[sub(r), :]], axis=0)
            o = _dot(pc, vcat) / dc
            lse = mc + jnp.log(dc)
            lse_t = jnp.zeros((BLOCK, LANE), F32)
            for g in range(n_g):
                o_scr[j * n_g + g][sub(r), :] = o[g * BLOCK:(g + 1) * BLOCK]
                lse_t = jnp.where(lane == g, lse[g * BLOCK:(g + 1) * BLOCK], lse_t)
            lse_ref[sub(r), j * LANE:(j + 1) * LANE] = lse_t
    for q in range(nq):
        o_ref[:, q * hd:(q + 1) * hd] = o_scr[q][...]


def _rope_slab(x, cos, sin, first):
    quarter = SWA_HD // 2
    return x * cos + jnp.where(first, pltpu.roll(x, LANE - quarter, 1), pltpu.roll(x, quarter, 1)) * sin


def _swa_attn_kernel(q_ref, kc_ref, kp_ref, vc_ref, vp_ref, cc_ref, sc_ref, cp_ref, sp_ref, sink_ref, gate_ref,
                     y_ref, rk_ref):
    half = SWA_HD
    n_slab = SWA_QH // 2
    slab_per_kvh = n_slab // SWA_KVH
    first = (lax.broadcasted_iota(jnp.int32, (BLOCK, LANE), 1) & (SWA_HD - 1)) < SWA_HD // 2
    cos_c, sin_c = cc_ref[...], sc_ref[...]
    k_cur = _rope_slab(kc_ref[...], cos_c, sin_c, first)
    rk_ref[...] = k_cur
    k2 = jnp.concatenate([_rope_slab(kp_ref[...], cp_ref[...], sp_ref[...], first), k_cur], axis=0)
    v2 = jnp.concatenate([vp_ref[...], vc_ref[...]], axis=0)
    lane = lax.broadcasted_iota(jnp.int32, k2.shape, 1)
    lo = lane < half
    k2r = pltpu.roll(k2, half, 1)
    v2r = pltpu.roll(v2, half, 1)
    kk = [jnp.concatenate([jnp.where(lo, k2, 0.0), jnp.where(lo, 0.0, k2r)], axis=0),
          jnp.concatenate([jnp.where(lo, k2r, 0.0), jnp.where(lo, 0.0, k2)], axis=0)]
    vv = [jnp.concatenate([jnp.where(lo, v2, 0.0), jnp.where(lo, 0.0, v2r)], axis=0),
          jnp.concatenate([jnp.where(lo, v2r, 0.0), jnp.where(lo, 0.0, v2)], axis=0)]
    kk = [a.astype(BF16) for a in kk]
    vv = [a.astype(BF16) for a in vv]
    valid = _band_mask(BLOCK, 4 * BLOCK, SWA_WINDOW, pl.program_id(1) == 0)
    sink = sink_ref[...]
    lane_o = lax.broadcasted_iota(jnp.int32, (BLOCK, LANE), 1) < half
    slabs = range(n_slab)
    s = []
    for a in slabs:
        qa = _rope_slab(q_ref[:, a * LANE:(a + 1) * LANE], cos_c, sin_c, first) * (SWA_HD ** -0.5)
        sa = lax.dot_general(qa.astype(BF16), kk[a // slab_per_kvh],
                             (((1,), (1,)), ((), ())), preferred_element_type=F32)
        s.append(jnp.where(valid, sa, NEG_INF))
    heads = range(SWA_QH)
    sh = [s[h // 2][:, (h % 2) * 2 * BLOCK:(h % 2 + 1) * 2 * BLOCK] for h in heads]
    snk = [sink[:, h:h + 1] for h in heads]
    m = [jnp.maximum(jnp.max(sh[h], axis=-1, keepdims=True), snk[h]) for h in heads]
    pe = [jnp.exp(sh[h] - m[h]) for h in heads]
    dh = [jnp.sum(pe[h], axis=-1, keepdims=True) + jnp.exp(snk[h] - m[h]) for h in heads]
    for a in slabs:
        p = jnp.concatenate([pe[2 * a], pe[2 * a + 1]], axis=1).astype(BF16)
        den = jnp.where(lane_o, dh[2 * a], dh[2 * a + 1])
        o = jnp.dot(p, vv[a // slab_per_kvh], preferred_element_type=F32) / den
        y_ref[:, a * LANE:(a + 1) * LANE] = (o * _silu(gate_ref[:, a * LANE:(a + 1) * LANE])).astype(BF16)


def _row_specs(rows, nblk, width, cb):
    cur = pl.BlockSpec((rows, width), lambda n, i, *_: (n * nblk + i, cb))
    prev = pl.BlockSpec((rows, width), lambda n, i, *_: (n * nblk + jnp.maximum(i - 1, 0), cb))
    return cur, prev


def _dil_attn(hp, gi, n_seq, t_len, bias):
    win, dil = DIL_GROUPS[gi]
    n_kv = DIL_KVH if dil == 1 else 1
    nq = n_kv * DIL_G
    rows = BLOCK * dil
    nblk = t_len // rows
    m = n_seq * t_len

    def head_spec(unit0, j, back):
        return pl.BlockSpec((rows, DIL_HD),
                            lambda n, i, h: (n * nblk + jnp.maximum(i - back, 0), unit0 + h * n_kv + j))

    qspecs = [pl.BlockSpec((rows, DIL_HD), functools.partial(
        lambda n, i, h, q: (n * nblk + i, U_BQ + gi * DIL_QH + h * nq + q), q=q)) for q in range(nq)]
    kvspecs = []
    for j in range(n_kv):
        kvspecs += [head_spec(U_BK + gi * DIL_KVH, j, 0), head_spec(U_BK + gi * DIL_KVH, j, 1),
                    head_spec(U_BV + gi * DIL_KVH, j, 0), head_spec(U_BV + gi * DIL_KVH, j, 1)]
    return pl.pallas_call(
        functools.partial(_dil_attn_kernel, window=win // dil, dil=dil, n_kv=n_kv),
        out_shape=(jax.ShapeDtypeStruct((m, DIL_QH * DIL_HD), F32),
                   jax.ShapeDtypeStruct((m, DIL_KVH * LANE), F32)),
        grid=(n_seq, nblk, DIL_KVH // n_kv),
        in_specs=qspecs + kvspecs + [pl.BlockSpec((nq, BLOCK, 2 * BLOCK), lambda n, i, h: (h, 0, 0))],
        out_specs=(pl.BlockSpec((rows, nq * DIL_HD), lambda n, i, h: (n * nblk + i, h)),
                   pl.BlockSpec((rows, n_kv * LANE), lambda n, i, h: (n * nblk + i, h))),
        scratch_shapes=[pltpu.VMEM((rows, DIL_HD), F32)] * nq,
        compiler_params=_cparams(("parallel", "arbitrary", "arbitrary")), name=f"dil_attn{gi}",
    )(*([hp] * (nq + 4 * n_kv)), bias)


def _swa_attn(hp, ht, cos_t, sin_t, sink_row, n_seq, t_len):
    wq, wk = SWA_QH * SWA_HD, SWA_KVH * SWA_HD
    nblk = t_len // BLOCK
    m = n_seq * t_len
    qspec, _ = _row_specs(BLOCK, nblk, wq, U_DQ * LANE // wq)
    kcur, kprev = _row_specs(BLOCK, nblk, wk, T_DK)
    vcur, vprev = _row_specs(BLOCK, nblk, wk, T_DV)
    gspec, _ = _row_specs(BLOCK, nblk, wq, U_DG * LANE // wq)
    tcur = pl.BlockSpec((BLOCK, LANE), lambda n, i: (i, 0))
    tprev = pl.BlockSpec((BLOCK, LANE), lambda n, i: (jnp.maximum(i - 1, 0), 0))
    return pl.pallas_call(
        _swa_attn_kernel,
        out_shape=(jax.ShapeDtypeStruct((m, wq), BF16), jax.ShapeDtypeStruct((m, wk), F32)),
        grid=(n_seq, nblk),
        in_specs=[qspec, kcur, kprev, vcur, vprev, tcur, tcur, tprev, tprev,
                  pl.BlockSpec(sink_row.shape, lambda n, i: (0, 0)), gspec],
        out_specs=(pl.BlockSpec((BLOCK, wq), lambda n, i: (n * nblk + i, 0)),
                   pl.BlockSpec((BLOCK, wk), lambda n, i: (n * nblk + i, 0))),
        compiler_params=_cparams(("parallel", "arbitrary")), name="swa_attn",
    )(hp, ht, ht, ht, ht, cos_t, sin_t, cos_t, sin_t, sink_row, hp)


def _dil_merge_kernel(o0_ref, o1_ref, o2_ref, l0_ref, l1_ref, l2_ref, g_ref, y_ref):
    l0 = l0_ref[...]
    l1 = l1_ref[...]
    l2 = l2_ref[...]
    m = jnp.maximum(jnp.maximum(l0, l1), l2)
    w0 = jnp.exp(l0 - m)
    w1 = jnp.exp(l1 - m)
    w2 = jnp.exp(l2 - m)
    inv = 1.0 / (w0 + w1 + w2)
    w0 = w0 * inv
    w1 = w1 * inv
    w2 = w2 * inv
    for h in range(DIL_QH):
        sl = slice(h * DIL_HD, (h + 1) * DIL_HD)
        c = (h // DIL_G) * LANE + h % DIL_G
        o = (w0[:, c:c + 1] * o0_ref[:, sl] + w1[:, c:c + 1] * o1_ref[:, sl] + w2[:, c:c + 1] * o2_ref[:, sl])
        y_ref[:, sl] = (o * _silu(g_ref[:, sl])).astype(BF16)


def _dil_merge(o0, o1, o2, l0, l1, l2, hp, tm):
    m = o0.shape[0]
    ospec = pl.BlockSpec((tm, BRANCH_W), lambda i: (i, 0))
    lspec = pl.BlockSpec((tm, DIL_KVH * LANE), lambda i: (i, 0))
    return pl.pallas_call(
        _dil_merge_kernel,
        out_shape=jax.ShapeDtypeStruct((m, BRANCH_W), BF16),
        grid=(m // tm,),
        in_specs=[ospec, ospec, ospec, lspec, lspec, lspec,
                  pl.BlockSpec((tm, BRANCH_W), lambda i: (i, U_BG * LANE // BRANCH_W))],
        out_specs=ospec,
        compiler_params=_cparams(("parallel",)),
        name="dil_merge",
    )(o0, o1, o2, l0, l1, l2, hp)


def _lru_gates(cx, wa_ref, wx_ref, ba, bx, lam):
    ra, rx = [], []
    for b in range(LRU_BLOCKS):
        xb = cx[:, b * LRU_BS:(b + 1) * LRU_BS].astype(BF16)
        ra.append(jnp.dot(xb, wa_ref[b], preferred_element_type=F32))
        rx.append(jnp.dot(xb, wx_ref[b], preferred_element_type=F32))
    r = _sigmoid(jnp.concatenate(ra, axis=1) + ba)
    ig = _sigmoid(jnp.concatenate(rx, axis=1) + bx)
    log_a = -LRU_C * r * _softplus(-lam)
    a = jnp.exp(log_a)
    th = jnp.tanh(log_a)
    bterm = jnp.sqrt(-2.0 * th / (1.0 - th)) * (ig * cx)
    return a, bterm


def _lru_kernel(x_ref, g_ref, cw_ref, cb_ref, wa_ref, wx_ref, ba_ref, bx_ref, lam_ref,
                y_ref, hl_ref, prev_ref, h_ref, a_s, b_s):
    tb = x_ref.shape[0]

    @pl.when(pl.program_id(1) == 0)
    def _():
        prev_ref[...] = jnp.zeros_like(prev_ref)
        h_ref[...] = jnp.zeros_like(h_ref)

    x = x_ref[...]
    cx = _conv_rows(x, prev_ref[...], cw_ref[...]) + cb_ref[...]
    prev_ref[...] = x[tb - 8:tb]
    a, bterm = _lru_gates(cx, wa_ref, wx_ref, ba_ref[...], bx_ref[...], lam_ref[...])
    a_s[...] = a
    b_s[...] = bterm

    row8 = lax.broadcasted_iota(jnp.int32, (SUBLANES, LRU_W), 0)

    def step(g, h):
        sl = pl.ds(pl.multiple_of(g * SUBLANES, SUBLANES), SUBLANES)
        a = a_s[sl, :]
        b = b_s[sl, :]
        for d in (1, 2, 4):
            keep = row8 >= d
            b = b + a * jnp.where(keep, pltpu.roll(b, d, 0), 0.0)
            a = a * jnp.where(keep, pltpu.roll(a, d, 0), 1.0)
        hs = a * h + b
        b_s[sl, :] = hs
        return hs[SUBLANES - 1:SUBLANES]

    h = lax.fori_loop(0, tb // SUBLANES, step, h_ref[...], unroll=2)
    h_ref[...] = h
    hl_ref[0] = h
    y_ref[...] = (b_s[...] * _silu(g_ref[...])).astype(BF16)


def _lru_prompt(hp, n_seq, t_len, conv_w, conv_b, wa, wx, ba, bx, lam, tb):
    nb = t_len // tb
    vec = pl.BlockSpec((1, LRU_W), lambda n, c: (0, 0))
    wspec = pl.BlockSpec((LRU_BLOCKS, LRU_BS, LRU_BS), lambda n, c: (0, 0, 0))
    return pl.pallas_call(
        _lru_kernel,
        out_shape=(jax.ShapeDtypeStruct((n_seq * t_len, LRU_W), BF16),
                   jax.ShapeDtypeStruct((n_seq, 1, LRU_W), F32)),
        grid=(n_seq, nb),
        in_specs=[pl.BlockSpec((tb, LRU_W), lambda n, c: (n * nb + c, U_CX * LANE // LRU_W)),
                  pl.BlockSpec((tb, LRU_W), lambda n, c: (n * nb + c, U_CG * LANE // LRU_W)),
                  pl.BlockSpec((CONV_W, LRU_W), lambda n, c: (0, 0)),
                  vec, wspec, wspec, vec, vec, vec],
        out_specs=(pl.BlockSpec((tb, LRU_W), lambda n, c: (n * nb + c, 0)),
                   pl.BlockSpec((1, 1, LRU_W), lambda n, c: (n, 0, 0))),
        scratch_shapes=[pltpu.VMEM((8, LRU_W), F32), pltpu.VMEM((1, LRU_W), F32),
                        pltpu.VMEM((tb, LRU_W), F32), pltpu.VMEM((tb, LRU_W), F32)],
        compiler_params=_cparams(("parallel", "arbitrary")),
        name="lru_prompt",
    )(hp, hp, conv_w, conv_b, wa, wx, ba, bx, lam)


def _branch_kernel(ya_ref, yb_ref, yc_ref, yd_ref, g0_ref, g1_ref, g2_ref, g3_ref, wb_ref, o_ref, wbf_ref):
    i, j = pl.program_id(0), pl.program_id(1)

    @pl.when(i == 0)
    def _():
        wbf_ref[j] = wb_ref[...].astype(BF16)

    acc = None
    for b, (y_ref, g_ref) in enumerate(((ya_ref, g0_ref), (yb_ref, g1_ref), (yc_ref, g2_ref), (yd_ref, g3_ref))):
        br = jnp.dot(y_ref[...], wbf_ref[j, b], preferred_element_type=F32)
        term = _sigmoid(g_ref[...]) * br
        acc = term if acc is None else acc + term
    o_ref[...] = acc.astype(BF16)


def _branch_merge(ys, hp, w_branch, layer, tm, tn):
    m = ys[0].shape[0]
    nj = D_MODEL // tn
    yspec = pl.BlockSpec((tm, BRANCH_W), lambda i, j: (i, 0))
    gspecs = [pl.BlockSpec((tm, tn), functools.partial(lambda i, j, b: (i, U_MG * LANE // tn + b * nj + j), b=b))
              for b in range(N_BRANCH)]
    wspec = pl.BlockSpec((None, N_BRANCH, BRANCH_W, tn),
                         lambda i, j: (layer, 0, 0, jnp.where(i == 0, j, nj - 1)), pipeline_mode=pl.Buffered(1))
    return pl.pallas_call(
        _branch_kernel,
        out_shape=jax.ShapeDtypeStruct((m, D_MODEL), BF16),
        grid=(m // tm, nj),
        in_specs=[yspec, yspec, yspec, yspec] + gspecs + [wspec],
        out_specs=pl.BlockSpec((tm, tn), lambda i, j: (i, j)),
        scratch_shapes=[pltpu.VMEM((nj, N_BRANCH, BRANCH_W, tn), BF16)],
        compiler_params=_cparams(("arbitrary", "arbitrary")),
        name="branch_merge",
    )(*ys, hp, hp, hp, hp, w_branch)


def _out_kernel(m_ref, x_ref, w_ref, g_ref, b_ref, o_ref, ob_ref, wbf_ref):
    @pl.when(pl.program_id(0) == 0)
    def _():
        wbf_ref[...] = w_ref[...].astype(BF16)

    f = jnp.dot(m_ref[...], wbf_ref[...], preferred_element_type=F32)
    z = DN_ALPHA * x_ref[...] + f
    mu = jnp.mean(z, axis=-1, keepdims=True)
    zc = z - mu
    var = jnp.mean(zc * zc, axis=-1, keepdims=True)
    y = zc * lax.rsqrt(var + LN_EPS) * g_ref[...] + b_ref[...]
    o_ref[...] = y
    ob_ref[...] = y.astype(BF16)


def _out_proj(merged, x, w_out, layer, ln_g, ln_b, tm):
    m = x.shape[0]
    vec = pl.BlockSpec((1, D_MODEL), lambda i: (0, 0))
    ospec = pl.BlockSpec((tm, D_MODEL), lambda i: (i, 0))
    return pl.pallas_call(
        _out_kernel,
        out_shape=(jax.ShapeDtypeStruct((m, D_MODEL), F32), jax.ShapeDtypeStruct((m, D_MODEL), BF16)),
        grid=(m // tm,),
        in_specs=[pl.BlockSpec((tm, D_MODEL), lambda i: (i, 0)),
                  pl.BlockSpec((tm, D_MODEL), lambda i: (i, 0)),
                  pl.BlockSpec((None, D_MODEL, D_MODEL), lambda i: (layer, 0, 0), pipeline_mode=pl.Buffered(1)),
                  vec, vec],
        out_specs=(ospec, ospec),
        scratch_shapes=[pltpu.VMEM((D_MODEL, D_MODEL), BF16)],
        compiler_params=_cparams(("arbitrary",)),
        name="out_proj",
    )(merged, x, w_out, ln_g, ln_b)


def _gdn_dec_pre_kernel(x_ref, b0_ref, b1_ref, b2_ref, cw_ref, ab_ref, arow_ref, dtb_ref,
                        qkv_ref, beta_ref, eg_ref):
    cw = cw_ref[...]
    y = b0_ref[...] * cw[0:1] + b1_ref[...] * cw[1:2] + b2_ref[...] * cw[2:3] + x_ref[...] * cw[3:4]
    act = _silu(y)
    for h in range(GDN_HEADS):
        sq = slice(h * GDN_DK, (h + 1) * GDN_DK)
        sk = slice((GDN_HEADS + h) * GDN_DK, (GDN_HEADS + h + 1) * GDN_DK)
        q = act[:, sq]
        k = act[:, sk]
        qkv_ref[:, sq] = q * lax.rsqrt(jnp.sum(q * q, axis=-1, keepdims=True) + 1e-6) * (GDN_DK ** -0.5)
        qkv_ref[:, sk] = k * lax.rsqrt(jnp.sum(k * k, axis=-1, keepdims=True) + 1e-6)
    qkv_ref[:, 2 * GDN_HEADS * GDN_DK:] = act[:, 2 * GDN_HEADS * GDN_DK:]
    ab = ab_ref[...]
    beta_ref[...] = _sigmoid(ab)
    eg_ref[...] = jnp.exp(-arow_ref[...] * _softplus(ab + dtb_ref[...]))


def _gdn_dec_pre(x, b0, b1, b2, conv_w, ab, arow, dtb):
    n = x.shape[0]
    full = lambda a: pl.BlockSpec(a.shape, lambda i: (0,) * a.ndim)
    ins = (x, b0, b1, b2, conv_w, ab, arow, dtb)
    return pl.pallas_call(
        _gdn_dec_pre_kernel,
        out_shape=(jax.ShapeDtypeStruct((n, GDN_QKV), F32), jax.ShapeDtypeStruct((n, LANE), F32),
                   jax.ShapeDtypeStruct((n, LANE), F32)),
        grid=(1,),
        in_specs=[full(a) for a in ins],
        out_specs=(pl.BlockSpec((n, GDN_QKV), lambda i: (0, 0)), pl.BlockSpec((n, LANE), lambda i: (0, 0)),
                   pl.BlockSpec((n, LANE), lambda i: (0, 0))),
        compiler_params=_cparams(("arbitrary",)),
        name="gdn_dec_pre",
    )(*ins)


def _gdn_dec_kernel(qt_ref, kt_ref, v_ref, z_ref, beta_ref, eg_ref, nw_ref, s_ref, so_ref, y_ref):
    qt = qt_ref[0]
    kt = kt_ref[0]
    v = v_ref[0]
    z = z_ref[0]
    beta = beta_ref[0]
    eg = eg_ref[0]
    nw = nw_ref[...]
    heads = range(GDN_HEADS)
    s = [s_ref[0, h] * eg[:, GDN_HEADS + h:GDN_HEADS + h + 1] for h in heads]
    kcol = [kt[:, h:h + 1] for h in heads]
    v_old = [jnp.sum(s[h] * kcol[h], axis=0, keepdims=True) for h in heads]
    delta = [(v[h:h + 1, :] - v_old[h]) * beta[:, h:h + 1] for h in heads]
    s = [s[h] + kcol[h] * delta[h] for h in heads]
    o = [jnp.sum(s[h] * qt[:, h:h + 1], axis=0, keepdims=True) for h in heads]
    for h in heads:
        so_ref[0, h] = s[h]
        y = o[h] * lax.rsqrt(jnp.mean(o[h] * o[h], axis=-1, keepdims=True) + RMS_EPS) * nw
        y_ref[0, h:h + 1, :] = y * _silu(z[h:h + 1, :])


def _gdn_dec(qt, kt, v, z, beta, eg, norm_w, state, layer):
    n = qt.shape[0]
    tspec = pl.BlockSpec((1, GDN_DK, GDN_HEADS), lambda i: (i, 0, 0))
    hspec = pl.BlockSpec((1, GDN_HEADS, GDN_DV), lambda i: (i, 0, 0))
    rspec = pl.BlockSpec((1, 1, LANE), lambda i: (i, 0, 0))
    return pl.pallas_call(
        _gdn_dec_kernel,
        out_shape=(jax.ShapeDtypeStruct((n, GDN_HEADS, GDN_DK, GDN_DV), F32),
                   jax.ShapeDtypeStruct((n, GDN_HEADS, GDN_DV), F32)),
        grid=(n,),
        in_specs=[tspec, tspec, hspec, hspec, rspec, rspec,
                  pl.BlockSpec((1, GDN_DV), lambda i: (0, 0)),
                  pl.BlockSpec((None, 1, GDN_HEADS, GDN_DK, GDN_DV), lambda i: (layer, i, 0, 0, 0))],
        out_specs=(pl.BlockSpec((1, GDN_HEADS, GDN_DK, GDN_DV), lambda i: (i, 0, 0, 0)), hspec),
        compiler_params=_cparams(("parallel",)),
        name="gdn_dec",
    )(qt, kt, v, z, beta, eg, norm_w, state)


def _dec_attend(problems, n_g, scale):
    hq = problems[0][0].shape[0]
    row = lax.broadcasted_iota(jnp.int32, (hq, 1), 0)
    n_kv = len(problems[0][1])
    sel = [(row >= kvh * n_g) & (row < (kvh + 1) * n_g) for kvh in range(n_kv)]

    def pick(parts):
        out = parts[0]
        for kvh in range(1, n_kv):
            out = jnp.where(sel[kvh], parts[kvh], out)
        return out

    s_parts = [[_dot_nt_hi(q, kc[kvh]) * scale for kvh in range(n_kv)] for q, kc, *_ in problems]
    soft = []
    for (q, kc, vc, knew, vnew, bias_c, bias_n, sink), parts in zip(problems, s_parts):
        sc = pick(parts)
        kn = pick([jnp.broadcast_to(k1, q.shape) for k1 in knew])
        sn = jnp.sum(q * kn, axis=-1, keepdims=True) * scale
        if bias_c is not None:
            sc = sc + bias_c
            sn = sn + bias_n
        m = jnp.maximum(jnp.max(sc, axis=-1, keepdims=True), sn)
        if sink is not None:
            m = jnp.maximum(m, sink)
        pc = jnp.exp(sc - m)
        pn = jnp.exp(sn - m)
        den = jnp.sum(pc, axis=-1, keepdims=True) + pn
        if sink is not None:
            den = den + jnp.exp(sink - m)
        soft.append((pc, pn, den, m))
    o_parts = [[_dot_hi(pc, vc[kvh]) for kvh in range(n_kv)] for (q, kc, vc, *_), (pc, *_) in zip(problems, soft)]
    res = []
    for (q, kc, vc, knew, vnew, *_), (pc, pn, den, m), parts in zip(problems, soft, o_parts):
        vn = pick([jnp.broadcast_to(v1, q.shape) for v1 in vnew])
        res.append(((pick(parts) + pn * vn) / den, m + jnp.log(den)))
    return res


def _dil_dec_kernel(q_ref, kn_ref, vn_ref, g_ref, c0_ref, c1_ref, c2_ref, bc_ref, bn_ref, y_ref):
    q_all = q_ref[0]
    kn_all = kn_ref[0]
    vn_all = vn_ref[0]
    problems = []
    kvw = DIL_KVH * DIL_HD
    for gi, c_ref in enumerate((c0_ref, c1_ref, c2_ref)):
        q = q_all[gi * DIL_QH:(gi + 1) * DIL_QH]
        kc = [c_ref[:, kvh * DIL_HD:(kvh + 1) * DIL_HD] for kvh in range(DIL_KVH)]
        vc = [c_ref[:, kvw + kvh * DIL_HD:kvw + (kvh + 1) * DIL_HD] for kvh in range(DIL_KVH)]
        knew = [kn_all[gi * DIL_KVH + kvh:gi * DIL_KVH + kvh + 1] for kvh in range(DIL_KVH)]
        vnew = [vn_all[gi * DIL_KVH + kvh:gi * DIL_KVH + kvh + 1] for kvh in range(DIL_KVH)]
        problems.append((q, kc, vc, knew, vnew, bc_ref[gi], bn_ref[gi], None))
    res = _dec_attend(problems, DIL_G, DIL_HD ** -0.5)
    outs = [o for o, _ in res]
    lses = [l for _, l in res]
    m = jnp.maximum(jnp.maximum(lses[0], lses[1]), lses[2])
    ws = [jnp.exp(l - m) for l in lses]
    inv = 1.0 / (ws[0] + ws[1] + ws[2])
    o = (ws[0] * outs[0] + ws[1] * outs[1] + ws[2] * outs[2]) * inv
    y_ref[0] = o * _silu(g_ref[0])


def _dil_dec(q3, kn3, vn3, g3, views, layer, bias_c, bias_n):
    n = q3.shape[0]
    cspecs = [pl.BlockSpec((None, None) + v.shape[2:], lambda i: (layer, i, 0, 0)) for v in views]
    return pl.pallas_call(
        _dil_dec_kernel,
        out_shape=jax.ShapeDtypeStruct((n, DIL_QH, DIL_HD), F32),
        grid=(n,),
        in_specs=[pl.BlockSpec((1, N_DIL * DIL_QH, DIL_HD), lambda i: (i, 0, 0)),
                  pl.BlockSpec((1, N_DIL * DIL_KVH, DIL_HD), lambda i: (i, 0, 0)),
                  pl.BlockSpec((1, N_DIL * DIL_KVH, DIL_HD), lambda i: (i, 0, 0)),
                  pl.BlockSpec((1, DIL_QH, DIL_HD), lambda i: (i, 0, 0))] + cspecs +
                 [pl.BlockSpec(bias_c.shape, lambda i: (0, 0, 0)),
                  pl.BlockSpec(bias_n.shape, lambda i: (0, 0, 0))],
        out_specs=pl.BlockSpec((1, DIL_QH, DIL_HD), lambda i: (i, 0, 0)),
        compiler_params=_cparams(("parallel",)),
        name="dil_dec",
    )(q3, kn3, vn3, g3, *views, bias_c, bias_n)


def _swa_dec_kernel(q_ref, kn_ref, vn_ref, g_ref, c_ref, sink_ref, cos_ref, sin_ref, perm_ref, y_ref, rk_ref):
    cos = cos_ref[...]
    sin = sin_ref[...]
    perm = perm_ref[...]
    q = q_ref[0]
    kn = kn_ref[0]
    q = q * cos + _dot_hi(q, perm) * sin
    kn = kn * cos + _dot_hi(kn, perm) * sin
    rk_ref[0] = kn
    vn = vn_ref[0]
    kvw = SWA_KVH * SWA_HD
    kc = [c_ref[:, kvh * SWA_HD:(kvh + 1) * SWA_HD] for kvh in range(SWA_KVH)]
    vc = [c_ref[:, kvw + kvh * SWA_HD:kvw + (kvh + 1) * SWA_HD] for kvh in range(SWA_KVH)]
    knew = [kn[kvh:kvh + 1] for kvh in range(SWA_KVH)]
    vnew = [vn[kvh:kvh + 1] for kvh in range(SWA_KVH)]
    (o, _), = _dec_attend([(q, kc, vc, knew, vnew, None, None, sink_ref[...])], SWA_G, SWA_HD ** -0.5)
    y_ref[0] = o * _silu(g_ref[0])


def _swa_dec(q3, kn3, vn3, g3, view, layer, sink_col, cos_d, sin_d, perm):
    n = q3.shape[0]
    win, kvw2 = view.shape[2:]
    return pl.pallas_call(
        _swa_dec_kernel,
        out_shape=(jax.ShapeDtypeStruct((n, SWA_QH, SWA_HD), F32),
                   jax.ShapeDtypeStruct((n, SWA_KVH, SWA_HD), F32)),
        grid=(n,),
        in_specs=[pl.BlockSpec((1, SWA_QH, SWA_HD), lambda i: (i, 0, 0)),
                  pl.BlockSpec((1, SWA_KVH, SWA_HD), lambda i: (i, 0, 0)),
                  pl.BlockSpec((1, SWA_KVH, SWA_HD), lambda i: (i, 0, 0)),
                  pl.BlockSpec((1, SWA_QH, SWA_HD), lambda i: (i, 0, 0)),
                  pl.BlockSpec((None, None, win, kvw2), lambda i: (layer, i, 0, 0)),
                  pl.BlockSpec(sink_col.shape, lambda i: (0, 0)),
                  pl.BlockSpec(cos_d.shape, lambda i: (0, 0)),
                  pl.BlockSpec(sin_d.shape, lambda i: (0, 0)),
                  pl.BlockSpec(perm.shape, lambda i: (0, 0))],
        out_specs=(pl.BlockSpec((1, SWA_QH, SWA_HD), lambda i: (i, 0, 0)),
                   pl.BlockSpec((1, SWA_KVH, SWA_HD), lambda i: (i, 0, 0))),
        compiler_params=_cparams(("parallel",)),
        name="swa_dec",
    )(q3, kn3, vn3, g3, view, sink_col, cos_d, sin_d, perm)


def _lru_dec_kernel(x_ref, g_ref, b0_ref, b1_ref, b2_ref, h0_ref, cw_ref, cb_ref, wa_ref, wx_ref,
                    ba_ref, bx_ref, lam_ref, y_ref, h_ref):
    cw = cw_ref[...]
    cx = (b0_ref[...] * cw[0:1] + b1_ref[...] * cw[1:2] + b2_ref[...] * cw[2:3] + x_ref[...] * cw[3:4]
          + cb_ref[...])
    a, bterm = _lru_gates(cx, wa_ref, wx_ref, ba_ref[...], bx_ref[...], lam_ref[...])
    h = a * h0_ref[...] + bterm
    h_ref[...] = h
    y_ref[...] = h * _silu(g_ref[...])


def _lru_dec(x, g, b0, b1, b2, h0, conv_w, conv_b, wa, wx, ba, bx, lam):
    n = x.shape[0]
    ins = (x, g, b0, b1, b2, h0, conv_w, conv_b, wa, wx, ba, bx, lam)
    full = lambda a: pl.BlockSpec(a.shape, lambda i: (0,) * a.ndim)
    return pl.pallas_call(
        _lru_dec_kernel,
        out_shape=(jax.ShapeDtypeStruct((n, LRU_W), F32), jax.ShapeDtypeStruct((n, LRU_W), F32)),
        grid=(1,),
        in_specs=[full(a) for a in ins],
        out_specs=(pl.BlockSpec((n, LRU_W), lambda i: (0, 0)), pl.BlockSpec((n, LRU_W), lambda i: (0, 0))),
        compiler_params=_cparams(("arbitrary",)),
        name="lru_dec",
    )(*ins)


def _rel_bucket(dist):
    max_exact = REL_BUCKETS // 2
    n = dist.astype(F32)
    large = max_exact + (jnp.log(jnp.maximum(n, 1.0) / max_exact) / math.log(REL_MAX_DIST / max_exact)
                         * (REL_BUCKETS - max_exact)).astype(jnp.int32)
    large = jnp.minimum(large, REL_BUCKETS - 1)
    return jnp.where(dist < max_exact, dist, large)


def _offset_bias(rel_bias, gi, win, dil):
    j = win // dil + 1
    b = rel_bias[_rel_bucket(dil * jnp.arange(j, dtype=jnp.int32))]
    return b[:, gi * DIL_QH:(gi + 1) * DIL_QH].astype(F32)


def _rope_tables(pos):
    half = SWA_HD // 2
    inv = ROPE_THETA ** (-jnp.arange(half, dtype=F32) / half)
    ang = pos.astype(F32)[:, None] * inv[None, :]
    c, s = jnp.cos(ang), jnp.sin(ang)
    return jnp.concatenate([c, c], axis=1), jnp.concatenate([-s, s], axis=1)


SUBLANES = 8


def _shift_kernel(c_ref, n_ref, o_ref, *, k):
    rows = c_ref.shape[0]
    shifted = pltpu.roll(c_ref[...], rows - k, 0)
    row8 = lax.broadcasted_iota(jnp.int32, (SUBLANES, LANE), 0)
    o_ref[0:rows - SUBLANES, :] = shifted[0:rows - SUBLANES]
    o_ref[rows - SUBLANES:rows, :] = jnp.where(row8 >= SUBLANES - k, n_ref[...], shifted[rows - SUBLANES:rows])


def _shift_rows_kernel(c_ref, n_ref, o_ref):
    w = c_ref.shape[0]
    o_ref[0:w - 1] = c_ref[1:w]
    o_ref[w - 1:w] = n_ref[...]


def _shift_append(cache, new_rows):
    d, n, w = cache.shape[:3]
    if cache.shape[-1] != LANE:
        inner = cache.shape[3:]
        zeros = (0,) * len(inner)
        return pl.pallas_call(
            _shift_rows_kernel,
            out_shape=jax.ShapeDtypeStruct(cache.shape, cache.dtype),
            grid=(d, n),
            in_specs=[pl.BlockSpec((None, None, w) + inner, lambda a, b: (a, b, 0) + zeros),
                      pl.BlockSpec((None, None, 1) + inner, lambda a, b: (a, b, 0) + zeros)],
            out_specs=pl.BlockSpec((None, None, w) + inner, lambda a, b: (a, b, 0) + zeros),
            compiler_params=_cparams(("parallel", "parallel")),
            name="shift_append_rows",
        )(cache, new_rows)
    k = int(np.prod(cache.shape[3:])) // LANE
    assert k < SUBLANES
    new8 = jnp.concatenate([jnp.zeros((d, n, SUBLANES - k, LANE), cache.dtype), new_rows.reshape(d, n, k, LANE)],
                           axis=2)
    out = pl.pallas_call(
        functools.partial(_shift_kernel, k=k),
        out_shape=jax.ShapeDtypeStruct((d, n, w * k, LANE), cache.dtype),
        grid=(d, n),
        in_specs=[pl.BlockSpec((None, None, w * k, LANE), lambda a, b: (a, b, 0, 0)),
                  pl.BlockSpec((None, None, SUBLANES, LANE), lambda a, b: (a, b, 0, 0))],
        out_specs=pl.BlockSpec((None, None, w * k, LANE), lambda a, b: (a, b, 0, 0)),
        compiler_params=_cparams(("parallel", "parallel")),
        name="shift_append",
    )(cache.reshape(d, n, w * k, LANE), new8)
    return out.reshape(cache.shape)


def _cols(h, unit, width):
    return h[..., unit * LANE:unit * LANE + width]


def kernel(x_prompt, x_sample, state_gdn, state_gdn_conv, cache_dil_w128, cache_dil_w512, cache_dil_w2048,
           cache_swa, state_rglru, state_rglru_conv, w_in, gdn_conv_w, gdn_a_log, gdn_dt_bias, gdn_norm_w,
           lru_conv_w, lru_conv_b, lru_wa, lru_ba, lru_wx, lru_bx, lru_lambda, swa_sink, rel_bias, w_branch,
           w_out, ln_g, ln_b):
    nb, t_len, _ = x_prompt.shape
    ns = x_sample.shape[0]
    mp = nb * t_len
    caches = (cache_dil_w128, cache_dil_w512, cache_dil_w2048)
    xp = x_prompt.reshape(mp, D_MODEL)
    xs = x_sample.reshape(ns, D_MODEL)

    cos64, sin64 = _rope_tables(jnp.arange(t_len))
    cos_t = jnp.concatenate([cos64, cos64], axis=1)
    sin_t = jnp.concatenate([sin64, sin64], axis=1)
    cos_d, sin_d = _rope_tables(jnp.full((1,), PAST_LEN))
    perm = jnp.asarray(np.roll(np.eye(SWA_HD, dtype=np.float32), SWA_HD // 2, axis=0))
    qi = jnp.arange(BLOCK)[:, None]
    kj = jnp.arange(2 * BLOCK)[None, :]
    bias_p, bias_c, bias_n = [], [], []
    for gi, (win, dil) in enumerate(DIL_GROUPS):
        ob = _offset_bias(rel_bias, gi, win, dil)
        jw = win // dil
        onehot = (jnp.clip(qi + BLOCK - kj, 0, jw)[:, :, None] == jnp.arange(jw + 1)).astype(F32)
        bias_p.append(jnp.einsum('qkj,jh->hqk', onehot, ob, precision=lax.Precision.HIGHEST))
        bias_c.append(jnp.transpose(ob[::-1][:jw], (1, 0)))
        bias_n.append(ob[0][:, None])
    bias_c = jnp.stack(bias_c)
    bias_n = jnp.stack(bias_n)
    dil_views = [c[:, :, ::dil].reshape(DEPTH, ns, win // dil, 2 * DIL_KVH * DIL_HD)
                 for (win, dil), c in zip(DIL_GROUPS, caches)]
    swa_view = cache_swa.reshape(DEPTH, ns, cache_swa.shape[2], 2 * SWA_KVH * SWA_HD)
    wt = jnp.swapaxes(w_in, 1, 2)
    xp_b, xs_b = xp.astype(BF16), xs.astype(BF16)

    new_p = [[] for _ in range(8)]
    new_s = [[] for _ in range(8)]
    for l in range(DEPTH):
        wa = lru_wa[l].astype(BF16)
        wx = lru_wx[l].astype(BF16)
        zpad = jnp.zeros((1, LANE - 2 * GDN_HEADS), F32)
        arow = jnp.concatenate([jnp.zeros((1, GDN_HEADS), F32), jnp.exp(gdn_a_log[l])[None], zpad], axis=1)
        dtb = jnp.concatenate([jnp.zeros((1, GDN_HEADS), F32), gdn_dt_bias[l][None], zpad], axis=1)
        norm_w = gdn_norm_w[l][None]
        conv_b = lru_conv_b[l][None]
        ba, bx, lam = lru_ba[l][None], lru_bx[l][None], lru_lambda[l][None]
        lng, lnb = ln_g[l][None], ln_b[l][None]
        sink = swa_sink[l].astype(F32)

        hp, hs = _in_proj(xp_b, xs_b, wt, l, 2048)
        ht = _in_proj_tail(xp_b, wt, l, 2048)
        hp3 = hp.reshape(nb, t_len, D_INP)
        ht3 = ht.reshape(nb, t_len, D_TAIL)
        ya, s_p = _gdn_prompt(hp, ht, nb, t_len, gdn_conv_w[l], arow, dtb, norm_w)
        outs, lses = [], []
        for gi in range(N_DIL):
            o_g, l_g = _dil_attn(hp, gi, nb, t_len, bias_p[gi])
            outs.append(o_g)
            lses.append(l_g)
        yb = _dil_merge(outs[0], outs[1], outs[2], lses[0], lses[1], lses[2], hp, 512)
        yc, h_p = _lru_prompt(hp, nb, t_len, lru_conv_w[l], conv_b, wa, wx, ba, bx, lam, 512)
        sink_row = jnp.concatenate([sink[None], jnp.zeros((1, LANE - SWA_QH), F32)], axis=1)
        yd, rk = _swa_attn(hp, ht, cos_t, sin_t, sink_row, nb, t_len)
        merged = _branch_merge((ya, yb, yc, yd), hp, w_branch, l, 512, 512)
        xp_new, xp_b_new = _out_proj(merged, xp, w_out, l, lng, lnb, 512)

        new_p[0].append(s_p)
        new_p[1].append(_cols(hp3, U_AQKV, GDN_QKV)[:, t_len - (CONV_W - 1):])
        for gi, (win, dil) in enumerate(DIL_GROUPS):
            kk = _cols(hp3, U_BK + gi * DIL_KVH, DIL_KVH * DIL_HD)[:, t_len - win:]
            vv = _cols(hp3, U_BV + gi * DIL_KVH, DIL_KVH * DIL_HD)[:, t_len - win:]
            new_p[2 + gi].append(jnp.stack([kk, vv], axis=2).reshape(nb, win, 2, DIL_KVH, DIL_HD))
        kk = rk.reshape(nb, t_len, LANE)[:, t_len - SWA_WINDOW:]
        vv = _cols(ht3, T_DV, LANE)[:, t_len - SWA_WINDOW:]
        new_p[5].append(jnp.stack([kk, vv], axis=2).reshape(nb, SWA_WINDOW, 2, SWA_KVH, SWA_HD))
        new_p[6].append(h_p.reshape(nb, LRU_W))
        new_p[7].append(_cols(hp3, U_CX, LRU_W)[:, t_len - (CONV_W - 1):])

        hst = _in_proj_tail(xs_b, wt, l, ns)
        gbuf = state_gdn_conv[l]
        a_qkv_s = _cols(hs, U_AQKV, GDN_QKV)
        qkv_n, beta_s, eg_s = _gdn_dec_pre(a_qkv_s, gbuf[:, 0], gbuf[:, 1], gbuf[:, 2], gdn_conv_w[l],
                                           _cols(hst, T_AB, LANE), arow, dtb)
        qkv4 = qkv_n.reshape(ns, 3, GDN_HEADS, GDN_DK)
        s_s, ya_s = _gdn_dec(jnp.swapaxes(qkv4[:, 0], 1, 2), jnp.swapaxes(qkv4[:, 1], 1, 2), qkv4[:, 2],
                             _cols(hs, U_AZ, BRANCH_W).reshape(ns, GDN_HEADS, GDN_DV),
                             beta_s[:, None], eg_s[:, None], norm_w, state_gdn, l)
        bk_s = _cols(hs, U_BK, N_DIL * DIL_KVH * DIL_HD)
        bv_s = _cols(hs, U_BV, N_DIL * DIL_KVH * DIL_HD)
        yb_s = _dil_dec(_cols(hs, U_BQ, N_DIL * BRANCH_W).reshape(ns, N_DIL * DIL_QH, DIL_HD),
                        bk_s.reshape(ns, N_DIL * DIL_KVH, DIL_HD), bv_s.reshape(ns, N_DIL * DIL_KVH, DIL_HD),
                        _cols(hs, U_BG, BRANCH_W).reshape(ns, DIL_QH, DIL_HD), dil_views, l, bias_c, bias_n)
        lbuf = state_rglru_conv[l]
        cx_s = _cols(hs, U_CX, LRU_W)
        yc_s, h_s = _lru_dec(cx_s, _cols(hs, U_CG, LRU_W), lbuf[:, 0], lbuf[:, 1], lbuf[:, 2], state_rglru[l],
                             lru_conv_w[l], conv_b, wa, wx, ba, bx, lam)
        dv_s = _cols(hst, T_DV, LANE)
        yd_s, rk_s = _swa_dec(_cols(hs, U_DQ, BRANCH_W).reshape(ns, SWA_QH, SWA_HD),
                              _cols(hst, T_DK, LANE).reshape(ns, SWA_KVH, SWA_HD),
                              dv_s.reshape(ns, SWA_KVH, SWA_HD),
                              _cols(hs, U_DG, BRANCH_W).reshape(ns, SWA_QH, SWA_HD),
                              swa_view, l, sink.reshape(SWA_QH, 1), cos_d, sin_d, perm)
        ys_s = (ya_s.reshape(ns, BRANCH_W).astype(BF16), yb_s.reshape(ns, BRANCH_W).astype(BF16),
                yc_s.astype(BF16), yd_s.reshape(ns, BRANCH_W).astype(BF16))
        merged_s = _branch_merge(ys_s, hs, w_branch, l, ns, 512)
        xs_new, xs_b_new = _out_proj(merged_s, xs, w_out, l, lng, lnb, ns)

        new_s[0].append(s_s)
        new_s[1].append(jnp.concatenate([gbuf[:, 1:], a_qkv_s[:, None]], axis=1))
        for gi in range(N_DIL):
            kk = bk_s[:, gi * DIL_KVH * DIL_HD:(gi + 1) * DIL_KVH * DIL_HD].reshape(ns, 1, DIL_KVH, DIL_HD)
            vv = bv_s[:, gi * DIL_KVH * DIL_HD:(gi + 1) * DIL_KVH * DIL_HD].reshape(ns, 1, DIL_KVH, DIL_HD)
            new_s[2 + gi].append(jnp.stack([kk, vv], axis=2))
        new_s[5].append(jnp.stack([rk_s[:, None], dv_s.reshape(ns, 1, SWA_KVH, SWA_HD)], axis=2))
        new_s[6].append(h_s)
        new_s[7].append(jnp.concatenate([lbuf[:, 1:], cx_s[:, None]], axis=1))

        xp, xs, xp_b, xs_b = xp_new, xs_new, xp_b_new, xs_b_new

    p = [jnp.stack(v) for v in new_p]
    s = [jnp.stack(new_s[i]) for i in (0, 1)]
    for i, c in zip((2, 3, 4, 5), caches + (cache_swa,)):
        s.append(_shift_append(c, jnp.stack(new_s[i])))
    s += [jnp.stack(new_s[i]) for i in (6, 7)]
    return (xp.reshape(nb, t_len, D_MODEL), xs.reshape(ns, 1, D_MODEL),
            p[0], s[0], p[1], s[1], p[2], s[2], p[3], s[3], p[4], s[4], p[5], s[5], p[6], s[6], p[7], s[7])
```

```python
import functools
import math

import jax
import jax.numpy as jnp
import numpy as np
from jax import lax
from jax.experimental import pallas as pl
from jax.experimental.pallas import tpu as pltpu

F32 = jnp.float32
BF16 = jnp.bfloat16
NEG_INF = -1e30

D_MODEL = 2048
DEPTH = 2
PAST_LEN = 16384
BRANCH_W = D_MODEL // 2
N_BRANCH = 4
CONV_W = 4
BLOCK = 128
GDN_DK = 128
GDN_DV = 128
GDN_HEADS = BRANCH_W // GDN_DV
GDN_QKV = GDN_HEADS * (2 * GDN_DK + GDN_DV)
DIL_GROUPS = ((128, 1), (512, 4), (2048, 16))
N_DIL = len(DIL_GROUPS)
DIL_HD = 128
DIL_QH = BRANCH_W // DIL_HD
DIL_KVH = 2
DIL_G = DIL_QH // DIL_KVH
LRU_W = BRANCH_W
LRU_BLOCKS = 8
LRU_BS = LRU_W // LRU_BLOCKS
LRU_C = 8.0
SWA_HD = 64
SWA_QH = BRANCH_W // SWA_HD
SWA_KVH = 2
SWA_G = SWA_QH // SWA_KVH
SWA_WINDOW = 128
ROPE_THETA = 150000.0
REL_BUCKETS = 32
REL_MAX_DIST = 2048
LN_EPS = 1e-5
RMS_EPS = 1e-6
DN_ALPHA = (2.0 * DEPTH) ** 0.25

IN_SIZES = (GDN_QKV, GDN_HEADS * GDN_DV, GDN_HEADS, GDN_HEADS,
            N_DIL * DIL_QH * DIL_HD, N_DIL * DIL_KVH * DIL_HD, N_DIL * DIL_KVH * DIL_HD, DIL_QH * DIL_HD,
            LRU_W, LRU_W,
            SWA_QH * SWA_HD, SWA_KVH * SWA_HD, SWA_KVH * SWA_HD, SWA_QH * SWA_HD,
            N_BRANCH * D_MODEL)
_OFF = [0]
for _s in IN_SIZES:
    _OFF.append(_OFF[-1] + _s)
(O_AQKV, O_AZ, O_AB, O_AA, O_BQ, O_BK, O_BV, O_BG, O_CX, O_CG, O_DQ, O_DK, O_DV, O_DG, O_MG, O_END) = _OFF

LANE = 128
U_MG, U_AZ, U_AQKV, U_BQ, U_BG, U_CX, U_CG, U_DQ, U_DG, U_BK, U_BV = (
    0, 64, 72, 96, 120, 128, 136, 144, 152, 160, 166)
N_UNITS = 172
D_INP = N_UNITS * LANE
T_DK, T_DV, T_AB = 0, 1, 2
D_TAIL = 3 * LANE
IN_TN = 512
_TILE_SRC = tuple(seg0 + IN_TN * t for seg0, width in (
    (O_MG, O_END - O_MG), (O_AZ, O_AB - O_AZ), (O_AQKV, O_AZ - O_AQKV), (O_BQ, O_BK - O_BQ), (O_BG, O_CX - O_BG),
    (O_CX, O_CG - O_CX), (O_CG, O_DQ - O_CG), (O_DQ, O_DK - O_DQ), (O_DG, O_MG - O_DG), (O_BK, O_BG - O_BK))
    for t in range(width // IN_TN))
_SRC_ALIGN = 2 * GDN_HEADS
assert len(_TILE_SRC) * IN_TN == D_INP and all(s % _SRC_ALIGN == 0 for s in _TILE_SRC)
GDN_CHUNK = 128
VMEM_LIMIT = 56 * 1024 * 1024


def _cparams(sem):
    return pltpu.CompilerParams(dimension_semantics=sem, vmem_limit_bytes=VMEM_LIMIT)


def _sigmoid(x):
    return 0.5 * jnp.tanh(0.5 * x) + 0.5


def _silu(x):
    return x * _sigmoid(x)


def _softplus(x):
    return jnp.maximum(x, 0.0) + jnp.log1p(jnp.exp(-jnp.abs(x)))


def _dot(a, b):
    return jnp.dot(a.astype(BF16), b.astype(BF16), preferred_element_type=F32)


def _dot_nt(a, b):
    return lax.dot_general(a.astype(BF16), b.astype(BF16), (((1,), (1,)), ((), ())),
                           preferred_element_type=F32)


def _dot_hi(a, b):
    return jnp.dot(a, b, preferred_element_type=F32, precision=lax.Precision.HIGHEST)


def _split2(a):
    hi = a.astype(BF16)
    return hi, (a - hi.astype(F32)).astype(BF16)


def _dot3(a, b):
    ah, al = _split2(a)
    bh, bl = _split2(b)
    return (jnp.dot(ah, bh, preferred_element_type=F32) + jnp.dot(ah, bl, preferred_element_type=F32)
            + jnp.dot(al, bh, preferred_element_type=F32))


def _dot_exact_lhs(a, b):
    a = a.astype(BF16)
    b0 = b.astype(BF16)
    r1 = b - b0.astype(F32)
    b1 = r1.astype(BF16)
    b2 = (r1 - b1.astype(F32)).astype(BF16)
    return (jnp.dot(a, b0, preferred_element_type=F32) + jnp.dot(a, b1, preferred_element_type=F32)
            + jnp.dot(a, b2, preferred_element_type=F32))


def _dot_nt_hi(a, b):
    return lax.dot_general(a, b, (((1,), (1,)), ((), ())), preferred_element_type=F32,
                           precision=lax.Precision.HIGHEST)


def _in_proj_kernel(tab_ref, x_ref, xs_ref, w_ref, o_ref, os_ref):
    del tab_ref
    w = w_ref[0].astype(BF16)
    o_ref[...] = _dot_nt(x_ref[...], w)

    @pl.when(pl.program_id(0) == 0)
    def _():
        os_ref[...] = _dot_nt(xs_ref[...], w)


def _in_proj(x, xs, wt, layer, tm):
    m, k = x.shape
    ns = xs.shape[0]
    nj = D_INP // IN_TN
    grid_spec = pltpu.PrefetchScalarGridSpec(
        num_scalar_prefetch=1,
        grid=(m // tm, nj),
        in_specs=[pl.BlockSpec((tm, k), lambda i, j, tab: (i, 0)),
                  pl.BlockSpec((ns, k), lambda i, j, tab: (0, 0)),
                  pl.BlockSpec((pl.Element(1), pl.Element(IN_TN), pl.Element(k)),
                               lambda i, j, tab: (layer, tab[j] * _SRC_ALIGN, 0))],
        out_specs=(pl.BlockSpec((tm, IN_TN), lambda i, j, tab: (i, j)),
                   pl.BlockSpec((ns, IN_TN), lambda i, j, tab: (0, jnp.where(i == 0, j, nj - 1)))))
    return pl.pallas_call(
        _in_proj_kernel,
        out_shape=(jax.ShapeDtypeStruct((m, D_INP), F32), jax.ShapeDtypeStruct((ns, D_INP), F32)),
        grid_spec=grid_spec,
        compiler_params=_cparams(("arbitrary", "arbitrary")),
        name="in_proj",
    )(jnp.asarray([s // _SRC_ALIGN for s in _TILE_SRC], jnp.int32), x, xs, wt)


def _in_tail_kernel(x_ref, wkv_ref, wab_ref, o_ref):
    x = x_ref[...]
    o_ref[:, 0:2 * LANE] = _dot_nt(x, wkv_ref[0])
    o_ref[:, 2 * LANE:] = _dot_nt(x, wab_ref[0])


def _in_proj_tail(x, wt, layer, tm):
    m, k = x.shape
    return pl.pallas_call(
        _in_tail_kernel,
        out_shape=jax.ShapeDtypeStruct((m, D_TAIL), F32),
        grid=(m // tm,),
        in_specs=[pl.BlockSpec((tm, k), lambda i: (i, 0)),
                  pl.BlockSpec((pl.Element(1), pl.Element(2 * LANE), pl.Element(k)), lambda i: (layer, O_DK, 0)),
                  pl.BlockSpec((pl.Element(1), pl.Element(LANE), pl.Element(k)), lambda i: (layer, O_AB, 0))],
        out_specs=pl.BlockSpec((tm, D_TAIL), lambda i: (i, 0)),
        compiler_params=_cparams(("parallel",)),
        name="in_proj_tail",
    )(x, wt, wt)


def _conv_rows(x, prev8, cw):
    row8 = lax.broadcasted_iota(jnp.int32, (8, x.shape[1]), 0)
    y = x * cw[CONV_W - 1:CONV_W]
    for k in range(1, CONV_W):
        xk = pltpu.roll(x, k, 0)
        fk = pltpu.roll(prev8, k, 0)
        head = jnp.where(row8 < k, fk, xk[0:8])
        xk = jnp.concatenate([head, xk[8:]], axis=0)
        y = y + xk * cw[CONV_W - 1 - k:CONV_W - k]
    return y


def _tri_inv(lmat, ri, ci):
    c = lmat[0].shape[0]
    eye = (ri == ci).astype(F32)
    blk = (ri >> 4) == (ci >> 4)
    d = [jnp.where(blk, l, 0.0) for l in lmat]
    x = [eye - di for di in d]
    p = [_dot(di, di) for di in d]
    x = [xi + _dot(xi, pi) for xi, pi in zip(x, p)]
    p = [_dot(pi, pi) for pi in p]
    x = [xi + _dot(xi, pi) for xi, pi in zip(x, p)]
    p = [_dot(pi, pi) for pi in p]
    x = [xi + _dot(xi, pi) for xi, pi in zip(x, p)]
    sh = 4
    while (1 << sh) < c:
        lower_left = ((ri >> (sh + 1)) == (ci >> (sh + 1))) & (((ri >> sh) & 1) == 1) & (((ci >> sh) & 1) == 0)
        t = [_dot(jnp.where(lower_left, l, 0.0), xi) for l, xi in zip(lmat, x)]
        x = [xi - _dot(xi, ti) for xi, ti in zip(x, t)]
        sh += 1
    return x


def _tri_solve(lmat, rhs, ri, ci):
    tinv = _tri_inv(lmat, ri, ci)
    sol = [_dot(t, r) for t, r in zip(tinv, rhs)]
    resid = [r - s - _dot3(l, s) for r, s, l in zip(rhs, sol, lmat)]
    return [s + _dot(t, r) for s, t, r in zip(sol, tinv, resid)]


def _gdn_kernel(qkv_ref, z_ref, ab_ref, cw_ref, arow_ref, dtb_ref, nw_ref, y_ref, s_ref, prev_ref):
    cidx = pl.program_id(1)
    C = GDN_CHUNK

    @pl.when(cidx == 0)
    def _():
        s_ref[...] = jnp.zeros_like(s_ref)
        prev_ref[...] = jnp.zeros_like(prev_ref)

    x = qkv_ref[...]
    act = _silu(_conv_rows(x, prev_ref[...], cw_ref[...]))
    prev_ref[...] = x[C - 8:C]

    ab = ab_ref[...]
    beta_t = _sigmoid(ab)
    g_t = -arow_ref[...] * _softplus(ab + dtb_ref[...])
    ri = lax.broadcasted_iota(jnp.int32, (C, C), 0)
    ci = lax.broadcasted_iota(jnp.int32, (C, C), 1)
    incl = ri >= ci
    strict = ri > ci
    gc_t = _dot_exact_lhs(incl.astype(F32), g_t)
    gc_tt = gc_t.T
    egc_t = jnp.exp(gc_t)
    nw = nw_ref[...]
    z = z_ref[...]

    heads = range(GDN_HEADS)
    q, k, v, beta, gcc, egc, decay, glc = [], [], [], [], [], [], [], []
    for h in heads:
        qh = act[:, h * GDN_DK:(h + 1) * GDN_DK]
        kh = act[:, (GDN_HEADS + h) * GDN_DK:(GDN_HEADS + h + 1) * GDN_DK]
        q.append(qh * lax.rsqrt(jnp.sum(qh * qh, axis=-1, keepdims=True) + 1e-6) * (GDN_DK ** -0.5))
        k.append(kh * lax.rsqrt(jnp.sum(kh * kh, axis=-1, keepdims=True) + 1e-6))
        v.append(act[:, (2 * GDN_HEADS + h) * GDN_DK:(2 * GDN_HEADS + h + 1) * GDN_DK])
        beta.append(beta_t[:, h:h + 1])
        gcc.append(gc_t[:, GDN_HEADS + h:GDN_HEADS + h + 1])
        gcr = gc_tt[GDN_HEADS + h:GDN_HEADS + h + 1, :]
        egc.append(egc_t[:, GDN_HEADS + h:GDN_HEADS + h + 1])
        glc.append(gc_t[C - 1:C, GDN_HEADS + h:GDN_HEADS + h + 1])
        decay.append(jnp.where(incl, jnp.exp(jnp.where(incl, gcc[h] - gcr, 0.0)), 0.0))
    kb = [k[h] * beta[h] for h in heads]
    lmat = [jnp.where(strict, _dot_nt(kb[h], k[h]) * decay[h], 0.0) for h in heads]
    rhs = [jnp.concatenate([v[h] * beta[h], kb[h] * egc[h]], axis=1) for h in heads]
    sol = _tri_solve(lmat, rhs, ri, ci)
    qk = [_dot_nt(q[h], k[h]) * decay[h] for h in heads]
    s = [s_ref[0, h] for h in heads]
    v_new = [sol[h][:, :GDN_DV] - _dot(sol[h][:, GDN_DV:], s[h]) for h in heads]
    o = [_dot(q[h] * egc[h], s[h]) + _dot(qk[h], v_new[h]) for h in heads]
    for h in heads:
        kd = k[h] * jnp.exp(glc[h] - gcc[h])
        s_ref[0, h] = s[h] * jnp.exp(glc[h]) + _dot(kd.T, v_new[h])
    for h in heads:
        y = o[h] * lax.rsqrt(jnp.mean(o[h] * o[h], axis=-1, keepdims=True) + RMS_EPS) * nw
        y = y * _silu(z[:, h * GDN_DV:(h + 1) * GDN_DV])
        y_ref[:, h * GDN_DV:(h + 1) * GDN_DV] = y.astype(BF16)


def _gdn_prompt(hp, ht, n_seq, t_len, conv_w, arow, dtb, norm_w):
    C = GDN_CHUNK
    nc = t_len // C
    return pl.pallas_call(
        _gdn_kernel,
        out_shape=(jax.ShapeDtypeStruct((n_seq * t_len, BRANCH_W), BF16),
                   jax.ShapeDtypeStruct((n_seq, GDN_HEADS, GDN_DK, GDN_DV), F32)),
        grid=(n_seq, nc),
        in_specs=[pl.BlockSpec((C, GDN_QKV), lambda n, c: (n * nc + c, U_AQKV * LANE // GDN_QKV)),
                  pl.BlockSpec((C, BRANCH_W), lambda n, c: (n * nc + c, U_AZ * LANE // BRANCH_W)),
                  pl.BlockSpec((C, LANE), lambda n, c: (n * nc + c, T_AB)),
                  pl.BlockSpec((CONV_W, GDN_QKV), lambda n, c: (0, 0)),
                  pl.BlockSpec((1, LANE), lambda n, c: (0, 0)),
                  pl.BlockSpec((1, LANE), lambda n, c: (0, 0)),
                  pl.BlockSpec((1, GDN_DV), lambda n, c: (0, 0))],
        out_specs=(pl.BlockSpec((C, BRANCH_W), lambda n, c: (n * nc + c, 0)),
                   pl.BlockSpec((1, GDN_HEADS, GDN_DK, GDN_DV), lambda n, c: (n, 0, 0, 0))),
        scratch_shapes=[pltpu.VMEM((8, GDN_QKV), F32)],
        compiler_params=_cparams(("parallel", "arbitrary")),
        name="gdn_prompt",
    )(hp, hp, ht, conv_w, arow, dtb, norm_w)


def _band_mask(rows, cols, window, first_block):
    qi = lax.broadcasted_iota(jnp.int32, (rows, cols), 0) & (BLOCK - 1)
    kj = lax.broadcasted_iota(jnp.int32, (rows, cols), 1) & (2 * BLOCK - 1)
    off = qi + BLOCK - kj
    kmin = jnp.where(first_block, BLOCK, 0)
    return (off >= 0) & (off <= window) & (kj >= kmin)


def _dil_attn_kernel(*refs, window, dil, n_kv):
    hd, n_g = DIL_HD, DIL_G
    nq = n_kv * n_g
    q_refs = refs[:nq]
    kv_refs = refs[nq:nq + 4 * n_kv]
    bias_ref, o_ref, lse_ref = refs[nq + 4 * n_kv:nq + 4 * n_kv + 3]
    o_scr = refs[nq + 4 * n_kv + 3:]
    rows = n_g * BLOCK
    valid = _band_mask(rows, 2 * BLOCK, window, pl.program_id(1) == 0)
    lane = lax.broadcasted_iota(jnp.int32, (BLOCK, LANE), 1)

    def sub(r):
        return pl.ds(r, BLOCK, stride=dil) if dil > 1 else slice(None)

    bias_m = [jnp.where(valid, bias_ref[j * n_g:(j + 1) * n_g].reshape(rows, 2 * BLOCK), NEG_INF)
              for j in range(n_kv)]
    chains = [(j, r) for r in range(dil) for j in range(n_kv)]
    for c0 in range(0, len(chains), 2):
        grp = chains[c0:c0 + 2]
        s = []
        for j, r in grp:
            kc_ref, kp_ref = kv_refs[4 * j], kv_refs[4 * j + 1]
            qh = jnp.concatenate([q_refs[j * n_g + g][sub(r), :] for g in range(n_g)], axis=0)
            kcat = jnp.concatenate([kp_ref[sub(r), :], kc_ref[sub(r), :]], axis=0)
            s.append(_dot_nt(qh, kcat) * (hd ** -0.5) + bias_m[j])
        m = [jnp.max(sc, axis=-1, keepdims=True) for sc in s]
        p = [jnp.exp(sc - mc) for sc, mc in zip(s, m)]
        den = [jnp.sum(pc, axis=-1, keepdims=True) for pc in p]
        for (j, r), pc, mc, dc in zip(grp, p, m, den):
            vc_ref, vp_ref = kv_refs[4 * j + 2], kv_refs[4 * j + 3]
            vcat = jnp.concatenate([vp_ref[sub(r), :], vc_ref[sub(r), :]], axis=0)
            o = _dot(pc, vcat) / dc
            lse = mc + jnp.log(dc)
            lse_t = jnp.zeros((BLOCK, LANE), F32)
            for g in range(n_g):
                o_scr[j * n_g + g][sub(r), :] = o[g * BLOCK:(g + 1) * BLOCK]
                lse_t = jnp.where(lane == g, lse[g * BLOCK:(g + 1) * BLOCK], lse_t)
            lse_ref[sub(r), j * LANE:(j + 1) * LANE] = lse_t
    for q in range(nq):
        o_ref[:, q * hd:(q + 1) * hd] = o_scr[q][...]


def _rope_slab(x, cos, sin, first):
    quarter = SWA_HD // 2
    return x * cos + jnp.where(first, pltpu.roll(x, LANE - quarter, 1), pltpu.roll(x, quarter, 1)) * sin


def _swa_attn_kernel(q_ref, kc_ref, kp_ref, vc_ref, vp_ref, cc_ref, sc_ref, cp_ref, sp_ref, sink_ref, gate_ref,
                     y_ref, rk_ref):
    half = SWA_HD
    n_slab = SWA_QH // 2
    slab_per_kvh = n_slab // SWA_KVH
    first = (lax.broadcasted_iota(jnp.int32, (BLOCK, LANE), 1) & (SWA_HD - 1)) < SWA_HD // 2
    cos_c, sin_c = cc_ref[...], sc_ref[...]
    k_cur = _rope_slab(kc_ref[...], cos_c, sin_c, first)
    rk_ref[...] = k_cur
    k2 = jnp.concatenate([_rope_slab(kp_ref[...], cp_ref[...], sp_ref[...], first), k_cur], axis=0)
    v2 = jnp.concatenate([vp_ref[...], vc_ref[...]], axis=0)
    lane = lax.broadcasted_iota(jnp.int32, k2.shape, 1)
    lo = lane < half
    k2r = pltpu.roll(k2, half, 1)
    v2r = pltpu.roll(v2, half, 1)
    kk = [jnp.concatenate([jnp.where(lo, k2, 0.0), jnp.where(lo, 0.0, k2r)], axis=0),
          jnp.concatenate([jnp.where(lo, k2r, 0.0), jnp.where(lo, 0.0, k2)], axis=0)]
    vv = [jnp.concatenate([jnp.where(lo, v2, 0.0), jnp.where(lo, 0.0, v2r)], axis=0),
          jnp.concatenate([jnp.where(lo, v2r, 0.0), jnp.where(lo, 0.0, v2)], axis=0)]
    kk = [a.astype(BF16) for a in kk]
    vv = [a.astype(BF16) for a in vv]
    valid = _band_mask(BLOCK, 4 * BLOCK, SWA_WINDOW, pl.program_id(1) == 0)
    sink = sink_ref[...]
    lane_o = lax.broadcasted_iota(jnp.int32, (BLOCK, LANE), 1) < half
    slabs = range(n_slab)
    s = []
    for a in slabs:
        qa = _rope_slab(q_ref[:, a * LANE:(a + 1) * LANE], cos_c, sin_c, first) * (SWA_HD ** -0.5)
        sa = lax.dot_general(qa.astype(BF16), kk[a // slab_per_kvh],
                             (((1,), (1,)), ((), ())), preferred_element_type=F32)
        s.append(jnp.where(valid, sa, NEG_INF))
    heads = range(SWA_QH)
    sh = [s[h // 2][:, (h % 2) * 2 * BLOCK:(h % 2 + 1) * 2 * BLOCK] for h in heads]
    snk = [sink[:, h:h + 1] for h in heads]
    m = [jnp.maximum(jnp.max(sh[h], axis=-1, keepdims=True), snk[h]) for h in heads]
    pe = [jnp.exp(sh[h] - m[h]) for h in heads]
    dh = [jnp.sum(pe[h], axis=-1, keepdims=True) + jnp.exp(snk[h] - m[h]) for h in heads]
    for a in slabs:
        p = jnp.concatenate([pe[2 * a], pe[2 * a + 1]], axis=1).astype(BF16)
        den = jnp.where(lane_o, dh[2 * a], dh[2 * a + 1])
        o = jnp.dot(p, vv[a // slab_per_kvh], preferred_element_type=F32) / den
        y_ref[:, a * LANE:(a + 1) * LANE] = (o * _silu(gate_ref[:, a * LANE:(a + 1) * LANE])).astype(BF16)


def _row_specs(rows, nblk, width, cb):
    cur = pl.BlockSpec((rows, width), lambda n, i, *_: (n * nblk + i, cb))
    prev = pl.BlockSpec((rows, width), lambda n, i, *_: (n * nblk + jnp.maximum(i - 1, 0), cb))
    return cur, prev


def _dil_attn(hp, gi, n_seq, t_len, bias):
    win, dil = DIL_GROUPS[gi]
    n_kv = DIL_KVH if dil == 1 else 1
    nq = n_kv * DIL_G
    rows = BLOCK * dil
    nblk = t_len // rows
    m = n_seq * t_len

    def head_spec(unit0, j, back):
        return pl.BlockSpec((rows, DIL_HD),
                            lambda n, i, h: (n * nblk + jnp.maximum(i - back, 0), unit0 + h * n_kv + j))

    qspecs = [pl.BlockSpec((rows, DIL_HD), functools.partial(
        lambda n, i, h, q: (n * nblk + i, U_BQ + gi * DIL_QH + h * nq + q), q=q)) for q in range(nq)]
    kvspecs = []
    for j in range(n_kv):
        kvspecs += [head_spec(U_BK + gi * DIL_KVH, j, 0), head_spec(U_BK + gi * DIL_KVH, j, 1),
                    head_spec(U_BV + gi * DIL_KVH, j, 0), head_spec(U_BV + gi * DIL_KVH, j, 1)]
    return pl.pallas_call(
        functools.partial(_dil_attn_kernel, window=win // dil, dil=dil, n_kv=n_kv),
        out_shape=(jax.ShapeDtypeStruct((m, DIL_QH * DIL_HD), F32),
                   jax.ShapeDtypeStruct((m, DIL_KVH * LANE), F32)),
        grid=(n_seq, nblk, DIL_KVH // n_kv),
        in_specs=qspecs + kvspecs + [pl.BlockSpec((nq, BLOCK, 2 * BLOCK), lambda n, i, h: (h, 0, 0))],
        out_specs=(pl.BlockSpec((rows, nq * DIL_HD), lambda n, i, h: (n * nblk + i, h)),
                   pl.BlockSpec((rows, n_kv * LANE), lambda n, i, h: (n * nblk + i, h))),
        scratch_shapes=[pltpu.VMEM((rows, DIL_HD), F32)] * nq,
        compiler_params=_cparams(("parallel", "arbitrary", "arbitrary")), name=f"dil_attn{gi}",
    )(*([hp] * (nq + 4 * n_kv)), bias)


def _swa_attn(hp, ht, cos_t, sin_t, sink_row, n_seq, t_len):
    wq, wk = SWA_QH * SWA_HD, SWA_KVH * SWA_HD
    nblk = t_len // BLOCK
    m = n_seq * t_len
    qspec, _ = _row_specs(BLOCK, nblk, wq, U_DQ * LANE // wq)
    kcur, kprev = _row_specs(BLOCK, nblk, wk, T_DK)
    vcur, vprev = _row_specs(BLOCK, nblk, wk, T_DV)
    gspec, _ = _row_specs(BLOCK, nblk, wq, U_DG * LANE // wq)
    tcur = pl.BlockSpec((BLOCK, LANE), lambda n, i: (i, 0))
    tprev = pl.BlockSpec((BLOCK, LANE), lambda n, i: (jnp.maximum(i - 1, 0), 0))
    return pl.pallas_call(
        _swa_attn_kernel,
        out_shape=(jax.ShapeDtypeStruct((m, wq), BF16), jax.ShapeDtypeStruct((m, wk), F32)),
        grid=(n_seq, nblk),
        in_specs=[qspec, kcur, kprev, vcur, vprev, tcur, tcur, tprev, tprev,
                  pl.BlockSpec(sink_row.shape, lambda n, i: (0, 0)), gspec],
        out_specs=(pl.BlockSpec((BLOCK, wq), lambda n, i: (n * nblk + i, 0)),
                   pl.BlockSpec((BLOCK, wk), lambda n, i: (n * nblk + i, 0))),
        compiler_params=_cparams(("parallel", "arbitrary")), name="swa_attn",
    )(hp, ht, ht, ht, ht, cos_t, sin_t, cos_t, sin_t, sink_row, hp)


def _dil_merge_kernel(o0_ref, o1_ref, o2_ref, l0_ref, l1_ref, l2_ref, g_ref, y_ref):
    l0 = l0_ref[...]
    l1 = l1_ref[...]
    l2 = l2_ref[...]
    m = jnp.maximum(jnp.maximum(l0, l1), l2)
    w0 = jnp.exp(l0 - m)
    w1 = jnp.exp(l1 - m)
    w2 = jnp.exp(l2 - m)
    inv = 1.0 / (w0 + w1 + w2)
    w0 = w0 * inv
    w1 = w1 * inv
    w2 = w2 * inv
    for h in range(DIL_QH):
        sl = slice(h * DIL_HD, (h + 1) * DIL_HD)
        c = (h // DIL_G) * LANE + h % DIL_G
        o = (w0[:, c:c + 1] * o0_ref[:, sl] + w1[:, c:c + 1] * o1_ref[:, sl] + w2[:, c:c + 1] * o2_ref[:, sl])
        y_ref[:, sl] = (o * _silu(g_ref[:, sl])).astype(BF16)


def _dil_merge(o0, o1, o2, l0, l1, l2, hp, tm):
    m = o0.shape[0]
    ospec = pl.BlockSpec((tm, BRANCH_W), lambda i: (i, 0))
    lspec = pl.BlockSpec((tm, DIL_KVH * LANE), lambda i: (i, 0))
    return pl.pallas_call(
        _dil_merge_kernel,
        out_shape=jax.ShapeDtypeStruct((m, BRANCH_W), BF16),
        grid=(m // tm,),
        in_specs=[ospec, ospec, ospec, lspec, lspec, lspec,
                  pl.BlockSpec((tm, BRANCH_W), lambda i: (i, U_BG * LANE // BRANCH_W))],
        out_specs=ospec,
        compiler_params=_cparams(("parallel",)),
        name="dil_merge",
    )(o0, o1, o2, l0, l1, l2, hp)


def _lru_gates(cx, wa_ref, wx_ref, ba, bx, lam):
    ra, rx = [], []
    for b in range(LRU_BLOCKS):
        xb = cx[:, b * LRU_BS:(b + 1) * LRU_BS].astype(BF16)
        ra.append(jnp.dot(xb, wa_ref[b], preferred_element_type=F32))
        rx.append(jnp.dot(xb, wx_ref[b], preferred_element_type=F32))
    r = _sigmoid(jnp.concatenate(ra, axis=1) + ba)
    ig = _sigmoid(jnp.concatenate(rx, axis=1) + bx)
    log_a = -LRU_C * r * _softplus(-lam)
    a = jnp.exp(log_a)
    th = jnp.tanh(log_a)
    bterm = jnp.sqrt(-2.0 * th / (1.0 - th)) * (ig * cx)
    return a, bterm


def _lru_kernel(x_ref, g_ref, cw_ref, cb_ref, wa_ref, wx_ref, ba_ref, bx_ref, lam_ref,
                y_ref, hl_ref, prev_ref, h_ref, a_s, b_s):
    tb = x_ref.shape[0]

    @pl.when(pl.program_id(1) == 0)
    def _():
        prev_ref[...] = jnp.zeros_like(prev_ref)
        h_ref[...] = jnp.zeros_like(h_ref)

    x = x_ref[...]
    cx = _conv_rows(x, prev_ref[...], cw_ref[...]) + cb_ref[...]
    prev_ref[...] = x[tb - 8:tb]
    a, bterm = _lru_gates(cx, wa_ref, wx_ref, ba_ref[...], bx_ref[...], lam_ref[...])
    a_s[...] = a
    b_s[...] = bterm

    row8 = lax.broadcasted_iota(jnp.int32, (SUBLANES, LRU_W), 0)

    def step(g, h):
        sl = pl.ds(pl.multiple_of(g * SUBLANES, SUBLANES), SUBLANES)
        a = a_s[sl, :]
        b = b_s[sl, :]
        for d in (1, 2, 4):
            keep = row8 >= d
            b = b + a * jnp.where(keep, pltpu.roll(b, d, 0), 0.0)
            a = a * jnp.where(keep, pltpu.roll(a, d, 0), 1.0)
        hs = a * h + b
        b_s[sl, :] = hs
        return hs[SUBLANES - 1:SUBLANES]

    h = lax.fori_loop(0, tb // SUBLANES, step, h_ref[...], unroll=2)
    h_ref[...] = h
    hl_ref[0] = h
    y_ref[...] = (b_s[...] * _silu(g_ref[...])).astype(BF16)


def _lru_prompt(hp, n_seq, t_len, conv_w, conv_b, wa, wx, ba, bx, lam, tb):
    nb = t_len // tb
    vec = pl.BlockSpec((1, LRU_W), lambda n, c: (0, 0))
    wspec = pl.BlockSpec((LRU_BLOCKS, LRU_BS, LRU_BS), lambda n, c: (0, 0, 0))
    return pl.pallas_call(
        _lru_kernel,
        out_shape=(jax.ShapeDtypeStruct((n_seq * t_len, LRU_W), BF16),
                   jax.ShapeDtypeStruct((n_seq, 1, LRU_W), F32)),
        grid=(n_seq, nb),
        in_specs=[pl.BlockSpec((tb, LRU_W), lambda n, c: (n * nb + c, U_CX * LANE // LRU_W)),
                  pl.BlockSpec((tb, LRU_W), lambda n, c: (n * nb + c, U_CG * LANE // LRU_W)),
                  pl.BlockSpec((CONV_W, LRU_W), lambda n, c: (0, 0)),
                  vec, wspec, wspec, vec, vec, vec],
        out_specs=(pl.BlockSpec((tb, LRU_W), lambda n, c: (n * nb + c, 0)),
                   pl.BlockSpec((1, 1, LRU_W), lambda n, c: (n, 0, 0))),
        scratch_shapes=[pltpu.VMEM((8, LRU_W), F32), pltpu.VMEM((1, LRU_W), F32),
                        pltpu.VMEM((tb, LRU_W), F32), pltpu.VMEM((tb, LRU_W), F32)],
        compiler_params=_cparams(("parallel", "arbitrary")),
        name="lru_prompt",
    )(hp, hp, conv_w, conv_b, wa, wx, ba, bx, lam)


def _branch_kernel(ya_ref, yb_ref, yc_ref, yd_ref, g0_ref, g1_ref, g2_ref, g3_ref, wb_ref, o_ref, wbf_ref):
    i, j = pl.program_id(0), pl.program_id(1)

    @pl.when(i == 0)
    def _():
        wbf_ref[j] = wb_ref[...].astype(BF16)

    acc = None
    for b, (y_ref, g_ref) in enumerate(((ya_ref, g0_ref), (yb_ref, g1_ref), (yc_ref, g2_ref), (yd_ref, g3_ref))):
        br = jnp.dot(y_ref[...], wbf_ref[j, b], preferred_element_type=F32)
        term = _sigmoid(g_ref[...]) * br
        acc = term if acc is None else acc + term
    o_ref[...] = acc.astype(BF16)


def _branch_merge(ys, hp, w_branch, layer, tm, tn):
    m = ys[0].shape[0]
    nj = D_MODEL // tn
    yspec = pl.BlockSpec((tm, BRANCH_W), lambda i, j: (i, 0))
    gspecs = [pl.BlockSpec((tm, tn), functools.partial(lambda i, j, b: (i, U_MG * LANE // tn + b * nj + j), b=b))
              for b in range(N_BRANCH)]
    wspec = pl.BlockSpec((None, N_BRANCH, BRANCH_W, tn),
                         lambda i, j: (layer, 0, 0, jnp.where(i == 0, j, nj - 1)), pipeline_mode=pl.Buffered(1))
    return pl.pallas_call(
        _branch_kernel,
        out_shape=jax.ShapeDtypeStruct((m, D_MODEL), BF16),
        grid=(m // tm, nj),
        in_specs=[yspec, yspec, yspec, yspec] + gspecs + [wspec],
        out_specs=pl.BlockSpec((tm, tn), lambda i, j: (i, j)),
        scratch_shapes=[pltpu.VMEM((nj, N_BRANCH, BRANCH_W, tn), BF16)],
        compiler_params=_cparams(("arbitrary", "arbitrary")),
        name="branch_merge",
    )(*ys, hp, hp, hp, hp, w_branch)


def _out_kernel(m_ref, x_ref, w_ref, g_ref, b_ref, o_ref, ob_ref, wbf_ref):
    @pl.when(pl.program_id(0) == 0)
    def _():
        wbf_ref[...] = w_ref[...].astype(BF16)

    f = jnp.dot(m_ref[...], wbf_ref[...], preferred_element_type=F32)
    z = DN_ALPHA * x_ref[...] + f
    mu = jnp.mean(z, axis=-1, keepdims=True)
    zc = z - mu
    var = jnp.mean(zc * zc, axis=-1, keepdims=True)
    y = zc * lax.rsqrt(var + LN_EPS) * g_ref[...] + b_ref[...]
    o_ref[...] = y
    ob_ref[...] = y.astype(BF16)


def _out_proj(merged, x, w_out, layer, ln_g, ln_b, tm):
    m = x.shape[0]
    vec = pl.BlockSpec((1, D_MODEL), lambda i: (0, 0))
    ospec = pl.BlockSpec((tm, D_MODEL), lambda i: (i, 0))
    return pl.pallas_call(
        _out_kernel,
        out_shape=(jax.ShapeDtypeStruct((m, D_MODEL), F32), jax.ShapeDtypeStruct((m, D_MODEL), BF16)),
        grid=(m // tm,),
        in_specs=[pl.BlockSpec((tm, D_MODEL), lambda i: (i, 0)),
                  pl.BlockSpec((tm, D_MODEL), lambda i: (i, 0)),
                  pl.BlockSpec((None, D_MODEL, D_MODEL), lambda i: (layer, 0, 0), pipeline_mode=pl.Buffered(1)),
                  vec, vec],
        out_specs=(ospec, ospec),
        scratch_shapes=[pltpu.VMEM((D_MODEL, D_MODEL), BF16)],
        compiler_params=_cparams(("arbitrary",)),
        name="out_proj",
    )(merged, x, w_out, ln_g, ln_b)


def _gdn_dec_pre_kernel(x_ref, b0_ref, b1_ref, b2_ref, cw_ref, ab_ref, arow_ref, dtb_ref,
                        qkv_ref, beta_ref, eg_ref):
    cw = cw_ref[...]
    y = b0_ref[...] * cw[0:1] + b1_ref[...] * cw[1:2] + b2_ref[...] * cw[2:3] + x_ref[...] * cw[3:4]
    act = _silu(y)
    for h in range(GDN_HEADS):
        sq = slice(h * GDN_DK, (h + 1) * GDN_DK)
        sk = slice((GDN_HEADS + h) * GDN_DK, (GDN_HEADS + h + 1) * GDN_DK)
        q = act[:, sq]
        k = act[:, sk]
        qkv_ref[:, sq] = q * lax.rsqrt(jnp.sum(q * q, axis=-1, keepdims=True) + 1e-6) * (GDN_DK ** -0.5)
        qkv_ref[:, sk] = k * lax.rsqrt(jnp.sum(k * k, axis=-1, keepdims=True) + 1e-6)
    qkv_ref[:, 2 * GDN_HEADS * GDN_DK:] = act[:, 2 * GDN_HEADS * GDN_DK:]
    ab = ab_ref[...]
    beta_ref[...] = _sigmoid(ab)
    eg_ref[...] = jnp.exp(-arow_ref[...] * _softplus(ab + dtb_ref[...]))


def _gdn_dec_pre(x, b0, b1, b2, conv_w, ab, arow, dtb):
    n = x.shape[0]
    full = lambda a: pl.BlockSpec(a.shape, lambda i: (0,) * a.ndim)
    ins = (x, b0, b1, b2, conv_w, ab, arow, dtb)
    return pl.pallas_call(
        _gdn_dec_pre_kernel,
        out_shape=(jax.ShapeDtypeStruct((n, GDN_QKV), F32), jax.ShapeDtypeStruct((n, LANE), F32),
                   jax.ShapeDtypeStruct((n, LANE), F32)),
        grid=(1,),
        in_specs=[full(a) for a in ins],
        out_specs=(pl.BlockSpec((n, GDN_QKV), lambda i: (0, 0)), pl.BlockSpec((n, LANE), lambda i: (0, 0)),
                   pl.BlockSpec((n, LANE), lambda i: (0, 0))),
        compiler_params=_cparams(("arbitrary",)),
        name="gdn_dec_pre",
    )(*ins)


def _gdn_dec_kernel(qt_ref, kt_ref, v_ref, z_ref, beta_ref, eg_ref, nw_ref, s_ref, so_ref, y_ref):
    nw = nw_ref[...]
    chains = [(t, h) for t in range(qt_ref.shape[0]) for h in range(GDN_HEADS)]
    qt = [qt_ref[t] for t in range(qt_ref.shape[0])]
    kt = [kt_ref[t] for t in range(qt_ref.shape[0])]
    s = [s_ref[t, h] * eg_ref[t][:, GDN_HEADS + h:GDN_HEADS + h + 1] for t, h in chains]
    kcol = [kt[t][:, h:h + 1] for t, h in chains]
    v_old = [jnp.sum(sc * kc, axis=0, keepdims=True) for sc, kc in zip(s, kcol)]
    delta = [(v_ref[t][h:h + 1, :] - vo) * beta_ref[t][:, h:h + 1] for (t, h), vo in zip(chains, v_old)]
    s = [sc + kc * dc for sc, kc, dc in zip(s, kcol, delta)]
    o = [jnp.sum(sc * qt[t][:, h:h + 1], axis=0, keepdims=True) for (t, h), sc in zip(chains, s)]
    for (t, h), sc, oc in zip(chains, s, o):
        so_ref[t, h] = sc
        y = oc * lax.rsqrt(jnp.mean(oc * oc, axis=-1, keepdims=True) + RMS_EPS) * nw
        y_ref[t, h:h + 1, :] = y * _silu(z_ref[t][h:h + 1, :])


def _gdn_dec(qt, kt, v, z, beta, eg, norm_w, state, layer, tb=2):
    n = qt.shape[0]
    tspec = pl.BlockSpec((tb, GDN_DK, GDN_HEADS), lambda i: (i, 0, 0))
    hspec = pl.BlockSpec((tb, GDN_HEADS, GDN_DV), lambda i: (i, 0, 0))
    rspec = pl.BlockSpec((tb, 1, LANE), lambda i: (i, 0, 0))
    return pl.pallas_call(
        _gdn_dec_kernel,
        out_shape=(jax.ShapeDtypeStruct((n, GDN_HEADS, GDN_DK, GDN_DV), F32),
                   jax.ShapeDtypeStruct((n, GDN_HEADS, GDN_DV), F32)),
        grid=(n // tb,),
        in_specs=[tspec, tspec, hspec, hspec, rspec, rspec,
                  pl.BlockSpec((1, GDN_DV), lambda i: (0, 0)),
                  pl.BlockSpec((None, tb, GDN_HEADS, GDN_DK, GDN_DV), lambda i: (layer, i, 0, 0, 0))],
        out_specs=(pl.BlockSpec((tb, GDN_HEADS, GDN_DK, GDN_DV), lambda i: (i, 0, 0, 0)), hspec),
        compiler_params=_cparams(("parallel",)),
        name="gdn_dec",
    )(qt, kt, v, z, beta, eg, norm_w, state)


def _dec_attend(problems, n_g, scale):
    hq = problems[0][0].shape[0]
    row = lax.broadcasted_iota(jnp.int32, (hq, 1), 0)
    n_kv = len(problems[0][1])
    sel = [(row >= kvh * n_g) & (row < (kvh + 1) * n_g) for kvh in range(n_kv)]

    def pick(parts):
        out = parts[0]
        for kvh in range(1, n_kv):
            out = jnp.where(sel[kvh], parts[kvh], out)
        return out

    s_parts = [[_dot_nt_hi(q, kc[kvh]) * scale for kvh in range(n_kv)] for q, kc, *_ in problems]
    soft = []
    for (q, kc, vc, knew, vnew, bias_c, bias_n, sink), parts in zip(problems, s_parts):
        sc = pick(parts)
        kn = pick([jnp.broadcast_to(k1, q.shape) for k1 in knew])
        sn = jnp.sum(q * kn, axis=-1, keepdims=True) * scale
        if bias_c is not None:
            sc = sc + bias_c
            sn = sn + bias_n
        m = jnp.maximum(jnp.max(sc, axis=-1, keepdims=True), sn)
        if sink is not None:
            m = jnp.maximum(m, sink)
        pc = jnp.exp(sc - m)
        pn = jnp.exp(sn - m)
        den = jnp.sum(pc, axis=-1, keepdims=True) + pn
        if sink is not None:
            den = den + jnp.exp(sink - m)
        soft.append((pc, pn, den, m))
    o_parts = [[_dot_hi(pc, vc[kvh]) for kvh in range(n_kv)] for (q, kc, vc, *_), (pc, *_) in zip(problems, soft)]
    res = []
    for (q, kc, vc, knew, vnew, *_), (pc, pn, den, m), parts in zip(problems, soft, o_parts):
        vn = pick([jnp.broadcast_to(v1, q.shape) for v1 in vnew])
        res.append(((pick(parts) + pn * vn) / den, m + jnp.log(den)))
    return res


def _dil_dec_kernel(q_ref, kn_ref, vn_ref, g_ref, c0_ref, c1_ref, c2_ref, bc_ref, bn_ref, y_ref):
    tb = q_ref.shape[0]
    problems = []
    kvw = DIL_KVH * DIL_HD
    for t in range(tb):
        q_all = q_ref[t]
        kn_all = kn_ref[t]
        vn_all = vn_ref[t]
        for gi, c_ref in enumerate((c0_ref, c1_ref, c2_ref)):
            q = q_all[gi * DIL_QH:(gi + 1) * DIL_QH]
            kc = [c_ref[t, :, kvh * DIL_HD:(kvh + 1) * DIL_HD] for kvh in range(DIL_KVH)]
            vc = [c_ref[t, :, kvw + kvh * DIL_HD:kvw + (kvh + 1) * DIL_HD] for kvh in range(DIL_KVH)]
            knew = [kn_all[gi * DIL_KVH + kvh:gi * DIL_KVH + kvh + 1] for kvh in range(DIL_KVH)]
            vnew = [vn_all[gi * DIL_KVH + kvh:gi * DIL_KVH + kvh + 1] for kvh in range(DIL_KVH)]
            problems.append((q, kc, vc, knew, vnew, bc_ref[gi], bn_ref[gi], None))
    res = _dec_attend(problems, DIL_G, DIL_HD ** -0.5)
    for t in range(tb):
        outs = [o for o, _ in res[t * N_DIL:(t + 1) * N_DIL]]
        lses = [l for _, l in res[t * N_DIL:(t + 1) * N_DIL]]
        m = jnp.maximum(jnp.maximum(lses[0], lses[1]), lses[2])
        ws = [jnp.exp(l - m) for l in lses]
        inv = 1.0 / (ws[0] + ws[1] + ws[2])
        o = (ws[0] * outs[0] + ws[1] * outs[1] + ws[2] * outs[2]) * inv
        y_ref[t] = o * _silu(g_ref[t])


def _dil_dec(q3, kn3, vn3, g3, views, layer, bias_c, bias_n, tb=2):
    n = q3.shape[0]
    cspecs = [pl.BlockSpec((None, tb) + v.shape[2:], lambda i: (layer, i, 0, 0)) for v in views]
    return pl.pallas_call(
        _dil_dec_kernel,
        out_shape=jax.ShapeDtypeStruct((n, DIL_QH, DIL_HD), F32),
        grid=(n // tb,),
        in_specs=[pl.BlockSpec((tb, N_DIL * DIL_QH, DIL_HD), lambda i: (i, 0, 0)),
                  pl.BlockSpec((tb, N_DIL * DIL_KVH, DIL_HD), lambda i: (i, 0, 0)),
                  pl.BlockSpec((tb, N_DIL * DIL_KVH, DIL_HD), lambda i: (i, 0, 0)),
                  pl.BlockSpec((tb, DIL_QH, DIL_HD), lambda i: (i, 0, 0))] + cspecs +
                 [pl.BlockSpec(bias_c.shape, lambda i: (0, 0, 0)),
                  pl.BlockSpec(bias_n.shape, lambda i: (0, 0, 0))],
        out_specs=pl.BlockSpec((tb, DIL_QH, DIL_HD), lambda i: (i, 0, 0)),
        compiler_params=_cparams(("parallel",)),
        name="dil_dec",
    )(q3, kn3, vn3, g3, *views, bias_c, bias_n)


def _swa_dec_kernel(q_ref, kn_ref, vn_ref, g_ref, c_ref, sink_ref, cos_ref, sin_ref, perm_ref, y_ref, rk_ref):
    cos = cos_ref[...]
    sin = sin_ref[...]
    perm = perm_ref[...]
    q = q_ref[0]
    kn = kn_ref[0]
    q = q * cos + _dot_hi(q, perm) * sin
    kn = kn * cos + _dot_hi(kn, perm) * sin
    rk_ref[0] = kn
    vn = vn_ref[0]
    kvw = SWA_KVH * SWA_HD
    kc = [c_ref[:, kvh * SWA_HD:(kvh + 1) * SWA_HD] for kvh in range(SWA_KVH)]
    vc = [c_ref[:, kvw + kvh * SWA_HD:kvw + (kvh + 1) * SWA_HD] for kvh in range(SWA_KVH)]
    knew = [kn[kvh:kvh + 1] for kvh in range(SWA_KVH)]
    vnew = [vn[kvh:kvh + 1] for kvh in range(SWA_KVH)]
    (o, _), = _dec_attend([(q, kc, vc, knew, vnew, None, None, sink_ref[...])], SWA_G, SWA_HD ** -0.5)
    y_ref[0] = o * _silu(g_ref[0])


def _swa_dec(q3, kn3, vn3, g3, view, layer, sink_col, cos_d, sin_d, perm):
    n = q3.shape[0]
    win, kvw2 = view.shape[2:]
    return pl.pallas_call(
        _swa_dec_kernel,
        out_shape=(jax.ShapeDtypeStruct((n, SWA_QH, SWA_HD), F32),
                   jax.ShapeDtypeStruct((n, SWA_KVH, SWA_HD), F32)),
        grid=(n,),
        in_specs=[pl.BlockSpec((1, SWA_QH, SWA_HD), lambda i: (i, 0, 0)),
                  pl.BlockSpec((1, SWA_KVH, SWA_HD), lambda i: (i, 0, 0)),
                  pl.BlockSpec((1, SWA_KVH, SWA_HD), lambda i: (i, 0, 0)),
                  pl.BlockSpec((1, SWA_QH, SWA_HD), lambda i: (i, 0, 0)),
                  pl.BlockSpec((None, None, win, kvw2), lambda i: (layer, i, 0, 0)),
                  pl.BlockSpec(sink_col.shape, lambda i: (0, 0)),
                  pl.BlockSpec(cos_d.shape, lambda i: (0, 0)),
                  pl.BlockSpec(sin_d.shape, lambda i: (0, 0)),
                  pl.BlockSpec(perm.shape, lambda i: (0, 0))],
        out_specs=(pl.BlockSpec((1, SWA_QH, SWA_HD), lambda i: (i, 0, 0)),
                   pl.BlockSpec((1, SWA_KVH, SWA_HD), lambda i: (i, 0, 0))),
        compiler_params=_cparams(("parallel",)),
        name="swa_dec",
    )(q3, kn3, vn3, g3, view, sink_col, cos_d, sin_d, perm)


def _lru_dec_kernel(x_ref, g_ref, b0_ref, b1_ref, b2_ref, h0_ref, cw_ref, cb_ref, wa_ref, wx_ref,
                    ba_ref, bx_ref, lam_ref, y_ref, h_ref):
    cw = cw_ref[...]
    cx = (b0_ref[...] * cw[0:1] + b1_ref[...] * cw[1:2] + b2_ref[...] * cw[2:3] + x_ref[...] * cw[3:4]
          + cb_ref[...])
    a, bterm = _lru_gates(cx, wa_ref, wx_ref, ba_ref[...], bx_ref[...], lam_ref[...])
    h = a * h0_ref[...] + bterm
    h_ref[...] = h
    y_ref[...] = h * _silu(g_ref[...])


def _lru_dec(x, g, b0, b1, b2, h0, conv_w, conv_b, wa, wx, ba, bx, lam):
    n = x.shape[0]
    ins = (x, g, b0, b1, b2, h0, conv_w, conv_b, wa, wx, ba, bx, lam)
    full = lambda a: pl.BlockSpec(a.shape, lambda i: (0,) * a.ndim)
    return pl.pallas_call(
        _lru_dec_kernel,
        out_shape=(jax.ShapeDtypeStruct((n, LRU_W), F32), jax.ShapeDtypeStruct((n, LRU_W), F32)),
        grid=(1,),
        in_specs=[full(a) for a in ins],
        out_specs=(pl.BlockSpec((n, LRU_W), lambda i: (0, 0)), pl.BlockSpec((n, LRU_W), lambda i: (0, 0))),
        compiler_params=_cparams(("arbitrary",)),
        name="lru_dec",
    )(*ins)


def _rel_bucket(dist):
    max_exact = REL_BUCKETS // 2
    n = dist.astype(F32)
    large = max_exact + (jnp.log(jnp.maximum(n, 1.0) / max_exact) / math.log(REL_MAX_DIST / max_exact)
                         * (REL_BUCKETS - max_exact)).astype(jnp.int32)
    large = jnp.minimum(large, REL_BUCKETS - 1)
    return jnp.where(dist < max_exact, dist, large)


def _offset_bias(rel_bias, gi, win, dil):
    j = win // dil + 1
    b = rel_bias[_rel_bucket(dil * jnp.arange(j, dtype=jnp.int32))]
    return b[:, gi * DIL_QH:(gi + 1) * DIL_QH].astype(F32)


def _rope_tables(pos):
    half = SWA_HD // 2
    inv = ROPE_THETA ** (-jnp.arange(half, dtype=F32) / half)
    ang = pos.astype(F32)[:, None] * inv[None, :]
    c, s = jnp.cos(ang), jnp.sin(ang)
    return jnp.concatenate([c, c], axis=1), jnp.concatenate([-s, s], axis=1)


SUBLANES = 8


def _shift_kernel(c_ref, n_ref, o_ref, *, k):
    rows = c_ref.shape[0]
    shifted = pltpu.roll(c_ref[...], rows - k, 0)
    row8 = lax.broadcasted_iota(jnp.int32, (SUBLANES, LANE), 0)
    o_ref[0:rows - SUBLANES, :] = shifted[0:rows - SUBLANES]
    o_ref[rows - SUBLANES:rows, :] = jnp.where(row8 >= SUBLANES - k, n_ref[...], shifted[rows - SUBLANES:rows])


def _shift_rows_kernel(c_ref, n_ref, o_ref):
    w = c_ref.shape[0]
    o_ref[0:w - 1] = c_ref[1:w]
    o_ref[w - 1:w] = n_ref[...]


def _shift_append(cache, new_rows):
    d, n, w = cache.shape[:3]
    if cache.shape[-1] != LANE:
        inner = cache.shape[3:]
        zeros = (0,) * len(inner)
        return pl.pallas_call(
            _shift_rows_kernel,
            out_shape=jax.ShapeDtypeStruct(cache.shape, cache.dtype),
            grid=(d, n),
            in_specs=[pl.BlockSpec((None, None, w) + inner, lambda a, b: (a, b, 0) + zeros),
                      pl.BlockSpec((None, None, 1) + inner, lambda a, b: (a, b, 0) + zeros)],
            out_specs=pl.BlockSpec((None, None, w) + inner, lambda a, b: (a, b, 0) + zeros),
            compiler_params=_cparams(("parallel", "parallel")),
            name="shift_append_rows",
        )(cache, new_rows)
    k = int(np.prod(cache.shape[3:])) // LANE
    assert k < SUBLANES
    new8 = jnp.concatenate([jnp.zeros((d, n, SUBLANES - k, LANE), cache.dtype), new_rows.reshape(d, n, k, LANE)],
                           axis=2)
    out = pl.pallas_call(
        functools.partial(_shift_kernel, k=k),
        out_shape=jax.ShapeDtypeStruct((d, n, w * k, LANE), cache.dtype),
        grid=(d, n),
        in_specs=[pl.BlockSpec((None, None, w * k, LANE), lambda a, b: (a, b, 0, 0)),
                  pl.BlockSpec((None, None, SUBLANES, LANE), lambda a, b: (a, b, 0, 0))],
        out_specs=pl.BlockSpec((None, None, w * k, LANE), lambda a, b: (a, b, 0, 0)),
        compiler_params=_cparams(("parallel", "parallel")),
        name="shift_append",
    )(cache.reshape(d, n, w * k, LANE), new8)
    return out.reshape(cache.shape)


def _cols(h, unit, width):
    return h[..., unit * LANE:unit * LANE + width]


def kernel(x_prompt, x_sample, state_gdn, state_gdn_conv, cache_dil_w128, cache_dil_w512, cache_dil_w2048,
           cache_swa, state_rglru, state_rglru_conv, w_in, gdn_conv_w, gdn_a_log, gdn_dt_bias, gdn_norm_w,
           lru_conv_w, lru_conv_b, lru_wa, lru_ba, lru_wx, lru_bx, lru_lambda, swa_sink, rel_bias, w_branch,
           w_out, ln_g, ln_b):
    nb, t_len, _ = x_prompt.shape
    ns = x_sample.shape[0]
    mp = nb * t_len
    caches = (cache_dil_w128, cache_dil_w512, cache_dil_w2048)
    xp = x_prompt.reshape(mp, D_MODEL)
    xs = x_sample.reshape(ns, D_MODEL)

    cos64, sin64 = _rope_tables(jnp.arange(t_len))
    cos_t = jnp.concatenate([cos64, cos64], axis=1)
    sin_t = jnp.concatenate([sin64, sin64], axis=1)
    cos_d, sin_d = _rope_tables(jnp.full((1,), PAST_LEN))
    perm = jnp.asarray(np.roll(np.eye(SWA_HD, dtype=np.float32), SWA_HD // 2, axis=0))
    qi = jnp.arange(BLOCK)[:, None]
    kj = jnp.arange(2 * BLOCK)[None, :]
    bias_p, bias_c, bias_n = [], [], []
    for gi, (win, dil) in enumerate(DIL_GROUPS):
        ob = _offset_bias(rel_bias, gi, win, dil)
        jw = win // dil
        onehot = (jnp.clip(qi + BLOCK - kj, 0, jw)[:, :, None] == jnp.arange(jw + 1)).astype(F32)
        bias_p.append(jnp.einsum('qkj,jh->hqk', onehot, ob, precision=lax.Precision.HIGHEST))
        bias_c.append(jnp.transpose(ob[::-1][:jw], (1, 0)))
        bias_n.append(ob[0][:, None])
    bias_c = jnp.stack(bias_c)
    bias_n = jnp.stack(bias_n)
    dil_views = [c[:, :, ::dil].reshape(DEPTH, ns, win // dil, 2 * DIL_KVH * DIL_HD)
                 for (win, dil), c in zip(DIL_GROUPS, caches)]
    swa_view = cache_swa.reshape(DEPTH, ns, cache_swa.shape[2], 2 * SWA_KVH * SWA_HD)
    wt = jnp.swapaxes(w_in, 1, 2)
    xp_b, xs_b = xp.astype(BF16), xs.astype(BF16)

    new_p = [[] for _ in range(8)]
    new_s = [[] for _ in range(8)]
    for l in range(DEPTH):
        wa = lru_wa[l].astype(BF16)
        wx = lru_wx[l].astype(BF16)
        zpad = jnp.zeros((1, LANE - 2 * GDN_HEADS), F32)
        arow = jnp.concatenate([jnp.zeros((1, GDN_HEADS), F32), jnp.exp(gdn_a_log[l])[None], zpad], axis=1)
        dtb = jnp.concatenate([jnp.zeros((1, GDN_HEADS), F32), gdn_dt_bias[l][None], zpad], axis=1)
        norm_w = gdn_norm_w[l][None]
        conv_b = lru_conv_b[l][None]
        ba, bx, lam = lru_ba[l][None], lru_bx[l][None], lru_lambda[l][None]
        lng, lnb = ln_g[l][None], ln_b[l][None]
        sink = swa_sink[l].astype(F32)

        hp, hs = _in_proj(xp_b, xs_b, wt, l, 2048)
        ht = _in_proj_tail(xp_b, wt, l, 2048)
        hp3 = hp.reshape(nb, t_len, D_INP)
        ht3 = ht.reshape(nb, t_len, D_TAIL)
        ya, s_p = _gdn_prompt(hp, ht, nb, t_len, gdn_conv_w[l], arow, dtb, norm_w)
        outs, lses = [], []
        for gi in range(N_DIL):
            o_g, l_g = _dil_attn(hp, gi, nb, t_len, bias_p[gi])
            outs.append(o_g)
            lses.append(l_g)
        yb = _dil_merge(outs[0], outs[1], outs[2], lses[0], lses[1], lses[2], hp, 512)
        yc, h_p = _lru_prompt(hp, nb, t_len, lru_conv_w[l], conv_b, wa, wx, ba, bx, lam, 512)
        sink_row = jnp.concatenate([sink[None], jnp.zeros((1, LANE - SWA_QH), F32)], axis=1)
        yd, rk = _swa_attn(hp, ht, cos_t, sin_t, sink_row, nb, t_len)
        merged = _branch_merge((ya, yb, yc, yd), hp, w_branch, l, 512, 512)
        xp_new, xp_b_new = _out_proj(merged, xp, w_out, l, lng, lnb, 512)

        new_p[0].append(s_p)
        new_p[1].append(_cols(hp3, U_AQKV, GDN_QKV)[:, t_len - (CONV_W - 1):])
        for gi, (win, dil) in enumerate(DIL_GROUPS):
            kk = _cols(hp3, U_BK + gi * DIL_KVH, DIL_KVH * DIL_HD)[:, t_len - win:]
            vv = _cols(hp3, U_BV + gi * DIL_KVH, DIL_KVH * DIL_HD)[:, t_len - win:]
            new_p[2 + gi].append(jnp.stack([kk, vv], axis=2).reshape(nb, win, 2, DIL_KVH, DIL_HD))
        kk = rk.reshape(nb, t_len, LANE)[:, t_len - SWA_WINDOW:]
        vv = _cols(ht3, T_DV, LANE)[:, t_len - SWA_WINDOW:]
        new_p[5].append(jnp.stack([kk, vv], axis=2).reshape(nb, SWA_WINDOW, 2, SWA_KVH, SWA_HD))
        new_p[6].append(h_p.reshape(nb, LRU_W))
        new_p[7].append(_cols(hp3, U_CX, LRU_W)[:, t_len - (CONV_W - 1):])

        hst = _in_proj_tail(xs_b, wt, l, ns)
        gbuf = state_gdn_conv[l]
        a_qkv_s = _cols(hs, U_AQKV, GDN_QKV)
        qkv_n, beta_s, eg_s = _gdn_dec_pre(a_qkv_s, gbuf[:, 0], gbuf[:, 1], gbuf[:, 2], gdn_conv_w[l],
                                           _cols(hst, T_AB, LANE), arow, dtb)
        qkv4 = qkv_n.reshape(ns, 3, GDN_HEADS, GDN_DK)
        s_s, ya_s = _gdn_dec(jnp.swapaxes(qkv4[:, 0], 1, 2), jnp.swapaxes(qkv4[:, 1], 1, 2), qkv4[:, 2],
                             _cols(hs, U_AZ, BRANCH_W).reshape(ns, GDN_HEADS, GDN_DV),
                             beta_s[:, None], eg_s[:, None], norm_w, state_gdn, l)
        bk_s = _cols(hs, U_BK, N_DIL * DIL_KVH * DIL_HD)
        bv_s = _cols(hs, U_BV, N_DIL * DIL_KVH * DIL_HD)
        yb_s = _dil_dec(_cols(hs, U_BQ, N_DIL * BRANCH_W).reshape(ns, N_DIL * DIL_QH, DIL_HD),
                        bk_s.reshape(ns, N_DIL * DIL_KVH, DIL_HD), bv_s.reshape(ns, N_DIL * DIL_KVH, DIL_HD),
                        _cols(hs, U_BG, BRANCH_W).reshape(ns, DIL_QH, DIL_HD), dil_views, l, bias_c, bias_n)
        lbuf = state_rglru_conv[l]
        cx_s = _cols(hs, U_CX, LRU_W)
        yc_s, h_s = _lru_dec(cx_s, _cols(hs, U_CG, LRU_W), lbuf[:, 0], lbuf[:, 1], lbuf[:, 2], state_rglru[l],
                             lru_conv_w[l], conv_b, wa, wx, ba, bx, lam)
        dv_s = _cols(hst, T_DV, LANE)
        yd_s, rk_s = _swa_dec(_cols(hs, U_DQ, BRANCH_W).reshape(ns, SWA_QH, SWA_HD),
                              _cols(hst, T_DK, LANE).reshape(ns, SWA_KVH, SWA_HD),
                              dv_s.reshape(ns, SWA_KVH, SWA_HD),
                              _cols(hs, U_DG, BRANCH_W).reshape(ns, SWA_QH, SWA_HD),
                              swa_view, l, sink.reshape(SWA_QH, 1), cos_d, sin_d, perm)
        ys_s = (ya_s.reshape(ns, BRANCH_W).astype(BF16), yb_s.reshape(ns, BRANCH_W).astype(BF16),
                yc_s.astype(BF16), yd_s.reshape(ns, BRANCH_W).astype(BF16))
        merged_s = _branch_merge(ys_s, hs, w_branch, l, ns, 512)
        xs_new, xs_b_new = _out_proj(merged_s, xs, w_out, l, lng, lnb, ns)

        new_s[0].append(s_s)
        new_s[1].append(jnp.concatenate([gbuf[:, 1:], a_qkv_s[:, None]], axis=1))
        for gi in range(N_DIL):
            kk = bk_s[:, gi * DIL_KVH * DIL_HD:(gi + 1) * DIL_KVH * DIL_HD].reshape(ns, 1, DIL_KVH, DIL_HD)
            vv = bv_s[:, gi * DIL_KVH * DIL_HD:(gi + 1) * DIL_KVH * DIL_HD].reshape(ns, 1, DIL_KVH, DIL_HD)
            new_s[2 + gi].append(jnp.stack([kk, vv], axis=2))
        new_s[5].append(jnp.stack([rk_s[:, None], dv_s.reshape(ns, 1, SWA_KVH, SWA_HD)], axis=2))
        new_s[6].append(h_s)
        new_s[7].append(jnp.concatenate([lbuf[:, 1:], cx_s[:, None]], axis=1))

        xp, xs, xp_b, xs_b = xp_new, xs_new, xp_b_new, xs_b_new

    p = [jnp.stack(v) for v in new_p]
    s = [jnp.stack(new_s[i]) for i in (0, 1)]
    for i, c in zip((2, 3, 4, 5), caches + (cache_swa,)):
        s.append(_shift_append(c, jnp.stack(new_s[i])))
    s += [jnp.stack(new_s[i]) for i in (6, 7)]
    return (xp.reshape(nb, t_len, D_MODEL), xs.reshape(ns, 1, D_MODEL),
            p[0], s[0], p[1], s[1], p[2], s[2], p[3], s[3], p[4], s[4], p[5], s[5], p[6], s[6], p[7], s[7])
```
